```python
import math, functools
import jax, jax.numpy as jnp
from jax import lax
import numpy as np

D_MODEL = 2048
BATCH = 1
SEQ = 8192
DEPTH = 2
DEC_BATCH = 32
DEC_SEQ = 4
PAST_LEN = 8192
PAGE_SIZE = 128

D_A = D_MODEL // 4
D_B = D_MODEL // 4
D_C = D_MODEL - D_A - D_B
N_A_BLOCKS = 4
A_BLOCK = D_A // N_A_BLOCKS
CONV_WIDTH = 4
LRU_C = 8.0
N_RET_HEADS = 4
RET_DV = D_B // N_RET_HEADS
RET_DK = RET_DV // 2
RET_CHUNK = 128
ROPE_BASE = 10000.0
HEAD_DIM_C = 128
N_C_HEADS = D_C // HEAD_DIM_C
N_KV_HEADS = 2
N_IDX_HEADS = 16
IDX_DIM = 64
TOPK_MAX = 256
Q_BLOCK = 128
NUM_BUCKETS = 32
MAX_DISTANCE = 128
D_FF = ((8 * D_MODEL // 3 + 255) // 256) * 256
EPS = 1e-6

PROJ_WIDTHS = (D_A, D_A, N_RET_HEADS * RET_DK, N_RET_HEADS * RET_DK, D_B, D_B,
               D_C, N_KV_HEADS * HEAD_DIM_C, N_KV_HEADS * HEAD_DIM_C,
               N_IDX_HEADS * IDX_DIM, IDX_DIM, N_IDX_HEADS)
D_PROJ = sum(PROJ_WIDTHS)

kernel_name = 'hymba_rglru_retention_dsa_step'

F32 = jnp.float32


def rmsnorm(x, g):
    xf = x.astype(F32)
    y = xf * lax.rsqrt(jnp.mean(xf * xf, axis=-1, keepdims=True) + EPS)
    return (y * g.astype(F32)).astype(x.dtype)


def swiglu(h, w_gate, w_up, w_down):
    return (jax.nn.silu(h @ w_gate) * (h @ w_up)) @ w_down


def rotary(x, pos):
    half = x.shape[-1] // 2
    freqs = ROPE_BASE ** (-jnp.arange(half, dtype=F32) / half)
    ang = pos.astype(F32)[:, None] * freqs[None, :]
    cos = jnp.cos(ang)[None, :, None, :]
    sin = jnp.sin(ang)[None, :, None, :]
    xf = x.astype(F32)
    x1, x2 = xf[..., :half], xf[..., half:]
    return jnp.concatenate([x1 * cos - x2 * sin, x1 * sin + x2 * cos], axis=-1)


def causal_conv(x, buf, w, b):
    T = x.shape[1]
    xp = jnp.concatenate([buf.astype(x.dtype), x], axis=1)
    out = b
    for j in range(w.shape[0]):
        out = out + xp[:, j:j + T] * w[j]
    return out, xp[:, T:]


def rglru(xc, h0, wa, ba, wx, bx, lam):
    B, T, DA = xc.shape
    xb = xc.reshape(B, T, N_A_BLOCKS, A_BLOCK)
    r = jax.nn.sigmoid((jnp.einsum('btnc,ncd->btnd', xb, wa).reshape(B, T, DA) + ba).astype(F32))
    i = jax.nn.sigmoid((jnp.einsum('btnc,ncd->btnd', xb, wx).reshape(B, T, DA) + bx).astype(F32))
    log_a = -LRU_C * r * jax.nn.softplus(-lam.astype(F32))
    a = jnp.exp(log_a)
    u = jnp.sqrt(-jnp.expm1(2.0 * log_a)) * i * xc.astype(F32)
    u = u.at[:, 0].add(a[:, 0] * h0.astype(F32))

    def combine(left, right):
        a1, b1 = left
        a2, b2 = right
        return a1 * a2, a2 * b1 + b2

    _, h = lax.associative_scan(combine, (a, u), axis=1)
    return h, h[:, -1]


def retention(q, k, v, S0, chunk):
    B, T, H, DK = q.shape
    DV = v.shape[-1]
    n = T // chunk
    lg = jnp.log1p(-jnp.exp2(-5.0 - jnp.arange(H, dtype=F32)))
    idx = jnp.arange(chunk, dtype=F32)
    diff = idx[:, None] - idx[None, :]
    dmask = jnp.where(diff[None] >= 0, jnp.exp(jnp.maximum(diff, 0.0)[None] * lg[:, None, None]), 0.0)
    cross_dec = jnp.exp((idx + 1.0)[:, None] * lg[None, :])
    state_dec = jnp.exp((chunk - 1.0 - idx)[:, None] * lg[None, :])
    chunk_dec = jnp.exp(chunk * lg)

    def to_chunks(a):
        return a.reshape((B, n, chunk) + a.shape[2:]).swapaxes(0, 1)

    def step(S, inp):
        qb, kb, vb = inp
        att = jnp.einsum('bnhd,bmhd->bhnm', qb, kb) * dmask[None]
        inner = jnp.einsum('bhnm,bmhe->bnhe', att, vb)
        cross = jnp.einsum('bnhd,bhde->bnhe', qb, S) * cross_dec[None, :, :, None]
        S = S * chunk_dec[None, :, None, None] + jnp.einsum('bmhd,bmhe->bhde', kb * state_dec[None, :, :, None], vb)
        return S, inner + cross

    S, o = lax.scan(step, S0.astype(F32), (to_chunks(q), to_chunks(k), to_chunks(v)))
    return o.swapaxes(0, 1).reshape(B, T, H, DV), S


def t5_bucket(dist):
    n = jnp.maximum(dist, 0)
    max_exact = NUM_BUCKETS // 2
    large = max_exact + (jnp.log(jnp.maximum(n, 1).astype(F32) / max_exact)
                         / math.log(MAX_DISTANCE / max_exact) * (NUM_BUCKETS - max_exact)).astype(jnp.int32)
    large = jnp.minimum(large, NUM_BUCKETS - 1)
    return jnp.where(n < max_exact, n, large)


def dsa_select(qi, w, ki, q_pos, n_keep):
    s = jnp.einsum('bthd,bsd->bths', qi, ki).astype(F32)
    score = jnp.einsum('bths,bth->bts', jax.nn.relu(s), w.astype(F32)) * (N_IDX_HEADS ** -0.5 * IDX_DIM ** -0.5)
    key_pos = jnp.arange(ki.shape[1], dtype=jnp.int32)
    visible = key_pos[None, :] <= q_pos[:, None]
    score = jnp.where(visible[None], score, -jnp.inf)
    return lax.top_k(score, n_keep)[1]


def dsa_attend(q, k_sel, v_sel, idx, q_pos, rel_bias):
    B, T, H, DH = q.shape
    KVH = k_sel.shape[3]
    G = H // KVH
    K = idx.shape[-1]
    qg = q.reshape(B, T, KVH, G, DH)
    logits = jnp.einsum('btcgd,btscd->btcgs', qg, k_sel).astype(F32) * DH ** -0.5
    dist = q_pos[None, :, None] - idx
    bias = rel_bias[t5_bucket(dist)].astype(F32)
    bias = jnp.moveaxis(bias, -1, 2).reshape(B, T, KVH, G, K)
    valid = (dist >= 0)[:, :, None, None, :]
    logits = jnp.where(valid, logits + bias, -jnp.inf)
    p = jax.nn.softmax(logits, axis=-1).astype(v_sel.dtype)
    return jnp.einsum('btcgs,btscd->btcgd', p, v_sel).reshape(B, T, H, DH)


def dsa_prompt(q, k, v, qi, w, ki, rel_bias, n_keep):
    B, S, H, DH = q.shape
    nb = S // Q_BLOCK
    gather = jax.vmap(lambda a, i: a[i])

    def to_blocks(a):
        return a.reshape((B, nb, Q_BLOCK) + a.shape[2:]).swapaxes(0, 1)

    def block(args):
        qb, qib, wb, start = args
        q_pos = start + jnp.arange(Q_BLOCK, dtype=jnp.int32)
        idx = dsa_select(qib, wb, ki, q_pos, n_keep)
        return dsa_attend(qb, gather(k, idx), gather(v, idx), idx, q_pos, rel_bias)

    starts = jnp.arange(nb, dtype=jnp.int32) * Q_BLOCK
    out = lax.map(block, (to_blocks(q), to_blocks(qi), to_blocks(w), starts))
    return out.swapaxes(0, 1).reshape(B, S, H, DH)


def dsa_sample(q, k, v, qi, w, ki, pool_k, pool_v, pool_ki, page_table, rel_bias):
    B, T = q.shape[0], q.shape[1]
    P = pool_k.shape[1]
    past = page_table.shape[1] * P
    ki_past = pool_ki[page_table].reshape(B, past, ki.shape[-1]).astype(ki.dtype)
    ki_all = jnp.concatenate([ki_past, ki], axis=1)
    n_keep = min(TOPK_MAX, (past + T) // 4)
    q_pos = past + jnp.arange(T, dtype=jnp.int32)
    idx = dsa_select(qi, w, ki_all, q_pos, n_keep)
    in_past = (idx < past)[..., None, None]
    ic = jnp.minimum(idx, past - 1)
    pages = jax.vmap(lambda pt, i: pt[i])(page_table, ic // P)
    offs = ic % P
    new_i = jnp.clip(idx - past, 0, T - 1)
    gather = jax.vmap(lambda a, i: a[i])
    k_sel = jnp.where(in_past, pool_k[pages, offs].astype(k.dtype), gather(k, new_i))
    v_sel = jnp.where(in_past, pool_v[pages, offs].astype(v.dtype), gather(v, new_i))
    return dsa_attend(q, k_sel, v_sel, idx, q_pos, rel_bias)


def mixer(h, pos, conv0, h0, S0, attn_fn, w_in, w_out, conv_w, conv_b, lru_wa, lru_ba, lru_wx, lru_bx, lru_lambda):
    B, T = h.shape[0], h.shape[1]
    dt = h.dtype
    u = h @ w_in
    xa, ga, qb, kb, vb, gb, qc, kc, vc, qi, ki, wi = jnp.split(u, np.cumsum(PROJ_WIDTHS)[:-1].tolist(), axis=-1)
    xa_c, conv_new = causal_conv(xa, conv0, conv_w, conv_b)
    ha, h_last = rglru(xa_c, h0, lru_wa, lru_ba, lru_wx, lru_bx, lru_lambda)
    ya = (ha * jax.nn.gelu(ga.astype(F32))).astype(dt)
    q_r = rotary(qb.reshape(B, T, N_RET_HEADS, RET_DK), pos)
    k_r = rotary(kb.reshape(B, T, N_RET_HEADS, RET_DK), pos) * RET_DK ** -0.5
    ob, S_last = retention(q_r, k_r, vb.reshape(B, T, N_RET_HEADS, RET_DV).astype(F32), S0, min(RET_CHUNK, T))
    ob = ob * lax.rsqrt(jnp.mean(ob * ob, axis=-1, keepdims=True) + EPS)
    yb = (ob.reshape(B, T, D_B) * jax.nn.silu(gb.astype(F32))).astype(dt)
    kc4 = kc.reshape(B, T, N_KV_HEADS, HEAD_DIM_C)
    vc4 = vc.reshape(B, T, N_KV_HEADS, HEAD_DIM_C)
    yc = attn_fn(qc.reshape(B, T, N_C_HEADS, HEAD_DIM_C), kc4, vc4,
                 qi.reshape(B, T, N_IDX_HEADS, IDX_DIM), wi, ki).reshape(B, T, D_C)
    y = jnp.concatenate([ya, yb, yc], axis=-1) @ w_out
    return y, (kc4, vc4, ki, h_last.astype(dt), conv_new, S_last.astype(dt))


def layer(x, pos, conv0, h0, S0, attn_fn, n1, f1g, f1u, f1d, nm, w_in, w_out, conv_w, conv_b,
          lru_wa, lru_ba, lru_wx, lru_bx, lru_lambda, n2, f2g, f2u, f2d):
    x = x + 0.5 * swiglu(rmsnorm(x, n1), f1g, f1u, f1d)
    y, st = mixer(rmsnorm(x, nm), pos, conv0, h0, S0, attn_fn, w_in, w_out, conv_w, conv_b,
                  lru_wa, lru_ba, lru_wx, lru_bx, lru_lambda)
    x = x + y
    x = x + 0.5 * swiglu(rmsnorm(x, n2), f2g, f2u, f2d)
    return x, st


def setup_inputs(seed: int = 0) -> dict:
    key = jax.random.key(seed)
    ks = jax.random.split(key, 32)
    n_pages = PAST_LEN // PAGE_SIZE
    n_used = DEC_BATCH * n_pages
    n_pool = n_used + max(1, n_used // 4)

    def nrm(k, shape, scale):
        return jax.random.normal(k, shape, F32) * scale

    u = jax.random.uniform(ks[13], (DEPTH, D_A), F32, minval=0.9, maxval=0.999)
    a = u ** (1.0 / LRU_C)
    return {
        'x_prompt': nrm(ks[0], (BATCH, SEQ, D_MODEL), 1.0),
        'x_sample': nrm(ks[1], (DEC_BATCH, DEC_SEQ, D_MODEL), 1.0),
        'cache_k': nrm(ks[2], (DEPTH, n_pool, PAGE_SIZE, N_KV_HEADS, HEAD_DIM_C), 1.0),
        'cache_v': nrm(ks[3], (DEPTH, n_pool, PAGE_SIZE, N_KV_HEADS, HEAD_DIM_C), 1.0),
        'cache_kidx': nrm(ks[4], (DEPTH, n_pool, PAGE_SIZE, IDX_DIM), 1.0),
        'page_table': jax.random.permutation(ks[5], n_pool)[:n_used].reshape(DEC_BATCH, n_pages).astype(jnp.int32),
        'state_lru_h': nrm(ks[6], (DEPTH, DEC_BATCH, D_A), 0.5),
        'state_conv': nrm(ks[7], (DEPTH, DEC_BATCH, CONV_WIDTH - 1, D_A), 1.0),
        'state_ret': nrm(ks[8], (DEPTH, DEC_BATCH, N_RET_HEADS, RET_DK, RET_DV), 0.3),
        'norm_ffn1': 1.0 + nrm(ks[9], (DEPTH, D_MODEL), 0.02),
        'ffn1_gate': nrm(ks[10], (DEPTH, D_MODEL, D_FF), D_MODEL ** -0.5),
        'ffn1_up': nrm(ks[11], (DEPTH, D_MODEL, D_FF), D_MODEL ** -0.5),
        'ffn1_down': nrm(ks[12], (DEPTH, D_FF, D_MODEL), D_FF ** -0.5),
        'norm_mix': 1.0 + nrm(ks[14], (DEPTH, D_MODEL), 0.02),
        'w_in': nrm(ks[15], (DEPTH, D_MODEL, D_PROJ), D_MODEL ** -0.5),
        'w_out': nrm(ks[16], (DEPTH, D_MODEL, D_MODEL), D_MODEL ** -0.5),
        'conv_w': nrm(ks[17], (DEPTH, CONV_WIDTH, D_A), CONV_WIDTH ** -0.5),
        'conv_b': nrm(ks[18], (DEPTH, D_A), 0.01),
        'lru_wa': nrm(ks[19], (DEPTH, N_A_BLOCKS, A_BLOCK, A_BLOCK), A_BLOCK ** -0.5),
        'lru_ba': nrm(ks[20], (DEPTH, D_A), 0.01),
        'lru_wx': nrm(ks[21], (DEPTH, N_A_BLOCKS, A_BLOCK, A_BLOCK), A_BLOCK ** -0.5),
        'lru_bx': nrm(ks[22], (DEPTH, D_A), 0.01),
        'lru_lambda': jnp.log(a) - jnp.log1p(-a),
        'rel_bias': nrm(ks[23], (NUM_BUCKETS, N_C_HEADS), 0.5),
        'norm_ffn2': 1.0 + nrm(ks[24], (DEPTH, D_MODEL), 0.02),
        'ffn2_gate': nrm(ks[25], (DEPTH, D_MODEL, D_FF), D_MODEL ** -0.5),
        'ffn2_up': nrm(ks[26], (DEPTH, D_MODEL, D_FF), D_MODEL ** -0.5),
        'ffn2_down': nrm(ks[27], (DEPTH, D_FF, D_MODEL), D_FF ** -0.5),
        'norm_final': 1.0 + nrm(ks[28], (D_MODEL,), 0.02),
    }


def reference(x_prompt, x_sample, cache_k, cache_v, cache_kidx, page_table, state_lru_h, state_conv, state_ret,
              norm_ffn1, ffn1_gate, ffn1_up, ffn1_down, norm_mix, w_in, w_out, conv_w, conv_b,
              lru_wa, lru_ba, lru_wx, lru_bx, lru_lambda, rel_bias,
              norm_ffn2, ffn2_gate, ffn2_up, ffn2_down, norm_final):
    B, S = x_prompt.shape[0], x_prompt.shape[1]
    T = x_sample.shape[1]
    past = page_table.shape[1] * cache_k.shape[2]
    dt = x_prompt.dtype
    pos_p = jnp.arange(S, dtype=jnp.int32)
    pos_s = past + jnp.arange(T, dtype=jnp.int32)
    conv0_p = jnp.zeros((B, CONV_WIDTH - 1, D_A), dt)
    h0_p = jnp.zeros((B, D_A), dt)
    S0_p = jnp.zeros((B, N_RET_HEADS, RET_DK, RET_DV), dt)
    attn_p = functools.partial(dsa_prompt, rel_bias=rel_bias, n_keep=min(TOPK_MAX, S // 4))
    xp, xs = x_prompt, x_sample
    sts_p, sts_s = [], []
    for l in range(DEPTH):
        lw = (norm_ffn1[l], ffn1_gate[l], ffn1_up[l], ffn1_down[l], norm_mix[l], w_in[l], w_out[l],
              conv_w[l], conv_b[l], lru_wa[l], lru_ba[l], lru_wx[l], lru_bx[l], lru_lambda[l],
              norm_ffn2[l], ffn2_gate[l], ffn2_up[l], ffn2_down[l])
        attn_s = functools.partial(dsa_sample, pool_k=cache_k[l], pool_v=cache_v[l], pool_ki=cache_kidx[l],
                                   page_table=page_table, rel_bias=rel_bias)
        xp, st_p = layer(xp, pos_p, conv0_p, h0_p, S0_p, attn_p, *lw)
        xs, st_s = layer(xs, pos_s, state_conv[l], state_lru_h[l], state_ret[l], attn_s, *lw)
        sts_p.append(st_p)
        sts_s.append(st_s)
    y_prompt = rmsnorm(xp, norm_final)
    y_sample = rmsnorm(xs, norm_final)
    k_prompt = jnp.stack([st[0] for st in sts_p])
    v_prompt = jnp.stack([st[1] for st in sts_p])
    kidx_prompt = jnp.stack([st[2] for st in sts_p])
    lru_h_prompt = jnp.stack([st[3] for st in sts_p])
    conv_prompt = jnp.stack([st[4] for st in sts_p])
    ret_prompt = jnp.stack([st[5] for st in sts_p])
    k_sample = jnp.stack([st[0] for st in sts_s])
    v_sample = jnp.stack([st[1] for st in sts_s])
    kidx_sample = jnp.stack([st[2] for st in sts_s])
    lru_h_sample = jnp.stack([st[3] for st in sts_s])
    conv_sample = jnp.stack([st[4] for st in sts_s])
    ret_sample = jnp.stack([st[5] for st in sts_s])
    return (y_prompt, y_sample, k_prompt, v_prompt, kidx_prompt, lru_h_prompt, conv_prompt, ret_prompt,
            k_sample, v_sample, kidx_sample, lru_h_sample, conv_sample, ret_sample)
```

```python
import functools
import math

import numpy as np
import jax
import jax.numpy as jnp
from jax import lax
from jax.experimental import pallas as pl
from jax.experimental.pallas import tpu as pltpu

F32 = jnp.float32
BF16 = jnp.bfloat16
I32 = jnp.int32

EPS = 1e-6
N_A_BLOCKS = 4
CONV_WIDTH = 4
LRU_C = 8.0
N_RET_HEADS = 4
ROPE_BASE = 10000.0
HEAD_DIM_C = 128
N_KV_HEADS = 2
N_IDX_HEADS = 16
IDX_DIM = 64
TOPK_MAX = 256
NUM_BUCKETS = 32
MAX_DISTANCE = 128
SAMPLE_PAD_T = 8
PAGES_PER_STEP = 8
NEG_BIG = -1e30
INT_MIN = -2 ** 31

VMEM_LIMIT = 56 * 1024 * 1024


def _cparams(sem):
    return pltpu.CompilerParams(dimension_semantics=sem, vmem_limit_bytes=VMEM_LIMIT)


def _rms(x, g):
    return x * lax.rsqrt(jnp.mean(x * x, axis=-1, keepdims=True) + EPS) * g


def _dot(a, b):
    return jnp.dot(a, b, preferred_element_type=F32)


def _dot_nt(a, b):
    return lax.dot_general(a, b, (((1,), (1,)), ((), ())), preferred_element_type=F32)


def _dot_tn(a, b):
    return lax.dot_general(a, b, (((0,), (0,)), ((), ())), preferred_element_type=F32)


def _sort_key(score):
    bits = lax.bitcast_convert_type(score, I32)
    return jnp.where(bits < 0, bits ^ jnp.int32(0x7FFFFFFF), bits)


def _ffn_body(x_ref, g_ref, wg_ref, wu_ref, wd_ref, *rest, final_norm):
    if final_norm:
        gf_ref, o_ref, on_ref, h_ref = rest
    else:
        o_ref, h_ref = rest
    j = pl.program_id(1)

    @pl.when(j == 0)
    def _():
        h_ref[...] = _rms(x_ref[...], g_ref[...]).astype(BF16)
        o_ref[...] = jnp.zeros_like(o_ref)

    h = h_ref[...]
    g = _dot(h, wg_ref[...])
    u = _dot(h, wu_ref[...])
    a = (g * jax.nn.sigmoid(g) * u).astype(BF16)
    o_ref[...] += _dot(a, wd_ref[...])

    @pl.when(j == pl.num_programs(1) - 1)
    def _():
        y = x_ref[...] + 0.5 * o_ref[...]
        o_ref[...] = y
        if final_norm:
            on_ref[...] = _rms(y, gf_ref[...])


def _ffn(x, g, wg, wu, wd, gf=None, *, tm, tf):
    R, D = x.shape
    FF = wg.shape[1]
    final_norm = gf is not None
    in_specs = [
        pl.BlockSpec((tm, D), lambda i, j: (i, 0)),
        pl.BlockSpec((1, D), lambda i, j: (0, 0)),
        pl.BlockSpec((D, tf), lambda i, j: (0, j)),
        pl.BlockSpec((D, tf), lambda i, j: (0, j)),
        pl.BlockSpec((tf, D), lambda i, j: (j, 0)),
    ]
    args = [x, g.reshape(1, D), wg, wu, wd]
    out_shape = [jax.ShapeDtypeStruct((R, D), F32)]
    out_specs = [pl.BlockSpec((tm, D), lambda i, j: (i, 0))]
    if final_norm:
        in_specs.append(pl.BlockSpec((1, D), lambda i, j: (0, 0)))
        args.append(gf.reshape(1, D))
        out_shape.append(jax.ShapeDtypeStruct((R, D), F32))
        out_specs.append(pl.BlockSpec((tm, D), lambda i, j: (i, 0)))
    res = pl.pallas_call(
        functools.partial(_ffn_body, final_norm=final_norm),
        grid=(R // tm, FF // tf),
        in_specs=in_specs,
        out_specs=out_specs,
        out_shape=out_shape,
        scratch_shapes=[pltpu.VMEM((tm, D), BF16)],
        compiler_params=_cparams(("parallel", "arbitrary")),
        name="ffn",
    )(*args)
    return res if final_norm else res[0]


def _nmm_body(x_ref, g_ref, w_ref, *rest, out_f32, out_bf16, colscale):
    rest = list(rest)
    cs_ref = rest.pop(0) if colscale else None
    of_ref = rest.pop(0) if out_f32 else None
    ob_ref = rest.pop(0) if out_bf16 else None
    h_ref = rest.pop(0)

    @pl.when(pl.program_id(1) == 0)
    def _():
        h_ref[...] = _rms(x_ref[...], g_ref[...]).astype(BF16)

    r = _dot(h_ref[...], w_ref[...])
    if colscale:
        r = r * cs_ref[...]
    if out_f32:
        of_ref[...] = r
    if out_bf16:
        ob_ref[...] = r.astype(BF16)


def _norm_matmul(x, g, w, colscale=None, *, tm, tn, out_f32, out_bf16):
    R, D = x.shape
    N = w.shape[1]
    in_specs = [
        pl.BlockSpec((tm, D), lambda i, j: (i, 0)),
        pl.BlockSpec((1, D), lambda i, j: (0, 0)),
        pl.BlockSpec((D, tn), lambda i, j: (0, j)),
    ]
    args = [x, g.reshape(1, D), w]
    if colscale is not None:
        in_specs.append(pl.BlockSpec((1, tn), lambda i, j: (0, j)))
        args.append(colscale.reshape(1, N))
    out_shape, out_specs = [], []
    for flag, dt in ((out_f32, F32), (out_bf16, BF16)):
        if flag:
            out_shape.append(jax.ShapeDtypeStruct((R, N), dt))
            out_specs.append(pl.BlockSpec((tm, tn), lambda i, j: (i, j)))
    res = pl.pallas_call(
        functools.partial(_nmm_body, out_f32=out_f32, out_bf16=out_bf16, colscale=colscale is not None),
        grid=(R // tm, N // tn),
        in_specs=in_specs,
        out_specs=out_specs,
        out_shape=out_shape,
        scratch_shapes=[pltpu.VMEM((tm, D), BF16)],
        compiler_params=_cparams(("parallel", "arbitrary")),
        name="norm_matmul",
    )(*args)
    return res


def _oproj_body(x_ref, ya_ref, yb_ref, yc_ref, wa_ref, wb_ref, wc_ref, o_ref):
    o_ref[...] = (x_ref[...] + _dot(ya_ref[...], wa_ref[...]) + _dot(yb_ref[...], wb_ref[...])
                  + _dot(yc_ref[...], wc_ref[...]))


def _out_proj(x, ya, yb, yc, wa, wb, wc, *, tm, tn):
    R, D = x.shape
    row = lambda w: pl.BlockSpec((tm, w), lambda i, j: (i, 0))
    col = lambda k: pl.BlockSpec((k, tn), lambda i, j: (0, j))
    return pl.pallas_call(
        _oproj_body,
        grid=(R // tm, D // tn),
        in_specs=[pl.BlockSpec((tm, tn), lambda i, j: (i, j)), row(ya.shape[1]), row(yb.shape[1]), row(yc.shape[1]),
                  col(wa.shape[0]), col(wb.shape[0]), col(wc.shape[0])],
        out_specs=pl.BlockSpec((tm, tn), lambda i, j: (i, j)),
        out_shape=jax.ShapeDtypeStruct((R, D), F32),
        compiler_params=_cparams(("parallel", "arbitrary")),
        name="out_proj",
    )(x, ya, yb, yc, wa, wb, wc)


def _lru_body(xa_ref, ga_ref, conv0_ref, h0_ref, cw_ref, cb_ref, wa_ref, ba_ref, wx_ref, bx_ref, lam_ref,
              ya_ref, hl_ref, cn_ref, xs_ref, a_ref, u_ref, hs_ref, hc_ref, *, tc, t_valid_last):
    t = pl.program_id(1)
    da = xa_ref.shape[-1]
    blk = da // N_A_BLOCKS
    tail = 8

    @pl.when(t == 0)
    def _():
        xs_ref[0:tail, :] = conv0_ref[0]
        hc_ref[0:1, :] = h0_ref[0]

    xs_ref[tail:tail + tc, :] = xa_ref[0]
    xc = cb_ref[...]
    for j in range(CONV_WIDTH):
        off = tail - (CONV_WIDTH - 1) + j
        xc = xc + xs_ref[off:off + tc, :] * cw_ref[j:j + 1, :]
    xcb = xc.astype(BF16)
    pre_r = jnp.concatenate([_dot(xcb[:, n * blk:(n + 1) * blk], wa_ref[n]) for n in range(N_A_BLOCKS)], axis=1)
    pre_i = jnp.concatenate([_dot(xcb[:, n * blk:(n + 1) * blk], wx_ref[n]) for n in range(N_A_BLOCKS)], axis=1)
    r = jax.nn.sigmoid(pre_r + ba_ref[...])
    gi = jax.nn.sigmoid(pre_i + bx_ref[...])
    z = -lam_ref[...]
    softplus = jnp.maximum(z, 0.0) + jnp.log1p(jnp.exp(-jnp.abs(z)))
    log_a = (-LRU_C) * r * softplus
    a = jnp.exp(log_a)
    a_ref[...] = a
    u_ref[...] = jnp.sqrt(-jnp.tanh(log_a) * (a * a + 1.0)) * gi * xc

    def step(k, h):
        base = pl.multiple_of(k * 8, 8)
        for rr in range(8):
            h = a_ref[pl.ds(base + rr, 1), :] * h + u_ref[pl.ds(base + rr, 1), :]
            hs_ref[pl.ds(base + rr, 1), :] = h
        return h

    h = lax.fori_loop(0, tc // 8, step, hc_ref[0:1, :])
    hc_ref[0:1, :] = h
    ya_ref[0] = (hs_ref[...] * jax.nn.gelu(ga_ref[0])).astype(ya_ref.dtype)

    @pl.when(t == pl.num_programs(1) - 1)
    def _():
        hl_ref[0] = hs_ref[t_valid_last - 1:t_valid_last, :]
        lo = tail - (CONV_WIDTH - 1) + t_valid_last
        cn_ref[0] = xs_ref[lo:lo + CONV_WIDTH - 1, :]

    xs_ref[0:tail, :] = xs_ref[tc:tc + tail, :]


def _lru(ua, conv0, h0, cw, cb, wa, ba, wx, bx, lam, *, tc, t_valid_last):
    B, T, da2 = ua.shape
    da = da2 // 2
    vec = lambda: pl.BlockSpec((1, da), lambda b, t: (0, 0))
    blk = da // N_A_BLOCKS
    return pl.pallas_call(
        functools.partial(_lru_body, tc=tc, t_valid_last=t_valid_last),
        grid=(B, T // tc),
        in_specs=[
            pl.BlockSpec((1, tc, da), lambda b, t: (b, t, 0)),
            pl.BlockSpec((1, tc, da), lambda b, t: (b, t, 1)),
            pl.BlockSpec((1, 8, da), lambda b, t: (b, 0, 0)),
            pl.BlockSpec((1, 1, da), lambda b, t: (b, 0, 0)),
            pl.BlockSpec((CONV_WIDTH, da), lambda b, t: (0, 0)),
            vec(),
            pl.BlockSpec((N_A_BLOCKS, blk, blk), lambda b, t: (0, 0, 0)),
            vec(),
            pl.BlockSpec((N_A_BLOCKS, blk, blk), lambda b, t: (0, 0, 0)),
            vec(),
            vec(),
        ],
        out_specs=[
            pl.BlockSpec((1, tc, da), lambda b, t: (b, t, 0)),
            pl.BlockSpec((1, 1, da), lambda b, t: (b, 0, 0)),
            pl.BlockSpec((1, CONV_WIDTH - 1, da), lambda b, t: (b, 0, 0)),
        ],
        out_shape=[
            jax.ShapeDtypeStruct((B, T, da), BF16),
            jax.ShapeDtypeStruct((B, 1, da), F32),
            jax.ShapeDtypeStruct((B, CONV_WIDTH - 1, da), F32),
        ],
        scratch_shapes=[
            pltpu.VMEM((tc + 8, da), F32),
            pltpu.VMEM((tc, da), F32),
            pltpu.VMEM((tc, da), F32),
            pltpu.VMEM((tc, da), F32),
            pltpu.VMEM((8, da), F32),
        ],
        compiler_params=_cparams(("arbitrary", "arbitrary")),
        name="rglru",
    )(ua, ua, conv0, h0, cw, cb.reshape(1, da), wa, ba.reshape(1, da), wx, bx.reshape(1, da), lam.reshape(1, da))


def _ret_body(q_ref, k_ref, v_ref, gb_ref, cos_ref, sin_ref, s0_ref, yb_ref, sl_ref, s_ref, *, cp, c_valid, mm_dtype):
    t = pl.program_id(1)
    dk = q_ref.shape[-1] // N_RET_HEADS
    dv = v_ref.shape[-1] // N_RET_HEADS

    @pl.when(t == 0)
    def _():
        s_ref[...] = s0_ref[0]

    cos = cos_ref[...]
    sin = sin_ref[...]
    first_half = (lax.broadcasted_iota(I32, cos.shape, 1) % dk) < (dk // 2)
    width = cos.shape[1]

    def rot(x):
        partner = jnp.where(first_half, pltpu.roll(x, width - dk // 2, 1), pltpu.roll(x, dk // 2, 1))
        return x * cos + partner * sin

    row = lax.broadcasted_iota(I32, (cp, 1), 0)
    q = rot(q_ref[0])
    k = rot(k_ref[0]) * (dk ** -0.5)
    if c_valid < cp:
        k = jnp.where(row < c_valid, k, 0.0)
    v = v_ref[0]
    gb = gb_ref[0]
    ri = lax.broadcasted_iota(I32, (cp, cp), 0)
    ci = lax.broadcasted_iota(I32, (cp, cp), 1)
    diff = (ri - ci).astype(F32)
    rowf = row.astype(F32)
    for h in range(N_RET_HEADS):
        lg = math.log1p(-(2.0 ** (-5.0 - h)))
        dmask = jnp.where(diff >= 0, jnp.exp(jnp.maximum(diff, 0.0) * lg), 0.0)
        cross_dec = jnp.exp((rowf + 1.0) * lg)
        state_dec = jnp.exp((c_valid - 1.0 - rowf) * lg)
        chunk_dec = math.exp(c_valid * lg)
        qh = q[:, h * dk:(h + 1) * dk].astype(mm_dtype)
        kh = k[:, h * dk:(h + 1) * dk]
        vh = v[:, h * dv:(h + 1) * dv].astype(mm_dtype)
        s_old = s_ref[h]
        att = _dot_nt(qh, kh.astype(mm_dtype)) * dmask
        inner = _dot(att.astype(mm_dtype), vh)
        cross = _dot(qh, s_old.astype(mm_dtype)) * cross_dec
        s_ref[h] = s_old * chunk_dec + _dot_tn((kh * state_dec).astype(mm_dtype), vh)
        o = inner + cross
        o = o * lax.rsqrt(jnp.mean(o * o, axis=-1, keepdims=True) + EPS)
        gh = gb[:, h * dv:(h + 1) * dv]
        yb_ref[0, :, h * dv:(h + 1) * dv] = (o * (gh * jax.nn.sigmoid(gh))).astype(yb_ref.dtype)

    @pl.when(t == pl.num_programs(1) - 1)
    def _():
        sl_ref[0] = s_ref[...]


def _retention(ub, cos, sin, s0, *, cp, c_valid, mm_dtype):
    B, T, wtot = ub.shape
    w = wtot // 6
    H, dk, dv = s0.shape[1:]
    return pl.pallas_call(
        functools.partial(_ret_body, cp=cp, c_valid=c_valid, mm_dtype=mm_dtype),
        grid=(B, T // cp),
        in_specs=[
            pl.BlockSpec((1, cp, w), lambda b, t: (b, t, 0)),
            pl.BlockSpec((1, cp, w), lambda b, t: (b, t, 1)),
            pl.BlockSpec((1, cp, 2 * w), lambda b, t: (b, t, 1)),
            pl.BlockSpec((1, cp, 2 * w), lambda b, t: (b, t, 2)),
            pl.BlockSpec((cp, w), lambda b, t: (t, 0)),
            pl.BlockSpec((cp, w), lambda b, t: (t, 0)),
            pl.BlockSpec((1, H, dk, dv), lambda b, t: (b, 0, 0, 0)),
        ],
        out_specs=[
            pl.BlockSpec((1, cp, 2 * w), lambda b, t: (b, t, 0)),
            pl.BlockSpec((1, H, dk, dv), lambda b, t: (b, 0, 0, 0)),
        ],
        out_shape=[
            jax.ShapeDtypeStruct((B, T, 2 * w), BF16),
            jax.ShapeDtypeStruct((B, H, dk, dv), F32),
        ],
        scratch_shapes=[pltpu.VMEM((H, dk, dv), F32)],
        compiler_params=_cparams(("arbitrary", "arbitrary")),
        name="retention",
    )(ub, ub, ub, ub, cos, sin, s0)


def _rope_tables(pos, dk, heads):
    half = dk // 2
    freqs = ROPE_BASE ** (-jnp.arange(half, dtype=F32) / half)
    ang = pos.astype(F32)[:, None] * freqs[None, :]
    cos, sin = jnp.cos(ang), jnp.sin(ang)
    cos_t = jnp.tile(jnp.concatenate([cos, cos], axis=1), (1, heads))
    sin_t = jnp.tile(jnp.concatenate([-sin, sin], axis=1), (1, heads))
    return cos_t, sin_t


def _t5_bucket_np(dist):
    n = np.maximum(dist, 0)
    max_exact = NUM_BUCKETS // 2
    ratio = np.log(np.maximum(n, 1).astype(np.float32) / np.float32(max_exact)) / np.float32(math.log(MAX_DISTANCE / max_exact))
    large = max_exact + (ratio * np.float32(NUM_BUCKETS - max_exact)).astype(np.int32)
    large = np.minimum(large, NUM_BUCKETS - 1)
    return np.where(n < max_exact, n, large).astype(np.int32)


def _bias_from_buckets(bucket, rb_ref, head):
    out = jnp.zeros(bucket.shape, F32)
    for b in range(NUM_BUCKETS):
        out = jnp.where(bucket == b, rb_ref[b, head], out)
    return out


def _dsa_prompt_body(rb_ref, qc_ref, qi_ref, wi_ref, kvb_ref, btab_ref, o_ref,
                     keys_ref, bt_ref, wf_ref, qis_ref, m_ref, l_ref, acc_ref, *, tq, n_keep):
    i = pl.program_id(0)
    G = qc_ref.shape[1] // (N_KV_HEADS * HEAD_DIM_C)
    n_heads = N_KV_HEADS * G
    kcol, vcol, icol = 0, N_KV_HEADS * HEAD_DIM_C, 2 * N_KV_HEADS * HEAD_DIM_C

    @pl.when(i == 0)
    def _():
        for h in range(n_heads):
            for r in range(3):
                bt_ref[h // G, r, h % G] = _bias_from_buckets(btab_ref[r], rb_ref, h)

    for h in range(N_IDX_HEADS):
        qis_ref[h] = qi_ref[:, h * IDX_DIM:(h + 1) * IDX_DIM]
        wf_ref[h] = jnp.broadcast_to(wi_ref[:, IDX_DIM + h:IDX_DIM + h + 1], (tq, tq))

    rowi = lax.broadcasted_iota(I32, (tq, tq), 0)
    coli = lax.broadcasted_iota(I32, (tq, tq), 1)
    n_chunks = i + 1

    def visible(j):
        return (j * tq + coli) <= (i * tq + rowi)

    def score_chunk(j, carry):
        ki = kvb_ref[pl.ds(pl.multiple_of(j * tq, tq), tq), icol:icol + IDX_DIM]
        acc = jnp.zeros((tq, tq), F32)
        for h in range(N_IDX_HEADS):
            acc = acc + jnp.maximum(_dot_nt(qis_ref[h], ki), 0.0) * wf_ref[h]
        score = acc * (N_IDX_HEADS ** -0.5 * IDX_DIM ** -0.5)
        score = jnp.where(visible(j), score, -jnp.inf)
        keys_ref[j] = _sort_key(score)
        return carry

    lax.fori_loop(0, n_chunks, score_chunk, 0)

    def bit_step(b, tau):
        cand = tau + lax.shift_left(jnp.int32(1), 31 - b)

        def count_chunk(j, cnt):
            return cnt + jnp.where(keys_ref[j] >= cand, 1, 0)

        cnt = lax.fori_loop(0, n_chunks, count_chunk, jnp.zeros((tq, tq), I32))
        total = jnp.sum(cnt, axis=1, keepdims=True)
        return jnp.where(total >= n_keep, cand, tau)

    tau = lax.fori_loop(0, 32, bit_step, jnp.full((tq, 1), INT_MIN, I32))

    m_ref[...] = jnp.full(m_ref.shape, NEG_BIG, F32)
    l_ref[...] = jnp.zeros(l_ref.shape, F32)
    acc_ref[...] = jnp.zeros(acc_ref.shape, F32)

    def attend_chunk(j, carry):
        sel = (keys_ref[j] >= tau) & visible(j)
        mb = jnp.where(sel, 0.0, NEG_BIG)
        r = jnp.minimum(i - j, 2)
        base = pl.multiple_of(j * tq, tq)
        for c in range(N_KV_HEADS):
            kc = kvb_ref[pl.ds(base, tq), kcol + c * HEAD_DIM_C:kcol + (c + 1) * HEAD_DIM_C]
            vc = kvb_ref[pl.ds(base, tq), vcol + c * HEAD_DIM_C:vcol + (c + 1) * HEAD_DIM_C]
            for g in range(G):
                h = c * G + g
                q = qc_ref[:, h * HEAD_DIM_C:(h + 1) * HEAD_DIM_C]
                s = _dot_nt(q, kc) + (bt_ref[c, r, g] + mb)
                m_old = m_ref[h]
                m_new = jnp.maximum(m_old, jnp.max(s, axis=1, keepdims=True))
                alpha = jnp.exp(m_old - m_new)
                p = jnp.exp(s - m_new)
                l_ref[h] = alpha * l_ref[h] + jnp.sum(p, axis=1, keepdims=True)
                acc_ref[h] = alpha * acc_ref[h] + _dot(p.astype(BF16), vc)
                m_ref[h] = m_new
        return carry

    lax.fori_loop(0, n_chunks, attend_chunk, 0)
    for h in range(n_heads):
        o_ref[:, h * HEAD_DIM_C:(h + 1) * HEAD_DIM_C] = (acc_ref[h] / l_ref[h]).astype(o_ref.dtype)


def _dsa_prompt(qcqi, kvi_f32, kvi_bf16, rel_bias, *, tq, n_keep):
    S = qcqi.shape[0]
    dc = qcqi.shape[1] - N_IDX_HEADS * IDX_DIM
    n_heads = dc // HEAD_DIM_C
    G = n_heads // N_KV_HEADS
    wkv = kvi_bf16.shape[1]
    d = np.arange(tq)[:, None] - np.arange(tq)[None, :]
    btab = jnp.asarray(np.stack([_t5_bucket_np(d + r * tq) for r in range(3)]))
    slab = (2 * N_KV_HEADS * HEAD_DIM_C) // 128
    grid_spec = pltpu.PrefetchScalarGridSpec(
        num_scalar_prefetch=0,
        grid=(S // tq,),
        in_specs=[
            pl.BlockSpec(memory_space=pltpu.SMEM),
            pl.BlockSpec((tq, dc), lambda i: (i, 0)),
            pl.BlockSpec((tq, N_IDX_HEADS * IDX_DIM), lambda i: (i, dc // (N_IDX_HEADS * IDX_DIM))),
            pl.BlockSpec((tq, 128), lambda i: (i, slab)),
            pl.BlockSpec((S, wkv), lambda i: (0, 0)),
            pl.BlockSpec((3, tq, tq), lambda i: (0, 0, 0)),
        ],
        out_specs=pl.BlockSpec((tq, dc), lambda i: (i, 0)),
        scratch_shapes=[
            pltpu.VMEM((S // tq, tq, tq), I32),
            pltpu.VMEM((N_KV_HEADS, 3, G, tq, tq), F32),
            pltpu.VMEM((N_IDX_HEADS, tq, tq), F32),
            pltpu.VMEM((N_IDX_HEADS, tq, IDX_DIM), BF16),
            pltpu.VMEM((n_heads, tq, 1), F32),
            pltpu.VMEM((n_heads, tq, 1), F32),
            pltpu.VMEM((n_heads, tq, HEAD_DIM_C), F32),
        ],
    )
    return pl.pallas_call(
        functools.partial(_dsa_prompt_body, tq=tq, n_keep=n_keep),
        grid_spec=grid_spec,
        out_shape=jax.ShapeDtypeStruct((S, dc), BF16),
        compiler_params=_cparams(("arbitrary",)),
        name="dsa_prompt",
    )(rel_bias, qcqi, qcqi, kvi_f32, kvi_bf16, btab)


def _dsa_sel_body(pt_ref, qi_ref, w_ref, kin_ref, *rest, t_new, n_keep):
    pages = rest[:PAGES_PER_STEP]
    mb_ref, keys_ref = rest[PAGES_PER_STEP:]
    p = pl.program_id(1)
    n_chunks = keys_ref.shape[0]
    tp = SAMPLE_PAD_T
    q = qi_ref[0]
    w = w_ref[0]

    def chunk_scores(ki):
        s = jnp.maximum(_dot_nt(q, ki), 0.0) * w
        s = s.reshape(tp, N_IDX_HEADS, s.shape[-1]).sum(axis=1)
        return s * (N_IDX_HEADS ** -0.5 * IDX_DIM ** -0.5)

    for r in range(PAGES_PER_STEP):
        keys_ref[p * PAGES_PER_STEP + r] = _sort_key(chunk_scores(pages[r][0].astype(BF16)))

    @pl.when(p == pl.num_programs(1) - 1)
    def _():
        rowi = lax.broadcasted_iota(I32, (tp, 128), 0)
        coli = lax.broadcasted_iota(I32, (tp, 128), 1)
        vis_new = (coli <= rowi) & (coli < t_new)
        s_new = jnp.where(vis_new, chunk_scores(kin_ref[0]), -jnp.inf)
        keys_ref[n_chunks - 1] = _sort_key(s_new)

        def bit_step(b, tau):
            cand = tau + lax.shift_left(jnp.int32(1), 31 - b)
            cnt = jnp.sum(jnp.where(keys_ref[...] >= cand[None], 1, 0), axis=0)
            total = jnp.sum(cnt, axis=1, keepdims=True)
            return jnp.where(total >= n_keep, cand, tau)

        tau = lax.fori_loop(0, 32, bit_step, jnp.full((tp, 1), INT_MIN, I32))
        mb_ref[0] = jnp.where(keys_ref[...] >= tau[None], 0.0, NEG_BIG)
        mb_ref[0, n_chunks - 1] = jnp.where((keys_ref[n_chunks - 1] >= tau) & vis_new, 0.0, NEG_BIG)


def _dsa_sample_select(pt_flat, qi, w, ki_new, pool_ki, *, n_pages, t_new, n_keep):
    B = qi.shape[0]
    P = pool_ki.shape[1]
    steps = n_pages // PAGES_PER_STEP

    def page_spec(r):
        return pl.BlockSpec((1, P, IDX_DIM), lambda b, p, pt: (pt[b * n_pages + p * PAGES_PER_STEP + r], 0, 0))

    grid_spec = pltpu.PrefetchScalarGridSpec(
        num_scalar_prefetch=1,
        grid=(B, steps),
        in_specs=[
            pl.BlockSpec((1,) + qi.shape[1:], lambda b, p, pt: (b, 0, 0)),
            pl.BlockSpec((1,) + w.shape[1:], lambda b, p, pt: (b, 0, 0)),
            pl.BlockSpec((1,) + ki_new.shape[1:], lambda b, p, pt: (b, 0, 0)),
        ] + [page_spec(r) for r in range(PAGES_PER_STEP)],
        out_specs=pl.BlockSpec((1, n_pages + 1, SAMPLE_PAD_T, P), lambda b, p, pt: (b, 0, 0, 0)),
        scratch_shapes=[pltpu.VMEM((n_pages + 1, SAMPLE_PAD_T, P), I32)],
    )
    return pl.pallas_call(
        functools.partial(_dsa_sel_body, t_new=t_new, n_keep=n_keep),
        grid_spec=grid_spec,
        out_shape=jax.ShapeDtypeStruct((B, n_pages + 1, SAMPLE_PAD_T, P), F32),
        compiler_params=_cparams(("arbitrary", "arbitrary")),
        name="dsa_sample_select",
    )(pt_flat, qi, w, ki_new, *([pool_ki] * PAGES_PER_STEP))


def _dsa_att_body(pt_ref, rb_ref, q_ref, mb_ref, kn_ref, vn_ref, btab_ref, *rest):
    kp = rest[:PAGES_PER_STEP]
    vp = rest[PAGES_PER_STEP:2 * PAGES_PER_STEP]
    o_ref, bt_ref, m_ref, l_ref, acc_ref = rest[2 * PAGES_PER_STEP:]
    b = pl.program_id(0)
    p = pl.program_id(1)
    n_chunks = mb_ref.shape[1]
    tp = SAMPLE_PAD_T
    G = q_ref.shape[2] // tp

    @pl.when((b == 0) & (p == 0))
    def _():
        for c in range(N_KV_HEADS):
            for kind in range(3):
                for g in range(G):
                    bt_ref[c, kind, g * tp:(g + 1) * tp, :] = _bias_from_buckets(btab_ref[kind], rb_ref, c * G + g)

    @pl.when(p == 0)
    def _():
        m_ref[...] = jnp.full(m_ref.shape, NEG_BIG, F32)
        l_ref[...] = jnp.zeros(l_ref.shape, F32)
        acc_ref[...] = jnp.zeros(acc_ref.shape, F32)

    def attend(chunk, kind, k_all, v_all):
        mb = jnp.concatenate([mb_ref[0, chunk]] * G, axis=0)
        for c in range(N_KV_HEADS):
            kc = k_all[:, c * HEAD_DIM_C:(c + 1) * HEAD_DIM_C]
            vc = v_all[:, c * HEAD_DIM_C:(c + 1) * HEAD_DIM_C]
            s = _dot_nt(q_ref[0, c], kc) + (bt_ref[c, kind] + mb)
            m_old = m_ref[c]
            m_new = jnp.maximum(m_old, jnp.max(s, axis=1, keepdims=True))
            alpha = jnp.exp(m_old - m_new)
            pr = jnp.exp(s - m_new)
            l_ref[c] = alpha * l_ref[c] + jnp.sum(pr, axis=1, keepdims=True)
            acc_ref[c] = alpha * acc_ref[c] + _dot(pr.astype(BF16), vc)
            m_ref[c] = m_new

    for r in range(PAGES_PER_STEP):
        chunk = p * PAGES_PER_STEP + r
        kind = jnp.where(chunk == n_chunks - 2, 1, 0)
        attend(chunk, kind, kp[r][0].astype(BF16), vp[r][0].astype(BF16))

    @pl.when(p == pl.num_programs(1) - 1)
    def _():
        attend(n_chunks - 1, 2, kn_ref[0], vn_ref[0])
        for c in range(N_KV_HEADS):
            o_ref[0, c] = acc_ref[c] / l_ref[c]


def _dsa_sample_attend(pt_flat, rel_bias, q, mb, k_new, v_new, pool_k, pool_v, *, n_pages, past):
    B, kvh, rows, dh = q.shape
    P = pool_k.shape[1]
    steps = n_pages // PAGES_PER_STEP
    tp = SAMPLE_PAD_T
    t = np.arange(tp)[:, None]
    col = np.arange(P)[None, :]
    far = np.full((tp, P), NUM_BUCKETS - 1, np.int32)
    assert past - (n_pages - 1) * P + 0 - (P - 1) >= 1 and past - (n_pages - 2) * P - (P - 1) >= MAX_DISTANCE
    last_page = _t5_bucket_np(past + t - ((n_pages - 1) * P + col))
    new = _t5_bucket_np(t - col)
    btab = jnp.asarray(np.stack([far, last_page, new]))

    def page_spec(r):
        return pl.BlockSpec((1, P, kvh * dh), lambda b, p, pt: (pt[b * n_pages + p * PAGES_PER_STEP + r], 0, 0))

    grid_spec = pltpu.PrefetchScalarGridSpec(
        num_scalar_prefetch=1,
        grid=(B, steps),
        in_specs=[
            pl.BlockSpec(memory_space=pltpu.SMEM),
            pl.BlockSpec((1, kvh, rows, dh), lambda b, p, pt: (b, 0, 0, 0)),
            pl.BlockSpec((1,) + mb.shape[1:], lambda b, p, pt: (b, 0, 0, 0)),
            pl.BlockSpec((1,) + k_new.shape[1:], lambda b, p, pt: (b, 0, 0)),
            pl.BlockSpec((1,) + v_new.shape[1:], lambda b, p, pt: (b, 0, 0)),
            pl.BlockSpec((3, tp, P), lambda b, p, pt: (0, 0, 0)),
        ] + [page_spec(r) for r in range(PAGES_PER_STEP)] * 2,
        out_specs=pl.BlockSpec((1, kvh, rows, dh), lambda b, p, pt: (b, 0, 0, 0)),
        scratch_shapes=[
            pltpu.VMEM((kvh, 3, rows, P), F32),
            pltpu.VMEM((kvh, rows, 1), F32),
            pltpu.VMEM((kvh, rows, 1), F32),
            pltpu.VMEM((kvh, rows, dh), F32),
        ],
    )
    return pl.pallas_call(
        _dsa_att_body,
        grid_spec=grid_spec,
        out_shape=jax.ShapeDtypeStruct((B, kvh, rows, dh), F32),
        compiler_params=_cparams(("arbitrary", "arbitrary")),
        name="dsa_sample_attend",
    )(pt_flat, rel_bias, q, mb, k_new, v_new, btab, *([pool_k] * PAGES_PER_STEP), *([pool_v] * PAGES_PER_STEP))


def _split_w_in(w, d_a, d_b, d_c, rk, kv_w, qi_w):
    o = np.cumsum([0, d_a, d_a, rk, rk, d_b, d_b, d_c, kv_w, kv_w, qi_w, IDX_DIM, N_IDX_HEADS])
    wb = w.astype(BF16)
    grp_a = wb[:, o[0]:o[2]]
    grp_b = wb[:, o[2]:o[6]]
    grp_c1 = jnp.concatenate([wb[:, o[6]:o[7]], wb[:, o[9]:o[10]]], axis=1)
    tail = wb[:, o[10]:o[12]]
    pad = 128 - tail.shape[1]
    grp_c2 = jnp.concatenate([wb[:, o[7]:o[9]], tail, jnp.zeros((w.shape[0], pad), BF16)], axis=1)
    return grp_a, grp_b, grp_c1, grp_c2


def kernel(x_prompt, x_sample, cache_k, cache_v, cache_kidx, page_table, state_lru_h, state_conv, state_ret,
           norm_ffn1, ffn1_gate, ffn1_up, ffn1_down, norm_mix, w_in, w_out, conv_w, conv_b,
           lru_wa, lru_ba, lru_wx, lru_bx, lru_lambda, rel_bias,
           norm_ffn2, ffn2_gate, ffn2_up, ffn2_down, norm_final):
    depth = norm_ffn1.shape[0]
    _, S, D = x_prompt.shape
    Bs, Ts, _ = x_sample.shape
    n_pool, P = cache_k.shape[1], cache_k.shape[2]
    n_pages = page_table.shape[1]
    past = n_pages * P
    d_a = state_lru_h.shape[-1]
    H_r, rdk, rdv = state_ret.shape[2:]
    d_b = H_r * rdv
    d_c = D - d_a - d_b
    kv_w = N_KV_HEADS * HEAD_DIM_C
    qi_w = N_IDX_HEADS * IDX_DIM
    G = d_c // HEAD_DIM_C // N_KV_HEADS
    tp = SAMPLE_PAD_T
    Rs = Bs * tp

    xp = x_prompt.reshape(S, D)
    xs = jnp.pad(x_sample, ((0, 0), (0, tp - Ts), (0, 0))).reshape(Rs, D)

    cos_p, sin_p = _rope_tables(jnp.arange(S, dtype=I32), rdk, H_r)
    cos_s, sin_s = _rope_tables(past + jnp.arange(tp, dtype=I32), rdk, H_r)
    colscale = jnp.concatenate([jnp.full((d_c,), HEAD_DIM_C ** -0.5, F32), jnp.ones((qi_w,), F32)])

    pool_k = cache_k.reshape(depth * n_pool, P, kv_w)
    pool_v = cache_v.reshape(depth * n_pool, P, kv_w)
    pool_ki = cache_kidx.reshape(depth * n_pool, P, IDX_DIM)

    zeros_conv = jnp.zeros((1, 8, d_a), F32)
    zeros_h = jnp.zeros((1, 1, d_a), F32)
    zeros_s = jnp.zeros((1, H_r, rdk, rdv), F32)

    outs_p, outs_s = [], []
    y_prompt = y_sample = None
    for l in range(depth):
        bf = lambda a: a[l].astype(BF16)
        f1 = (bf(ffn1_gate), bf(ffn1_up), bf(ffn1_down))
        f2 = (bf(ffn2_gate), bf(ffn2_up), bf(ffn2_down))
        w_a, w_b, w_c1, w_c2 = _split_w_in(w_in[l], d_a, d_b, d_c, H_r * rdk, kv_w, qi_w)
        wo = w_out[l].astype(BF16)
        wo_a, wo_b, wo_c = wo[:d_a], wo[d_a:d_a + d_b], wo[d_a + d_b:]
        lwa, lwx = lru_wa[l].astype(BF16), lru_wx[l].astype(BF16)
        last = l == depth - 1
        pt_flat = (page_table + l * n_pool).reshape(-1).astype(I32)

        xp = _ffn(xp, norm_ffn1[l], *f1, tm=512, tf=512)
        proj = functools.partial(_norm_matmul, xp, norm_mix[l])
        (ua,) = proj(w_a, tm=512, tn=1024, out_f32=True, out_bf16=False)
        (ub,) = proj(w_b, tm=512, tn=768, out_f32=True, out_bf16=False)
        (c1,) = proj(w_c1, colscale, tm=512, tn=1024, out_f32=False, out_bf16=True)
        c2f, c2b = proj(w_c2, tm=512, tn=w_c2.shape[1], out_f32=True, out_bf16=True)
        ya, h_last, conv_new = _lru(ua.reshape(1, S, 2 * d_a), zeros_conv, zeros_h, conv_w[l], conv_b[l],
                                    lwa, lru_ba[l], lwx, lru_bx[l], lru_lambda[l], tc=1024, t_valid_last=1024)
        yb, s_last = _retention(ub.reshape(1, S, -1), cos_p, sin_p, zeros_s, cp=128, c_valid=128, mm_dtype=BF16)
        yc = _dsa_prompt(c1, c2f, c2b, rel_bias, tq=128, n_keep=min(TOPK_MAX, S // 4))
        xp = _out_proj(xp, ya.reshape(S, d_a), yb.reshape(S, d_b), yc, wo_a, wo_b, wo_c, tm=512, tn=1024)
        if last:
            xp, y_prompt = _ffn(xp, norm_ffn2[l], *f2, norm_final, tm=512, tf=512)
        else:
            xp = _ffn(xp, norm_ffn2[l], *f2, tm=512, tf=512)
        outs_p.append((c2f[:, :kv_w].reshape(1, S, N_KV_HEADS, HEAD_DIM_C),
                       c2f[:, kv_w:2 * kv_w].reshape(1, S, N_KV_HEADS, HEAD_DIM_C),
                       c2f[:, 2 * kv_w:2 * kv_w + IDX_DIM].reshape(1, S, IDX_DIM),
                       h_last.reshape(1, d_a), conv_new, s_last))

        xs = _ffn(xs, norm_ffn1[l], *f1, tm=Rs, tf=512)
        proj = functools.partial(_norm_matmul, xs, norm_mix[l])
        (ua,) = proj(w_a, tm=Rs, tn=1024, out_f32=True, out_bf16=False)
        (ub,) = proj(w_b, tm=Rs, tn=768, out_f32=True, out_bf16=False)
        (c1,) = proj(w_c1, colscale, tm=Rs, tn=1024, out_f32=False, out_bf16=True)
        c2f, c2b = proj(w_c2, tm=Rs, tn=w_c2.shape[1], out_f32=True, out_bf16=True)
        conv0 = jnp.pad(state_conv[l], ((0, 0), (8 - (CONV_WIDTH - 1), 0), (0, 0)))
        ya, h_last, conv_new = _lru(ua.reshape(Bs, tp, 2 * d_a), conv0, state_lru_h[l].reshape(Bs, 1, d_a),
                                    conv_w[l], conv_b[l], lwa, lru_ba[l], lwx, lru_bx[l], lru_lambda[l],
                                    tc=tp, t_valid_last=Ts)
        yb, s_last = _retention(ub.reshape(Bs, tp, -1), cos_s, sin_s, state_ret[l], cp=tp, c_valid=Ts, mm_dtype=F32)
        qi_s = c1[:, d_c:].reshape(Bs, tp * N_IDX_HEADS, IDX_DIM)
        w_s = c2f[:, 2 * kv_w + IDX_DIM:2 * kv_w + IDX_DIM + N_IDX_HEADS].reshape(Bs, tp * N_IDX_HEADS, 1)
        new_rows = jnp.pad(c2b.reshape(Bs, tp, -1), ((0, 0), (0, P - tp), (0, 0)))
        mb = _dsa_sample_select(pt_flat, qi_s, w_s, new_rows[:, :, 2 * kv_w:2 * kv_w + IDX_DIM], pool_ki,
                                n_pages=n_pages, t_new=Ts, n_keep=min(TOPK_MAX, (past + Ts) // 4))
        q_s = c1[:, :d_c].reshape(Bs, tp, N_KV_HEADS, G, HEAD_DIM_C).transpose(0, 2, 3, 1, 4)
        q_s = q_s.reshape(Bs, N_KV_HEADS, G * tp, HEAD_DIM_C)
        o_s = _dsa_sample_attend(pt_flat, rel_bias, q_s, mb, new_rows[:, :, :kv_w], new_rows[:, :, kv_w:2 * kv_w],
                                 pool_k, pool_v, n_pages=n_pages, past=past)
        yc = o_s.reshape(Bs, N_KV_HEADS, G, tp, HEAD_DIM_C).transpose(0, 3, 1, 2, 4).reshape(Rs, d_c).astype(BF16)
        xs = _out_proj(xs, ya.reshape(Rs, d_a), yb.reshape(Rs, d_b), yc, wo_a, wo_b, wo_c, tm=Rs, tn=1024)
        if last:
            xs, y_sample = _ffn(xs, norm_ffn2[l], *f2, norm_final, tm=Rs, tf=512)
        else:
            xs = _ffn(xs, norm_ffn2[l], *f2, tm=Rs, tf=512)
        c2s = c2f.reshape(Bs, tp, -1)[:, :Ts]
        outs_s.append((c2s[:, :, :kv_w].reshape(Bs, Ts, N_KV_HEADS, HEAD_DIM_C),
                       c2s[:, :, kv_w:2 * kv_w].reshape(Bs, Ts, N_KV_HEADS, HEAD_DIM_C),
                       c2s[:, :, 2 * kv_w:2 * kv_w + IDX_DIM],
                       h_last.reshape(Bs, d_a), conv_new, s_last))

    stack = lambda outs, k: jnp.stack([o[k] for o in outs])
    return ((y_prompt.reshape(1, S, D), y_sample.reshape(Bs, tp, D)[:, :Ts])
            + tuple(stack(outs_p, k) for k in range(6))
            + tuple(stack(outs_s, k) for k in range(6)))
```

```python
import functools
import math

import numpy as np
import jax
import jax.numpy as jnp
from jax import lax
from jax.experimental import pallas as pl
from jax.experimental.pallas import tpu as pltpu

F32 = jnp.float32
BF16 = jnp.bfloat16
I32 = jnp.int32

EPS = 1e-6
N_A_BLOCKS = 4
CONV_WIDTH = 4
LRU_C = 8.0
N_RET_HEADS = 4
ROPE_BASE = 10000.0
HEAD_DIM_C = 128
N_KV_HEADS = 2
N_IDX_HEADS = 16
IDX_DIM = 64
TOPK_MAX = 256
NUM_BUCKETS = 32
MAX_DISTANCE = 128
SAMPLE_PAD_T = 8
PAGES_PER_STEP = 8
NEG_BIG = -1e30
INT_MIN = -2 ** 31

VMEM_LIMIT = 56 * 1024 * 1024


def _cparams(sem):
    return pltpu.CompilerParams(dimension_semantics=sem, vmem_limit_bytes=VMEM_LIMIT)


def _rms(x, g):
    return x * lax.rsqrt(jnp.mean(x * x, axis=-1, keepdims=True) + EPS) * g


def _dot(a, b):
    return jnp.dot(a, b, preferred_element_type=F32)


def _dot_nt(a, b):
    return lax.dot_general(a, b, (((1,), (1,)), ((), ())), preferred_element_type=F32)


def _dot_tn(a, b):
    return lax.dot_general(a, b, (((0,), (0,)), ((), ())), preferred_element_type=F32)


def _sort_key(score):
    bits = lax.bitcast_convert_type(score, I32)
    return jnp.where(bits < 0, bits ^ jnp.int32(0x7FFFFFFF), bits)


def _ffn_body(x_ref, g_ref, wg_ref, wu_ref, wd_ref, *rest, final_norm):
    if final_norm:
        gf_ref, o_ref, on_ref, h_ref = rest
    else:
        o_ref, h_ref = rest
    j = pl.program_id(1)

    @pl.when(j == 0)
    def _():
        h_ref[...] = _rms(x_ref[...], g_ref[...]).astype(BF16)
        o_ref[...] = jnp.zeros_like(o_ref)

    h = h_ref[...]
    g = _dot(h, wg_ref[...])
    u = _dot(h, wu_ref[...])
    a = (g * jax.nn.sigmoid(g) * u).astype(BF16)
    o_ref[...] += _dot(a, wd_ref[...])

    @pl.when(j == pl.num_programs(1) - 1)
    def _():
        y = x_ref[...] + 0.5 * o_ref[...]
        o_ref[...] = y
        if final_norm:
            on_ref[...] = _rms(y, gf_ref[...])


def _ffn(x, g, wg, wu, wd, gf=None, *, tm, tf):
    R, D = x.shape
    FF = wg.shape[1]
    final_norm = gf is not None
    in_specs = [
        pl.BlockSpec((tm, D), lambda i, j: (i, 0)),
        pl.BlockSpec((1, D), lambda i, j: (0, 0)),
        pl.BlockSpec((D, tf), lambda i, j: (0, j)),
        pl.BlockSpec((D, tf), lambda i, j: (0, j)),
        pl.BlockSpec((tf, D), lambda i, j: (j, 0)),
    ]
    args = [x, g.reshape(1, D), wg, wu, wd]
    out_shape = [jax.ShapeDtypeStruct((R, D), F32)]
    out_specs = [pl.BlockSpec((tm, D), lambda i, j: (i, 0))]
    if final_norm:
        in_specs.append(pl.BlockSpec((1, D), lambda i, j: (0, 0)))
        args.append(gf.reshape(1, D))
        out_shape.append(jax.ShapeDtypeStruct((R, D), F32))
        out_specs.append(pl.BlockSpec((tm, D), lambda i, j: (i, 0)))
    res = pl.pallas_call(
        functools.partial(_ffn_body, final_norm=final_norm),
        grid=(R // tm, FF // tf),
        in_specs=in_specs,
        out_specs=out_specs,
        out_shape=out_shape,
        scratch_shapes=[pltpu.VMEM((tm, D), BF16)],
        compiler_params=_cparams(("parallel", "arbitrary")),
        name="ffn",
    )(*args)
    return res if final_norm else res[0]


def _nmm_body(x_ref, g_ref, w_ref, *rest, out_f32, out_bf16, colscale):
    rest = list(rest)
    cs_ref = rest.pop(0) if colscale else None
    of_ref = rest.pop(0) if out_f32 else None
    ob_ref = rest.pop(0) if out_bf16 else None
    h_ref = rest.pop(0)

    @pl.when(pl.program_id(1) == 0)
    def _():
        h_ref[...] = _rms(x_ref[...], g_ref[...]).astype(BF16)

    r = _dot(h_ref[...], w_ref[...])
    if colscale:
        r = r * cs_ref[...]
    if out_f32:
        of_ref[...] = r
    if out_bf16:
        ob_ref[...] = r.astype(BF16)


def _norm_matmul(x, g, w, colscale=None, *, tm, tn, out_f32, out_bf16):
    R, D = x.shape
    N = w.shape[1]
    in_specs = [
        pl.BlockSpec((tm, D), lambda i, j: (i, 0)),
        pl.BlockSpec((1, D), lambda i, j: (0, 0)),
        pl.BlockSpec((D, tn), lambda i, j: (0, j)),
    ]
    args = [x, g.reshape(1, D), w]
    if colscale is not None:
        in_specs.append(pl.BlockSpec((1, tn), lambda i, j: (0, j)))
        args.append(colscale.reshape(1, N))
    out_shape, out_specs = [], []
    for flag, dt in ((out_f32, F32), (out_bf16, BF16)):
        if flag:
            out_shape.append(jax.ShapeDtypeStruct((R, N), dt))
            out_specs.append(pl.BlockSpec((tm, tn), lambda i, j: (i, j)))
    res = pl.pallas_call(
        functools.partial(_nmm_body, out_f32=out_f32, out_bf16=out_bf16, colscale=colscale is not None),
        grid=(R // tm, N // tn),
        in_specs=in_specs,
        out_specs=out_specs,
        out_shape=out_shape,
        scratch_shapes=[pltpu.VMEM((tm, D), BF16)],
        compiler_params=_cparams(("parallel", "arbitrary")),
        name="norm_matmul",
    )(*args)
    return res


def _oproj_body(x_ref, ya_ref, yb_ref, yc_ref, wa_ref, wb_ref, wc_ref, o_ref):
    o_ref[...] = (x_ref[...] + _dot(ya_ref[...], wa_ref[...]) + _dot(yb_ref[...], wb_ref[...])
                  + _dot(yc_ref[...], wc_ref[...]))


def _out_proj(x, ya, yb, yc, wa, wb, wc, *, tm, tn):
    R, D = x.shape
    row = lambda w: pl.BlockSpec((tm, w), lambda i, j: (i, 0))
    col = lambda k: pl.BlockSpec((k, tn), lambda i, j: (0, j))
    return pl.pallas_call(
        _oproj_body,
        grid=(R // tm, D // tn),
        in_specs=[pl.BlockSpec((tm, tn), lambda i, j: (i, j)), row(ya.shape[1]), row(yb.shape[1]), row(yc.shape[1]),
                  col(wa.shape[0]), col(wb.shape[0]), col(wc.shape[0])],
        out_specs=pl.BlockSpec((tm, tn), lambda i, j: (i, j)),
        out_shape=jax.ShapeDtypeStruct((R, D), F32),
        compiler_params=_cparams(("parallel", "arbitrary")),
        name="out_proj",
    )(x, ya, yb, yc, wa, wb, wc)


def _lru_body(xa_ref, ga_ref, conv0_ref, h0_ref, cw_ref, cb_ref, wa_ref, ba_ref, wx_ref, bx_ref, lam_ref,
              ya_ref, hl_ref, cn_ref, xs_ref, a_ref, u_ref, hs_ref, hc_ref, *, tc, t_valid_last):
    t = pl.program_id(1)
    da = xa_ref.shape[-1]
    blk = da // N_A_BLOCKS
    tail = 8

    @pl.when(t == 0)
    def _():
        xs_ref[0:tail, :] = conv0_ref[0]
        hc_ref[0:1, :] = h0_ref[0]

    xs_ref[tail:tail + tc, :] = xa_ref[0]
    xc = cb_ref[...]
    for j in range(CONV_WIDTH):
        off = tail - (CONV_WIDTH - 1) + j
        xc = xc + xs_ref[off:off + tc, :] * cw_ref[j:j + 1, :]
    xcb = xc.astype(BF16)
    pre_r = jnp.concatenate([_dot(xcb[:, n * blk:(n + 1) * blk], wa_ref[n]) for n in range(N_A_BLOCKS)], axis=1)
    pre_i = jnp.concatenate([_dot(xcb[:, n * blk:(n + 1) * blk], wx_ref[n]) for n in range(N_A_BLOCKS)], axis=1)
    r = jax.nn.sigmoid(pre_r + ba_ref[...])
    gi = jax.nn.sigmoid(pre_i + bx_ref[...])
    z = -lam_ref[...]
    softplus = jnp.maximum(z, 0.0) + jnp.log1p(jnp.exp(-jnp.abs(z)))
    log_a = (-LRU_C) * r * softplus
    a = jnp.exp(log_a)
    a_ref[...] = a
    u_ref[...] = jnp.sqrt(-jnp.tanh(log_a) * (a * a + 1.0)) * gi * xc

    def step(k, h):
        base = pl.multiple_of(k * 8, 8)
        for rr in range(8):
            h = a_ref[pl.ds(base + rr, 1), :] * h + u_ref[pl.ds(base + rr, 1), :]
            hs_ref[pl.ds(base + rr, 1), :] = h
        return h

    h = lax.fori_loop(0, tc // 8, step, hc_ref[0:1, :])
    hc_ref[0:1, :] = h
    ya_ref[0] = (hs_ref[...] * jax.nn.gelu(ga_ref[0])).astype(ya_ref.dtype)

    @pl.when(t == pl.num_programs(1) - 1)
    def _():
        hl_ref[0] = hs_ref[t_valid_last - 1:t_valid_last, :]
        lo = tail - (CONV_WIDTH - 1) + t_valid_last
        cn_ref[0] = xs_ref[lo:lo + CONV_WIDTH - 1, :]

    xs_ref[0:tail, :] = xs_ref[tc:tc + tail, :]


def _lru(ua, conv0, h0, cw, cb, wa, ba, wx, bx, lam, *, tc, t_valid_last):
    B, T, da2 = ua.shape
    da = da2 // 2
    vec = lambda: pl.BlockSpec((1, da), lambda b, t: (0, 0))
    blk = da // N_A_BLOCKS
    return pl.pallas_call(
        functools.partial(_lru_body, tc=tc, t_valid_last=t_valid_last),
        grid=(B, T // tc),
        in_specs=[
            pl.BlockSpec((1, tc, da), lambda b, t: (b, t, 0)),
            pl.BlockSpec((1, tc, da), lambda b, t: (b, t, 1)),
            pl.BlockSpec((1, 8, da), lambda b, t: (b, 0, 0)),
            pl.BlockSpec((1, 1, da), lambda b, t: (b, 0, 0)),
            pl.BlockSpec((CONV_WIDTH, da), lambda b, t: (0, 0)),
            vec(),
            pl.BlockSpec((N_A_BLOCKS, blk, blk), lambda b, t: (0, 0, 0)),
            vec(),
            pl.BlockSpec((N_A_BLOCKS, blk, blk), lambda b, t: (0, 0, 0)),
            vec(),
            vec(),
        ],
        out_specs=[
            pl.BlockSpec((1, tc, da), lambda b, t: (b, t, 0)),
            pl.BlockSpec((1, 1, da), lambda b, t: (b, 0, 0)),
            pl.BlockSpec((1, CONV_WIDTH - 1, da), lambda b, t: (b, 0, 0)),
        ],
        out_shape=[
            jax.ShapeDtypeStruct((B, T, da), BF16),
            jax.ShapeDtypeStruct((B, 1, da), F32),
            jax.ShapeDtypeStruct((B, CONV_WIDTH - 1, da), F32),
        ],
        scratch_shapes=[
            pltpu.VMEM((tc + 8, da), F32),
            pltpu.VMEM((tc, da), F32),
            pltpu.VMEM((tc, da), F32),
            pltpu.VMEM((tc, da), F32),
            pltpu.VMEM((8, da), F32),
        ],
        compiler_params=_cparams(("arbitrary", "arbitrary")),
        name="rglru",
    )(ua, ua, conv0, h0, cw, cb.reshape(1, da), wa, ba.reshape(1, da), wx, bx.reshape(1, da), lam.reshape(1, da))


def _ret_body(q_ref, k_ref, v_ref, gb_ref, cos_ref, sin_ref, s0_ref, yb_ref, sl_ref, s_ref, *, cp, c_valid, mm_dtype):
    t = pl.program_id(1)
    dk = q_ref.shape[-1] // N_RET_HEADS
    dv = v_ref.shape[-1] // N_RET_HEADS

    @pl.when(t == 0)
    def _():
        s_ref[...] = s0_ref[0]

    cos = cos_ref[...]
    sin = sin_ref[...]
    first_half = (lax.broadcasted_iota(I32, cos.shape, 1) % dk) < (dk // 2)
    width = cos.shape[1]

    def rot(x):
        partner = jnp.where(first_half, pltpu.roll(x, width - dk // 2, 1), pltpu.roll(x, dk // 2, 1))
        return x * cos + partner * sin

    row = lax.broadcasted_iota(I32, (cp, 1), 0)
    q = rot(q_ref[0])
    k = rot(k_ref[0]) * (dk ** -0.5)
    if c_valid < cp:
        k = jnp.where(row < c_valid, k, 0.0)
    v = v_ref[0]
    gb = gb_ref[0]
    ri = lax.broadcasted_iota(I32, (cp, cp), 0)
    ci = lax.broadcasted_iota(I32, (cp, cp), 1)
    diff = (ri - ci).astype(F32)
    rowf = row.astype(F32)
    for h in range(N_RET_HEADS):
        lg = math.log1p(-(2.0 ** (-5.0 - h)))
        dmask = jnp.where(diff >= 0, jnp.exp(jnp.maximum(diff, 0.0) * lg), 0.0)
        cross_dec = jnp.exp((rowf + 1.0) * lg)
        state_dec = jnp.exp((c_valid - 1.0 - rowf) * lg)
        chunk_dec = math.exp(c_valid * lg)
        qh = q[:, h * dk:(h + 1) * dk].astype(mm_dtype)
        kh = k[:, h * dk:(h + 1) * dk]
        vh = v[:, h * dv:(h + 1) * dv].astype(mm_dtype)
        s_old = s_ref[h]
        att = _dot_nt(qh, kh.astype(mm_dtype)) * dmask
        inner = _dot(att.astype(mm_dtype), vh)
        cross = _dot(qh, s_old.astype(mm_dtype)) * cross_dec
        s_ref[h] = s_old * chunk_dec + _dot_tn((kh * state_dec).astype(mm_dtype), vh)
        o = inner + cross
        o = o * lax.rsqrt(jnp.mean(o * o, axis=-1, keepdims=True) + EPS)
        gh = gb[:, h * dv:(h + 1) * dv]
        yb_ref[0, :, h * dv:(h + 1) * dv] = (o * (gh * jax.nn.sigmoid(gh))).astype(yb_ref.dtype)

    @pl.when(t == pl.num_programs(1) - 1)
    def _():
        sl_ref[0] = s_ref[...]


def _retention(ub, cos, sin, s0, *, cp, c_valid, mm_dtype):
    B, T, wtot = ub.shape
    w = wtot // 6
    H, dk, dv = s0.shape[1:]
    return pl.pallas_call(
        functools.partial(_ret_body, cp=cp, c_valid=c_valid, mm_dtype=mm_dtype),
        grid=(B, T // cp),
        in_specs=[
            pl.BlockSpec((1, cp, w), lambda b, t: (b, t, 0)),
            pl.BlockSpec((1, cp, w), lambda b, t: (b, t, 1)),
            pl.BlockSpec((1, cp, 2 * w), lambda b, t: (b, t, 1)),
            pl.BlockSpec((1, cp, 2 * w), lambda b, t: (b, t, 2)),
            pl.BlockSpec((cp, w), lambda b, t: (t, 0)),
            pl.BlockSpec((cp, w), lambda b, t: (t, 0)),
            pl.BlockSpec((1, H, dk, dv), lambda b, t: (b, 0, 0, 0)),
        ],
        out_specs=[
            pl.BlockSpec((1, cp, 2 * w), lambda b, t: (b, t, 0)),
            pl.BlockSpec((1, H, dk, dv), lambda b, t: (b, 0, 0, 0)),
        ],
        out_shape=[
            jax.ShapeDtypeStruct((B, T, 2 * w), BF16),
            jax.ShapeDtypeStruct((B, H, dk, dv), F32),
        ],
        scratch_shapes=[pltpu.VMEM((H, dk, dv), F32)],
        compiler_params=_cparams(("arbitrary", "arbitrary")),
        name="retention",
    )(ub, ub, ub, ub, cos, sin, s0)


def _rope_tables(pos, dk, heads):
    half = dk // 2
    freqs = ROPE_BASE ** (-jnp.arange(half, dtype=F32) / half)
    ang = pos.astype(F32)[:, None] * freqs[None, :]
    cos, sin = jnp.cos(ang), jnp.sin(ang)
    cos_t = jnp.tile(jnp.concatenate([cos, cos], axis=1), (1, heads))
    sin_t = jnp.tile(jnp.concatenate([-sin, sin], axis=1), (1, heads))
    return cos_t, sin_t


def _t5_bucket_np(dist):
    n = np.maximum(dist, 0)
    max_exact = NUM_BUCKETS // 2
    ratio = np.log(np.maximum(n, 1).astype(np.float32) / np.float32(max_exact)) / np.float32(math.log(MAX_DISTANCE / max_exact))
    large = max_exact + (ratio * np.float32(NUM_BUCKETS - max_exact)).astype(np.int32)
    large = np.minimum(large, NUM_BUCKETS - 1)
    return np.where(n < max_exact, n, large).astype(np.int32)


def _bias_from_buckets(bucket, rb_ref, head):
    def step(b, out):
        return jnp.where(bucket == b, rb_ref[b, head], out)

    return lax.fori_loop(0, NUM_BUCKETS, step, jnp.zeros(bucket.shape, F32))


def _dsa_prompt_body(rb_ref, qc_ref, qi_ref, wi_ref, kvb_ref, btab_ref, o_ref,
                     keys_ref, bt_ref, wf_ref, qis_ref, qs_ref, mx_ref, mrep_ref, acc_ref, cut_ref,
                     *, tq, n_keep, idx_bits):
    i = pl.program_id(0)
    G = qc_ref.shape[1] // (N_KV_HEADS * HEAD_DIM_C)
    n_heads = N_KV_HEADS * G
    kcol, vcol, icol = 0, N_KV_HEADS * HEAD_DIM_C, 2 * N_KV_HEADS * HEAD_DIM_C
    lanes = HEAD_DIM_C
    halves = tq // lanes
    n_tiles = i + 1

    @pl.when(i == 0)
    def _():
        for h in range(n_heads):
            far = rb_ref[NUM_BUCKETS - 1, h]
            for r in range(2):
                bt_ref[h // G, r, h % G] = _bias_from_buckets(btab_ref[r], rb_ref, h) - far

    for h in range(N_IDX_HEADS):
        qis_ref[h] = qi_ref[:, h * IDX_DIM:(h + 1) * IDX_DIM]
        wf_ref[h] = jnp.broadcast_to(wi_ref[:, IDX_DIM + h:IDX_DIM + h + 1], (tq, lanes))
    for c in range(N_KV_HEADS):
        for g in range(G):
            h = c * G + g
            qs_ref[c, g * tq:(g + 1) * tq, :] = qc_ref[:, h * HEAD_DIM_C:(h + 1) * HEAD_DIM_C]

    rowi = lax.broadcasted_iota(I32, (tq, tq), 0)
    coli = lax.broadcasted_iota(I32, (tq, tq), 1)

    def key_index(j):
        return j * tq + coli

    def visible(j):
        return key_index(j) <= (i * tq + rowi)

    def score_tile(j, carry):
        ki = kvb_ref[pl.ds(pl.multiple_of(j * tq, tq), tq), icol:icol + IDX_DIM]
        for rh in range(halves):
            rows = slice(rh * lanes, (rh + 1) * lanes)
            acc = jnp.zeros((lanes, tq), F32)
            for h in range(N_IDX_HEADS):
                w = wf_ref[h, rows, :]
                acc = acc + jnp.maximum(_dot_nt(qis_ref[h, rows, :], ki), 0.0) * jnp.concatenate([w] * halves, axis=1)
            score = acc * (N_IDX_HEADS ** -0.5 * IDX_DIM ** -0.5)
            score = jnp.where(visible(j)[rows], score, -jnp.inf)
            keys_ref[j, rows, :] = _sort_key(score)
        return carry

    lax.fori_loop(0, n_tiles, score_tile, 0)

    ones_sq = jnp.ones((lanes, lanes), BF16)

    def count(pred, *row_args):
        totals = []
        for rh in range(halves):
            rows = slice(rh * lanes, (rh + 1) * lanes)
            args = [a[rows] for a in row_args]

            def body(j, cnt):
                for a in range(halves):
                    cols = slice(a * lanes, (a + 1) * lanes)
                    idx = j * tq + a * lanes + lax.broadcasted_iota(I32, (lanes, lanes), 1)
                    cnt = cnt + jnp.where(pred(keys_ref[j, rows, cols], idx, *args), 1, 0)
                return cnt

            totals.append(lax.fori_loop(0, n_tiles, body, jnp.zeros((lanes, lanes), I32)))
        return _dot(jnp.concatenate(totals, axis=0).astype(F32).astype(BF16), ones_sq).astype(I32)

    def bit_step(b, tau):
        cand = tau + lax.shift_left(jnp.int32(1), 31 - b)
        return jnp.where(count(lambda k, idx, c: k >= c, cand) >= n_keep, cand, tau)

    tau = lax.fori_loop(0, 32, bit_step, jnp.full((tq, lanes), INT_MIN, I32))

    cut_ref[...] = jnp.full((tq, lanes), 2 ** 31 - 1, I32)
    n_ge = count(lambda k, idx, t: k >= t, tau)

    @pl.when(jnp.max(n_ge) > n_keep)
    def _():
        need = n_keep - count(lambda k, idx, t: k > t, tau)

        def idx_step(b, cut):
            cand = cut + lax.shift_left(jnp.int32(1), idx_bits - 1 - b)
            below = count(lambda k, idx, t, c: (k == t) & (idx < c), tau, cand)
            return jnp.where(below < need, cand, cut)

        cut_ref[...] = lax.fori_loop(0, idx_bits, idx_step, jnp.zeros((tq, lanes), I32))

    tau_t = jnp.concatenate([tau] * halves, axis=1)
    cut_t = jnp.concatenate([cut_ref[...]] * halves, axis=1)

    def mask_tile(j, carry):
        k = keys_ref[j]
        sel = ((k > tau_t) | ((k == tau_t) & (key_index(j) <= cut_t))) & visible(j)
        keys_ref[j] = lax.bitcast_convert_type(jnp.where(sel, 0.0, NEG_BIG), I32)
        return carry

    lax.fori_loop(0, n_tiles, mask_tile, 0)

    def logits(j, c, r):
        base = pl.multiple_of(j * tq, tq)
        kc = kvb_ref[pl.ds(base, tq), kcol + c * HEAD_DIM_C:kcol + (c + 1) * HEAD_DIM_C]
        s = _dot_nt(qs_ref[c], kc).reshape(G, tq, tq) + lax.bitcast_convert_type(keys_ref[j], F32)[None]
        if r is not None:
            s = s + bt_ref[c, r]
        return s

    def max_tile(j, r):
        for c in range(N_KV_HEADS):
            s = logits(j, c, r)
            m = s[..., 0:lanes]
            for a in range(1, halves):
                m = jnp.maximum(m, s[..., a * lanes:(a + 1) * lanes])
            mx_ref[c] = jnp.maximum(mx_ref[c], m)

    def acc_tile(j, r):
        base = pl.multiple_of(j * tq, tq)
        for c in range(N_KV_HEADS):
            vc = kvb_ref[pl.ds(base, tq), vcol + c * HEAD_DIM_C:vcol + (c + 1) * HEAD_DIM_C]
            vext = jnp.concatenate([vc, jnp.ones((tq, lanes), BF16)], axis=1)
            m = mrep_ref[c]
            p = jnp.exp(logits(j, c, r) - jnp.concatenate([m] * halves, axis=-1))
            acc_ref[c] += _dot(p.reshape(G * tq, tq).astype(BF16), vext)

    def sweep(tile_fn):
        def far(j, carry):
            tile_fn(j, None)
            return carry

        lax.fori_loop(0, jnp.maximum(i - 1, 0), far, 0)

        @pl.when(i >= 1)
        def _():
            tile_fn(i - 1, 1)

        tile_fn(i, 0)

    mx_ref[...] = jnp.full(mx_ref.shape, NEG_BIG, F32)
    sweep(max_tile)
    mrep_ref[...] = jnp.broadcast_to(jnp.max(mx_ref[...], axis=-1, keepdims=True), mrep_ref.shape)
    acc_ref[...] = jnp.zeros(acc_ref.shape, F32)
    sweep(acc_tile)
    for c in range(N_KV_HEADS):
        for g in range(G):
            h = c * G + g
            a = acc_ref[c, g * tq:(g + 1) * tq, :]
            o_ref[:, h * HEAD_DIM_C:(h + 1) * HEAD_DIM_C] = (a[:, :lanes] / a[:, lanes:]).astype(o_ref.dtype)


def _dsa_prompt(qcqi, kvi_f32, kvi_bf16, rel_bias, *, tq, n_keep):
    S = qcqi.shape[0]
    dc = qcqi.shape[1] - N_IDX_HEADS * IDX_DIM
    n_heads = dc // HEAD_DIM_C
    G = n_heads // N_KV_HEADS
    wkv = kvi_bf16.shape[1]
    d = np.arange(tq)[:, None] - np.arange(tq)[None, :]
    btab = jnp.asarray(np.stack([_t5_bucket_np(d + r * tq) for r in range(2)]))
    assert _t5_bucket_np(np.array([tq + 1]))[0] == NUM_BUCKETS - 1
    slab = (2 * N_KV_HEADS * HEAD_DIM_C) // 128
    lanes = HEAD_DIM_C
    grid_spec = pltpu.PrefetchScalarGridSpec(
        num_scalar_prefetch=0,
        grid=(S // tq,),
        in_specs=[
            pl.BlockSpec(memory_space=pltpu.SMEM),
            pl.BlockSpec((tq, dc), lambda i: (i, 0)),
            pl.BlockSpec((tq, N_IDX_HEADS * IDX_DIM), lambda i: (i, dc // (N_IDX_HEADS * IDX_DIM))),
            pl.BlockSpec((tq, 128), lambda i: (i, slab)),
            pl.BlockSpec((S, wkv), lambda i: (0, 0), pipeline_mode=pl.Buffered(1)),
            pl.BlockSpec((2, tq, tq), lambda i: (0, 0, 0)),
        ],
        out_specs=pl.BlockSpec((tq, dc), lambda i: (i, 0)),
        scratch_shapes=[
            pltpu.VMEM((S // tq, tq, tq), I32),
            pltpu.VMEM((N_KV_HEADS, 2, G, tq, tq), F32),
            pltpu.VMEM((N_IDX_HEADS, tq, lanes), F32),
            pltpu.VMEM((N_IDX_HEADS, tq, IDX_DIM), BF16),
            pltpu.VMEM((N_KV_HEADS, G * tq, HEAD_DIM_C), BF16),
            pltpu.VMEM((N_KV_HEADS, G, tq, lanes), F32),
            pltpu.VMEM((N_KV_HEADS, G, tq, lanes), F32),
            pltpu.VMEM((N_KV_HEADS, G * tq, 2 * lanes), F32),
            pltpu.VMEM((tq, lanes), I32),
        ],
    )
    return pl.pallas_call(
        functools.partial(_dsa_prompt_body, tq=tq, n_keep=n_keep, idx_bits=(S - 1).bit_length()),
        grid_spec=grid_spec,
        out_shape=jax.ShapeDtypeStruct((S, dc), BF16),
        compiler_params=_cparams(("arbitrary",)),
        name="dsa_prompt",
    )(rel_bias, qcqi, qcqi, kvi_f32, kvi_bf16, btab)


def _dsa_sel_body(pt_ref, qi_ref, w_ref, kin_ref, *rest, t_new, n_keep):
    pages = rest[:PAGES_PER_STEP]
    mb_ref, keys_ref, cut_ref = rest[PAGES_PER_STEP:]
    p = pl.program_id(1)
    n_chunks = keys_ref.shape[0]
    tp = SAMPLE_PAD_T
    q = qi_ref[0]
    w = w_ref[0]

    def chunk_scores(s):
        s = jnp.maximum(s, 0.0) * w
        s = s.reshape(tp, N_IDX_HEADS, s.shape[-1]).sum(axis=1)
        return s * (N_IDX_HEADS ** -0.5 * IDX_DIM ** -0.5)

    for r in range(PAGES_PER_STEP):
        keys_ref[p * PAGES_PER_STEP + r] = _sort_key(chunk_scores(_dot(q, pages[r][0].astype(BF16))))

    @pl.when(p == pl.num_programs(1) - 1)
    def _():
        P = keys_ref.shape[-1]
        rowi = lax.broadcasted_iota(I32, (tp, P), 0)
        coli = lax.broadcasted_iota(I32, (tp, P), 1)
        vis_new = (coli <= rowi) & (coli < t_new)
        s_new = jnp.where(vis_new, chunk_scores(_dot_nt(q, kin_ref[0])), -jnp.inf)
        keys_ref[n_chunks - 1] = _sort_key(s_new)
        key_index = (lax.broadcasted_iota(I32, keys_ref.shape, 0) * P + lax.broadcasted_iota(I32, keys_ref.shape, 2))

        def count(hit):
            cnt = jnp.sum(jnp.where(hit, 1, 0), axis=0)
            return jnp.sum(cnt, axis=1, keepdims=True)

        def bit_step(b, tau):
            cand = tau + lax.shift_left(jnp.int32(1), 31 - b)
            return jnp.where(count(keys_ref[...] >= cand[None]) >= n_keep, cand, tau)

        tau = lax.fori_loop(0, 32, bit_step, jnp.full((tp, 1), INT_MIN, I32))

        cut_ref[...] = jnp.full((tp, 1), 2 ** 31 - 1, I32)

        @pl.when(jnp.max(count(keys_ref[...] >= tau[None])) > n_keep)
        def _():
            need = n_keep - count(keys_ref[...] > tau[None])
            idx_bits = (n_chunks * P - 1).bit_length()

            def idx_step(b, cut):
                cand = cut + lax.shift_left(jnp.int32(1), idx_bits - 1 - b)
                below = count((keys_ref[...] == tau[None]) & (key_index < cand[None]))
                return jnp.where(below < need, cand, cut)

            cut_ref[...] = lax.fori_loop(0, idx_bits, idx_step, jnp.zeros((tp, 1), I32))

        cut = cut_ref[...]
        k = keys_ref[...]
        sel = (k > tau[None]) | ((k == tau[None]) & (key_index <= cut[None]))
        mb_ref[0] = jnp.where(sel, 0.0, NEG_BIG)
        mb_ref[0, n_chunks - 1] = jnp.where(sel[n_chunks - 1] & vis_new, 0.0, NEG_BIG)


def _dsa_sample_select(pt_flat, qi, w, ki_new, pool_ki, *, n_pages, t_new, n_keep):
    B = qi.shape[0]
    P = pool_ki.shape[2]
    steps = n_pages // PAGES_PER_STEP

    def page_spec(r):
        return pl.BlockSpec((1, IDX_DIM, P), lambda b, p, pt: (pt[b * n_pages + p * PAGES_PER_STEP + r], 0, 0))

    grid_spec = pltpu.PrefetchScalarGridSpec(
        num_scalar_prefetch=1,
        grid=(B, steps),
        in_specs=[
            pl.BlockSpec((1,) + qi.shape[1:], lambda b, p, pt: (b, 0, 0)),
            pl.BlockSpec((1,) + w.shape[1:], lambda b, p, pt: (b, 0, 0)),
            pl.BlockSpec((1,) + ki_new.shape[1:], lambda b, p, pt: (b, 0, 0)),
        ] + [page_spec(r) for r in range(PAGES_PER_STEP)],
        out_specs=pl.BlockSpec((1, n_pages + 1, SAMPLE_PAD_T, P), lambda b, p, pt: (b, 0, 0, 0)),
        scratch_shapes=[pltpu.VMEM((n_pages + 1, SAMPLE_PAD_T, P), I32), pltpu.VMEM((SAMPLE_PAD_T, 1), I32)],
    )
    return pl.pallas_call(
        functools.partial(_dsa_sel_body, t_new=t_new, n_keep=n_keep),
        grid_spec=grid_spec,
        out_shape=jax.ShapeDtypeStruct((B, n_pages + 1, SAMPLE_PAD_T, P), F32),
        compiler_params=_cparams(("arbitrary", "arbitrary")),
        name="dsa_sample_select",
    )(pt_flat, qi, w, ki_new, *([pool_ki] * PAGES_PER_STEP))


def _dsa_att_body(pt_ref, rb_ref, q_ref, mb_ref, kn_ref, vn_ref, btab_ref, *rest):
    kp = rest[:PAGES_PER_STEP]
    vp = rest[PAGES_PER_STEP:2 * PAGES_PER_STEP]
    o_ref, bt_ref, m_ref, acc_ref = rest[2 * PAGES_PER_STEP:]
    b = pl.program_id(0)
    p = pl.program_id(1)
    n_chunks = mb_ref.shape[1]
    tp = SAMPLE_PAD_T
    G = q_ref.shape[2] // tp
    rows = G * tp
    P = kn_ref.shape[1]
    lanes = HEAD_DIM_C
    last = p == pl.num_programs(1) - 1

    @pl.when((b == 0) & (p == 0))
    def _():
        for c in range(N_KV_HEADS):
            for kind in range(2):
                for g in range(G):
                    h = c * G + g
                    bt_ref[c, kind, g * tp:(g + 1) * tp, :] = (_bias_from_buckets(btab_ref[kind], rb_ref, h)
                                                              - rb_ref[NUM_BUCKETS - 1, h])

    @pl.when(p == 0)
    def _():
        m_ref[...] = jnp.full(m_ref.shape, NEG_BIG, F32)
        acc_ref[...] = jnp.zeros(acc_ref.shape, F32)

    def attend(c, s, vext):
        m_old = m_ref[c]
        m_new = jnp.maximum(m_old, jnp.max(s, axis=1, keepdims=True))
        alpha = jnp.exp(m_old - m_new)
        pr = jnp.exp(s - jnp.concatenate([m_new] * (s.shape[1] // lanes), axis=1))
        acc_ref[c] = jnp.concatenate([alpha, alpha], axis=1) * acc_ref[c] + _dot(pr.astype(BF16), vext)
        m_ref[c] = m_new

    ones = jnp.ones((P, lanes), BF16)
    mb_step = jnp.concatenate([mb_ref[0, p * PAGES_PER_STEP + r] for r in range(PAGES_PER_STEP)], axis=1)
    mb_step = jnp.concatenate([mb_step] * G, axis=0)
    for c in range(N_KV_HEADS):
        head_rows = pl.ds(c, P, stride=N_KV_HEADS)
        k_all = jnp.concatenate([kp[r][0, head_rows, :].astype(BF16) for r in range(PAGES_PER_STEP)], axis=0)
        v_all = jnp.concatenate(
            [jnp.concatenate([vp[r][0, head_rows, :].astype(BF16), ones], axis=1) for r in range(PAGES_PER_STEP)], axis=0)
        s = _dot_nt(q_ref[0, c], k_all) + mb_step
        s = jnp.concatenate([s[:, :-P], s[:, -P:] + jnp.where(last, bt_ref[c, 0], 0.0)], axis=1)
        attend(c, s, v_all)

    @pl.when(last)
    def _():
        mb_new = jnp.concatenate([mb_ref[0, n_chunks - 1]] * G, axis=0)
        for c in range(N_KV_HEADS):
            cols = slice(c * HEAD_DIM_C, (c + 1) * HEAD_DIM_C)
            s = _dot_nt(q_ref[0, c], kn_ref[0, :, cols]) + mb_new + bt_ref[c, 1]
            attend(c, s, jnp.concatenate([vn_ref[0, :, cols], ones], axis=1))
            a = acc_ref[c]
            o_ref[0, c] = a[:, :lanes] / a[:, lanes:]


def _dsa_sample_attend(pt_flat, rel_bias, q, mb, k_new, v_new, pool_k, pool_v, *, n_pages, past):
    B, kvh, rows, dh = q.shape
    P = pool_k.shape[1] // kvh
    steps = n_pages // PAGES_PER_STEP
    tp = SAMPLE_PAD_T
    t = np.arange(tp)[:, None]
    col = np.arange(P)[None, :]
    assert past - (n_pages - 1) * P >= MAX_DISTANCE
    last_page = _t5_bucket_np(past + t - ((n_pages - 1) * P + col))
    new = _t5_bucket_np(t - col)
    btab = jnp.asarray(np.stack([last_page, new]))

    def page_spec(r):
        return pl.BlockSpec((1, P * kvh, dh), lambda b, p, pt: (pt[b * n_pages + p * PAGES_PER_STEP + r], 0, 0))

    grid_spec = pltpu.PrefetchScalarGridSpec(
        num_scalar_prefetch=1,
        grid=(B, steps),
        in_specs=[
            pl.BlockSpec(memory_space=pltpu.SMEM),
            pl.BlockSpec((1, kvh, rows, dh), lambda b, p, pt: (b, 0, 0, 0)),
            pl.BlockSpec((1,) + mb.shape[1:], lambda b, p, pt: (b, 0, 0, 0)),
            pl.BlockSpec((1,) + k_new.shape[1:], lambda b, p, pt: (b, 0, 0)),
            pl.BlockSpec((1,) + v_new.shape[1:], lambda b, p, pt: (b, 0, 0)),
            pl.BlockSpec((2, tp, P), lambda b, p, pt: (0, 0, 0)),
        ] + [page_spec(r) for r in range(PAGES_PER_STEP)] * 2,
        out_specs=pl.BlockSpec((1, kvh, rows, dh), lambda b, p, pt: (b, 0, 0, 0)),
        scratch_shapes=[
            pltpu.VMEM((kvh, 2, rows, P), F32),
            pltpu.VMEM((kvh, rows, dh), F32),
            pltpu.VMEM((kvh, rows, 2 * dh), F32),
        ],
    )
    return pl.pallas_call(
        _dsa_att_body,
        grid_spec=grid_spec,
        out_shape=jax.ShapeDtypeStruct((B, kvh, rows, dh), F32),
        compiler_params=_cparams(("arbitrary", "arbitrary")),
        name="dsa_sample_attend",
    )(pt_flat, rel_bias, q, mb, k_new, v_new, btab, *([pool_k] * PAGES_PER_STEP), *([pool_v] * PAGES_PER_STEP))


def _dsa_sample(c1, c2f, c2b, pt_flat, rel_bias, pool_k, pool_v, pool_ki, *, Bs, Ts, n_pages):
    tp = SAMPLE_PAD_T
    kv_w = N_KV_HEADS * HEAD_DIM_C
    d_c = c1.shape[1] - N_IDX_HEADS * IDX_DIM
    G = d_c // kv_w
    P = pool_ki.shape[2]
    past = n_pages * P
    qi_s = c1[:, d_c:].reshape(Bs, tp * N_IDX_HEADS, IDX_DIM)
    w_s = c2f[:, 2 * kv_w + IDX_DIM:2 * kv_w + IDX_DIM + N_IDX_HEADS].reshape(Bs, tp * N_IDX_HEADS, 1)
    new_rows = jnp.pad(c2b.reshape(Bs, tp, -1), ((0, 0), (0, P - tp), (0, 0)))
    mb = _dsa_sample_select(pt_flat, qi_s, w_s, new_rows[:, :, 2 * kv_w:2 * kv_w + IDX_DIM], pool_ki,
                            n_pages=n_pages, t_new=Ts, n_keep=min(TOPK_MAX, (past + Ts) // 4))
    q_s = c1[:, :d_c].reshape(Bs, tp, N_KV_HEADS, G, HEAD_DIM_C).transpose(0, 2, 3, 1, 4)
    q_s = q_s.reshape(Bs, N_KV_HEADS, G * tp, HEAD_DIM_C)
    o_s = _dsa_sample_attend(pt_flat, rel_bias, q_s, mb, new_rows[:, :, :kv_w], new_rows[:, :, kv_w:2 * kv_w],
                             pool_k, pool_v, n_pages=n_pages, past=past)
    return o_s.reshape(Bs, N_KV_HEADS, G, tp, HEAD_DIM_C).transpose(0, 3, 1, 2, 4).reshape(Bs * tp, d_c).astype(BF16)


def _split_w_in(w, d_a, d_b, d_c, rk, kv_w, qi_w):
    o = np.cumsum([0, d_a, d_a, rk, rk, d_b, d_b, d_c, kv_w, kv_w, qi_w, IDX_DIM, N_IDX_HEADS])
    wb = w.astype(BF16)
    grp_a = wb[:, o[0]:o[2]]
    grp_b = wb[:, o[2]:o[6]]
    grp_c1 = jnp.concatenate([wb[:, o[6]:o[7]], wb[:, o[9]:o[10]]], axis=1)
    tail = wb[:, o[10]:o[12]]
    pad = 128 - tail.shape[1]
    grp_c2 = jnp.concatenate([wb[:, o[7]:o[9]], tail, jnp.zeros((w.shape[0], pad), BF16)], axis=1)
    return grp_a, grp_b, grp_c1, grp_c2


def kernel(x_prompt, x_sample, cache_k, cache_v, cache_kidx, page_table, state_lru_h, state_conv, state_ret,
           norm_ffn1, ffn1_gate, ffn1_up, ffn1_down, norm_mix, w_in, w_out, conv_w, conv_b,
           lru_wa, lru_ba, lru_wx, lru_bx, lru_lambda, rel_bias,
           norm_ffn2, ffn2_gate, ffn2_up, ffn2_down, norm_final):
    depth = norm_ffn1.shape[0]
    _, S, D = x_prompt.shape
    Bs, Ts, _ = x_sample.shape
    n_pool, P = cache_k.shape[1], cache_k.shape[2]
    n_pages = page_table.shape[1]
    past = n_pages * P
    d_a = state_lru_h.shape[-1]
    H_r, rdk, rdv = state_ret.shape[2:]
    d_b = H_r * rdv
    d_c = D - d_a - d_b
    kv_w = N_KV_HEADS * HEAD_DIM_C
    qi_w = N_IDX_HEADS * IDX_DIM
    G = d_c // HEAD_DIM_C // N_KV_HEADS
    tp = SAMPLE_PAD_T
    Rs = Bs * tp

    xp = x_prompt.reshape(S, D)
    xs = jnp.pad(x_sample, ((0, 0), (0, tp - Ts), (0, 0))).reshape(Rs, D)

    cos_p, sin_p = _rope_tables(jnp.arange(S, dtype=I32), rdk, H_r)
    cos_s, sin_s = _rope_tables(past + jnp.arange(tp, dtype=I32), rdk, H_r)
    colscale = jnp.concatenate([jnp.full((d_c,), HEAD_DIM_C ** -0.5, F32), jnp.ones((qi_w,), F32)])

    pool_k = cache_k.reshape(depth * n_pool, P * N_KV_HEADS, HEAD_DIM_C)
    pool_v = cache_v.reshape(depth * n_pool, P * N_KV_HEADS, HEAD_DIM_C)
    pool_ki = jnp.swapaxes(cache_kidx, 2, 3).reshape(depth * n_pool, IDX_DIM, P)

    zeros_conv = jnp.zeros((1, 8, d_a), F32)
    zeros_h = jnp.zeros((1, 1, d_a), F32)
    zeros_s = jnp.zeros((1, H_r, rdk, rdv), F32)

    outs_p, outs_s = [], []
    y_prompt = y_sample = None
    for l in range(depth):
        bf = lambda a: a[l].astype(BF16)
        f1 = (bf(ffn1_gate), bf(ffn1_up), bf(ffn1_down))
        f2 = (bf(ffn2_gate), bf(ffn2_up), bf(ffn2_down))
        w_a, w_b, w_c1, w_c2 = _split_w_in(w_in[l], d_a, d_b, d_c, H_r * rdk, kv_w, qi_w)
        wo = w_out[l].astype(BF16)
        wo_a, wo_b, wo_c = wo[:d_a], wo[d_a:d_a + d_b], wo[d_a + d_b:]
        lwa, lwx = lru_wa[l].astype(BF16), lru_wx[l].astype(BF16)
        last = l == depth - 1
        pt_flat = (page_table + l * n_pool).reshape(-1).astype(I32)

        xp = _ffn(xp, norm_ffn1[l], *f1, tm=512, tf=512)
        proj = functools.partial(_norm_matmul, xp, norm_mix[l])
        (ua,) = proj(w_a, tm=512, tn=1024, out_f32=True, out_bf16=False)
        (ub,) = proj(w_b, tm=512, tn=768, out_f32=True, out_bf16=False)
        (c1,) = proj(w_c1, colscale, tm=512, tn=1024, out_f32=False, out_bf16=True)
        c2f, c2b = proj(w_c2, tm=512, tn=w_c2.shape[1], out_f32=True, out_bf16=True)
        ya, h_last, conv_new = _lru(ua.reshape(1, S, 2 * d_a), zeros_conv, zeros_h, conv_w[l], conv_b[l],
                                    lwa, lru_ba[l], lwx, lru_bx[l], lru_lambda[l], tc=1024, t_valid_last=1024)
        yb, s_last = _retention(ub.reshape(1, S, -1), cos_p, sin_p, zeros_s, cp=128, c_valid=128, mm_dtype=BF16)
        yc = _dsa_prompt(c1, c2f, c2b, rel_bias, tq=256, n_keep=min(TOPK_MAX, S // 4))
        xp = _out_proj(xp, ya.reshape(S, d_a), yb.reshape(S, d_b), yc, wo_a, wo_b, wo_c, tm=512, tn=1024)
        if last:
            xp, y_prompt = _ffn(xp, norm_ffn2[l], *f2, norm_final, tm=512, tf=512)
        else:
            xp = _ffn(xp, norm_ffn2[l], *f2, tm=512, tf=512)
        outs_p.append((c2f[:, :kv_w].reshape(1, S, N_KV_HEADS, HEAD_DIM_C),
                       c2f[:, kv_w:2 * kv_w].reshape(1, S, N_KV_HEADS, HEAD_DIM_C),
                       c2f[:, 2 * kv_w:2 * kv_w + IDX_DIM].reshape(1, S, IDX_DIM),
                       h_last.reshape(1, d_a), conv_new, s_last))

        xs = _ffn(xs, norm_ffn1[l], *f1, tm=Rs, tf=512)
        proj = functools.partial(_norm_matmul, xs, norm_mix[l])
        (ua,) = proj(w_a, tm=Rs, tn=1024, out_f32=True, out_bf16=False)
        (ub,) = proj(w_b, tm=Rs, tn=768, out_f32=True, out_bf16=False)
        (c1,) = proj(w_c1, colscale, tm=Rs, tn=1024, out_f32=False, out_bf16=True)
        c2f, c2b = proj(w_c2, tm=Rs, tn=w_c2.shape[1], out_f32=True, out_bf16=True)
        conv0 = jnp.pad(state_conv[l], ((0, 0), (8 - (CONV_WIDTH - 1), 0), (0, 0)))
        ya, h_last, conv_new = _lru(ua.reshape(Bs, tp, 2 * d_a), conv0, state_lru_h[l].reshape(Bs, 1, d_a),
                                    conv_w[l], conv_b[l], lwa, lru_ba[l], lwx, lru_bx[l], lru_lambda[l],
                                    tc=tp, t_valid_last=Ts)
        yb, s_last = _retention(ub.reshape(Bs, tp, -1), cos_s, sin_s, state_ret[l], cp=tp, c_valid=Ts, mm_dtype=F32)
        yc = _dsa_sample(c1, c2f, c2b, pt_flat, rel_bias, pool_k, pool_v, pool_ki, Bs=Bs, Ts=Ts, n_pages=n_pages)
        xs = _out_proj(xs, ya.reshape(Rs, d_a), yb.reshape(Rs, d_b), yc, wo_a, wo_b, wo_c, tm=Rs, tn=1024)
        if last:
            xs, y_sample = _ffn(xs, norm_ffn2[l], *f2, norm_final, tm=Rs, tf=512)
        else:
            xs = _ffn(xs, norm_ffn2[l], *f2, tm=Rs, tf=512)
        c2s = c2f.reshape(Bs, tp, -1)[:, :Ts]
        outs_s.append((c2s[:, :, :kv_w].reshape(Bs, Ts, N_KV_HEADS, HEAD_DIM_C),
                       c2s[:, :, kv_w:2 * kv_w].reshape(Bs, Ts, N_KV_HEADS, HEAD_DIM_C),
                       c2s[:, :, 2 * kv_w:2 * kv_w + IDX_DIM],
                       h_last.reshape(Bs, d_a), conv_new, s_last))

    stack = lambda outs, k: jnp.stack([o[k] for o in outs])
    return ((y_prompt.reshape(1, S, D), y_sample.reshape(Bs, tp, D)[:, :Ts])
            + tuple(stack(outs_p, k) for k in range(6))
            + tuple(stack(outs_s, k) for k in range(6)))
```

```python
import functools
import math

import numpy as np
import jax
import jax.numpy as jnp
from jax import lax
from jax.experimental import pallas as pl
from jax.experimental.pallas import tpu as pltpu

F32 = jnp.float32
BF16 = jnp.bfloat16
I32 = jnp.int32

EPS = 1e-6
N_A_BLOCKS = 4
CONV_WIDTH = 4
LRU_C = 8.0
N_RET_HEADS = 4
ROPE_BASE = 10000.0
HEAD_DIM_C = 128
N_KV_HEADS = 2
N_IDX_HEADS = 16
IDX_DIM = 64
TOPK_MAX = 256
NUM_BUCKETS = 32
MAX_DISTANCE = 128
SAMPLE_PAD_T = 8
PAGES_PER_STEP = 16
NEG_BIG = -1e30
INT_MIN = -2 ** 31

VMEM_LIMIT = 56 * 1024 * 1024


def _cparams(sem):
    return pltpu.CompilerParams(dimension_semantics=sem, vmem_limit_bytes=VMEM_LIMIT)


def _rms(x, g):
    return x * lax.rsqrt(jnp.mean(x * x, axis=-1, keepdims=True) + EPS) * g


def _dot(a, b):
    return jnp.dot(a, b, preferred_element_type=F32)


def _dot_nt(a, b):
    return lax.dot_general(a, b, (((1,), (1,)), ((), ())), preferred_element_type=F32)


def _dot_tn(a, b):
    return lax.dot_general(a, b, (((0,), (0,)), ((), ())), preferred_element_type=F32)


def _sort_key(score):
    bits = lax.bitcast_convert_type(score, I32)
    return jnp.where(bits < 0, bits ^ jnp.int32(0x7FFFFFFF), bits)


def _ffn_body(x_ref, g_ref, wg_ref, wu_ref, wd_ref, *rest, final_norm):
    if final_norm:
        gf_ref, o_ref, on_ref, h_ref = rest
    else:
        o_ref, h_ref = rest
    j = pl.program_id(1)

    @pl.when(j == 0)
    def _():
        h_ref[...] = _rms(x_ref[...], g_ref[...]).astype(BF16)
        o_ref[...] = jnp.zeros_like(o_ref)

    h = h_ref[...]
    g = _dot(h, wg_ref[...])
    u = _dot(h, wu_ref[...])
    a = (g * jax.nn.sigmoid(g) * u).astype(BF16)
    o_ref[...] += _dot(a, wd_ref[...])

    @pl.when(j == pl.num_programs(1) - 1)
    def _():
        y = x_ref[...] + 0.5 * o_ref[...]
        o_ref[...] = y
        if final_norm:
            on_ref[...] = _rms(y, gf_ref[...])


def _ffn(x, g, wg, wu, wd, gf=None, *, tm, tf):
    R, D = x.shape
    FF = wg.shape[1]
    final_norm = gf is not None
    in_specs = [
        pl.BlockSpec((tm, D), lambda i, j: (i, 0)),
        pl.BlockSpec((1, D), lambda i, j: (0, 0)),
        pl.BlockSpec((D, tf), lambda i, j: (0, j)),
        pl.BlockSpec((D, tf), lambda i, j: (0, j)),
        pl.BlockSpec((tf, D), lambda i, j: (j, 0)),
    ]
    args = [x, g.reshape(1, D), wg, wu, wd]
    out_shape = [jax.ShapeDtypeStruct((R, D), F32)]
    out_specs = [pl.BlockSpec((tm, D), lambda i, j: (i, 0))]
    if final_norm:
        in_specs.append(pl.BlockSpec((1, D), lambda i, j: (0, 0)))
        args.append(gf.reshape(1, D))
        out_shape.append(jax.ShapeDtypeStruct((R, D), F32))
        out_specs.append(pl.BlockSpec((tm, D), lambda i, j: (i, 0)))
    res = pl.pallas_call(
        functools.partial(_ffn_body, final_norm=final_norm),
        grid=(R // tm, FF // tf),
        in_specs=in_specs,
        out_specs=out_specs,
        out_shape=out_shape,
        scratch_shapes=[pltpu.VMEM((tm, D), BF16)],
        compiler_params=_cparams(("parallel", "arbitrary")),
        name="ffn",
    )(*args)
    return res if final_norm else res[0]


def _nmm_body(x_ref, g_ref, w_ref, *rest, out_f32, out_bf16, colscale):
    rest = list(rest)
    cs_ref = rest.pop(0) if colscale else None
    of_ref = rest.pop(0) if out_f32 else None
    ob_ref = rest.pop(0) if out_bf16 else None
    h_ref = rest.pop(0)

    @pl.when(pl.program_id(1) == 0)
    def _():
        h_ref[...] = _rms(x_ref[...], g_ref[...]).astype(BF16)

    r = _dot(h_ref[...], w_ref[...])
    if colscale:
        r = r * cs_ref[...]
    if out_f32:
        of_ref[...] = r
    if out_bf16:
        ob_ref[...] = r.astype(BF16)


def _norm_matmul(x, g, w, colscale=None, *, tm, tn, out_f32, out_bf16):
    R, D = x.shape
    N = w.shape[1]
    in_specs = [
        pl.BlockSpec((tm, D), lambda i, j: (i, 0)),
        pl.BlockSpec((1, D), lambda i, j: (0, 0)),
        pl.BlockSpec((D, tn), lambda i, j: (0, j)),
    ]
    args = [x, g.reshape(1, D), w]
    if colscale is not None:
        in_specs.append(pl.BlockSpec((1, tn), lambda i, j: (0, j)))
        args.append(colscale.reshape(1, N))
    out_shape, out_specs = [], []
    for flag, dt in ((out_f32, F32), (out_bf16, BF16)):
        if flag:
            out_shape.append(jax.ShapeDtypeStruct((R, N), dt))
            out_specs.append(pl.BlockSpec((tm, tn), lambda i, j: (i, j)))
    res = pl.pallas_call(
        functools.partial(_nmm_body, out_f32=out_f32, out_bf16=out_bf16, colscale=colscale is not None),
        grid=(R // tm, N // tn),
        in_specs=in_specs,
        out_specs=out_specs,
        out_shape=out_shape,
        scratch_shapes=[pltpu.VMEM((tm, D), BF16)],
        compiler_params=_cparams(("parallel", "arbitrary")),
        name="norm_matmul",
    )(*args)
    return res


def _kv_proj_body(x_ref, g_ref, w_ref, k_ref, v_ref, ki_ref, tail_ref, b_ref):
    r = _dot(_rms(x_ref[...], g_ref[...]).astype(BF16), w_ref[...])
    kv_w = k_ref.shape[1]
    k_ref[...] = r[:, :kv_w]
    v_ref[...] = r[:, kv_w:2 * kv_w]
    ki_ref[...] = r[:, 2 * kv_w:2 * kv_w + IDX_DIM]
    tail_ref[...] = r[:, 2 * kv_w:]
    b_ref[...] = r.astype(BF16)


def _kv_proj(x, g, w, *, tm):
    R, D = x.shape
    N = w.shape[1]
    kv_w = N_KV_HEADS * HEAD_DIM_C
    tail_w = N - 2 * kv_w
    widths = (kv_w, kv_w, IDX_DIM, tail_w)
    return pl.pallas_call(
        _kv_proj_body,
        grid=(R // tm,),
        in_specs=[pl.BlockSpec((tm, D), lambda i: (i, 0)), pl.BlockSpec((1, D), lambda i: (0, 0)),
                  pl.BlockSpec((D, N), lambda i: (0, 0))],
        out_specs=[pl.BlockSpec((tm, wd), lambda i: (i, 0)) for wd in widths + (N,)],
        out_shape=[jax.ShapeDtypeStruct((R, wd), F32) for wd in widths] + [jax.ShapeDtypeStruct((R, N), BF16)],
        compiler_params=_cparams(("parallel",)),
        name="kv_proj",
    )(x, g.reshape(1, D), w)


def _oproj_body(x_ref, ya_ref, yb_ref, yc_ref, wa_ref, wb_ref, wc_ref, o_ref):
    o_ref[...] = (x_ref[...] + _dot(ya_ref[...], wa_ref[...]) + _dot(yb_ref[...], wb_ref[...])
                  + _dot(yc_ref[...], wc_ref[...]))


def _out_proj(x, ya, yb, yc, wa, wb, wc, *, tm, tn):
    R, D = x.shape
    row = lambda w: pl.BlockSpec((tm, w), lambda i, j: (i, 0))
    col = lambda k: pl.BlockSpec((k, tn), lambda i, j: (0, j))
    return pl.pallas_call(
        _oproj_body,
        grid=(R // tm, D // tn),
        in_specs=[pl.BlockSpec((tm, tn), lambda i, j: (i, j)), row(ya.shape[1]), row(yb.shape[1]), row(yc.shape[1]),
                  col(wa.shape[0]), col(wb.shape[0]), col(wc.shape[0])],
        out_specs=pl.BlockSpec((tm, tn), lambda i, j: (i, j)),
        out_shape=jax.ShapeDtypeStruct((R, D), F32),
        compiler_params=_cparams(("parallel", "arbitrary")),
        name="out_proj",
    )(x, ya, yb, yc, wa, wb, wc)


def _lru_body(xa_ref, ga_ref, conv0_ref, h0_ref, cw_ref, cb_ref, wa_ref, ba_ref, wx_ref, bx_ref, lam_ref,
              ya_ref, hl_ref, cn_ref, xs_ref, a_ref, u_ref, hs_ref, hc_ref, *, tc, t_valid_last):
    t = pl.program_id(1)
    da = xa_ref.shape[-1]
    blk = da // N_A_BLOCKS
    tail = 8

    @pl.when(t == 0)
    def _():
        xs_ref[0:tail, :] = conv0_ref[0]
        hc_ref[0:1, :] = h0_ref[0]

    xs_ref[tail:tail + tc, :] = xa_ref[0]
    xc = cb_ref[...]
    for j in range(CONV_WIDTH):
        off = tail - (CONV_WIDTH - 1) + j
        xc = xc + xs_ref[off:off + tc, :] * cw_ref[j:j + 1, :]
    xcb = xc.astype(BF16)
    pre_r = jnp.concatenate([_dot(xcb[:, n * blk:(n + 1) * blk], wa_ref[n]) for n in range(N_A_BLOCKS)], axis=1)
    pre_i = jnp.concatenate([_dot(xcb[:, n * blk:(n + 1) * blk], wx_ref[n]) for n in range(N_A_BLOCKS)], axis=1)
    r = jax.nn.sigmoid(pre_r + ba_ref[...])
    gi = jax.nn.sigmoid(pre_i + bx_ref[...])
    z = -lam_ref[...]
    softplus = jnp.maximum(z, 0.0) + jnp.log1p(jnp.exp(-jnp.abs(z)))
    log_a = (-LRU_C) * r * softplus
    a = jnp.exp(log_a)
    a_ref[...] = a
    u_ref[...] = jnp.sqrt(-jnp.tanh(log_a) * (a * a + 1.0)) * gi * xc

    def step(k, h):
        base = pl.multiple_of(k * 8, 8)
        for rr in range(8):
            h = a_ref[pl.ds(base + rr, 1), :] * h + u_ref[pl.ds(base + rr, 1), :]
            hs_ref[pl.ds(base + rr, 1), :] = h
        return h

    h = lax.fori_loop(0, tc // 8, step, hc_ref[0:1, :])
    hc_ref[0:1, :] = h
    ya_ref[0] = (hs_ref[...] * jax.nn.gelu(ga_ref[0])).astype(ya_ref.dtype)

    @pl.when(t == pl.num_programs(1) - 1)
    def _():
        hl_ref[0] = hs_ref[t_valid_last - 1:t_valid_last, :]
        lo = tail - (CONV_WIDTH - 1) + t_valid_last
        cn_ref[0] = xs_ref[lo:lo + CONV_WIDTH - 1, :]

    xs_ref[0:tail, :] = xs_ref[tc:tc + tail, :]


def _lru(ua, conv0, h0, cw, cb, wa, ba, wx, bx, lam, *, tc, t_valid_last):
    B, T, da2 = ua.shape
    da = da2 // 2
    vec = lambda: pl.BlockSpec((1, da), lambda b, t: (0, 0))
    blk = da // N_A_BLOCKS
    return pl.pallas_call(
        functools.partial(_lru_body, tc=tc, t_valid_last=t_valid_last),
        grid=(B, T // tc),
        in_specs=[
            pl.BlockSpec((1, tc, da), lambda b, t: (b, t, 0)),
            pl.BlockSpec((1, tc, da), lambda b, t: (b, t, 1)),
            pl.BlockSpec((1, 8, da), lambda b, t: (b, 0, 0)),
            pl.BlockSpec((1, 1, da), lambda b, t: (b, 0, 0)),
            pl.BlockSpec((CONV_WIDTH, da), lambda b, t: (0, 0)),
            vec(),
            pl.BlockSpec((N_A_BLOCKS, blk, blk), lambda b, t: (0, 0, 0)),
            vec(),
            pl.BlockSpec((N_A_BLOCKS, blk, blk), lambda b, t: (0, 0, 0)),
            vec(),
            vec(),
        ],
        out_specs=[
            pl.BlockSpec((1, tc, da), lambda b, t: (b, t, 0)),
            pl.BlockSpec((1, 1, da), lambda b, t: (b, 0, 0)),
            pl.BlockSpec((1, CONV_WIDTH - 1, da), lambda b, t: (b, 0, 0)),
        ],
        out_shape=[
            jax.ShapeDtypeStruct((B, T, da), BF16),
            jax.ShapeDtypeStruct((B, 1, da), F32),
            jax.ShapeDtypeStruct((B, CONV_WIDTH - 1, da), F32),
        ],
        scratch_shapes=[
            pltpu.VMEM((tc + 8, da), F32),
            pltpu.VMEM((tc, da), F32),
            pltpu.VMEM((tc, da), F32),
            pltpu.VMEM((tc, da), F32),
            pltpu.VMEM((8, da), F32),
        ],
        compiler_params=_cparams(("arbitrary", "arbitrary")),
        name="rglru",
    )(ua, ua, conv0, h0, cw, cb.reshape(1, da), wa, ba.reshape(1, da), wx, bx.reshape(1, da), lam.reshape(1, da))


def _ret_body(q_ref, k_ref, v_ref, gb_ref, cos_ref, sin_ref, s0_ref, yb_ref, sl_ref, s_ref, *, cp, c_valid, mm_dtype):
    t = pl.program_id(1)
    dk = q_ref.shape[-1] // N_RET_HEADS
    dv = v_ref.shape[-1] // N_RET_HEADS

    @pl.when(t == 0)
    def _():
        s_ref[...] = s0_ref[0]

    cos = cos_ref[...]
    sin = sin_ref[...]
    first_half = (lax.broadcasted_iota(I32, cos.shape, 1) % dk) < (dk // 2)
    width = cos.shape[1]

    def rot(x):
        partner = jnp.where(first_half, pltpu.roll(x, width - dk // 2, 1), pltpu.roll(x, dk // 2, 1))
        return x * cos + partner * sin

    row = lax.broadcasted_iota(I32, (cp, 1), 0)
    q = rot(q_ref[0])
    k = rot(k_ref[0]) * (dk ** -0.5)
    if c_valid < cp:
        k = jnp.where(row < c_valid, k, 0.0)
    v = v_ref[0]
    gb = gb_ref[0]
    ri = lax.broadcasted_iota(I32, (cp, cp), 0)
    ci = lax.broadcasted_iota(I32, (cp, cp), 1)
    diff = (ri - ci).astype(F32)
    rowf = row.astype(F32)
    for h in range(N_RET_HEADS):
        lg = math.log1p(-(2.0 ** (-5.0 - h)))
        dmask = jnp.where(diff >= 0, jnp.exp(jnp.maximum(diff, 0.0) * lg), 0.0)
        cross_dec = jnp.exp((rowf + 1.0) * lg)
        state_dec = jnp.exp((c_valid - 1.0 - rowf) * lg)
        chunk_dec = math.exp(c_valid * lg)
        qh = q[:, h * dk:(h + 1) * dk].astype(mm_dtype)
        kh = k[:, h * dk:(h + 1) * dk]
        vh = v[:, h * dv:(h + 1) * dv].astype(mm_dtype)
        s_old = s_ref[h]
        att = _dot_nt(qh, kh.astype(mm_dtype)) * dmask
        inner = _dot(att.astype(mm_dtype), vh)
        cross = _dot(qh, s_old.astype(mm_dtype)) * cross_dec
        s_ref[h] = s_old * chunk_dec + _dot_tn((kh * state_dec).astype(mm_dtype), vh)
        o = inner + cross
        o = o * lax.rsqrt(jnp.mean(o * o, axis=-1, keepdims=True) + EPS)
        gh = gb[:, h * dv:(h + 1) * dv]
        yb_ref[0, :, h * dv:(h + 1) * dv] = (o * (gh * jax.nn.sigmoid(gh))).astype(yb_ref.dtype)

    @pl.when(t == pl.num_programs(1) - 1)
    def _():
        sl_ref[0] = s_ref[...]


def _retention(ub, cos, sin, s0, *, cp, c_valid, mm_dtype):
    B, T, wtot = ub.shape
    w = wtot // 6
    H, dk, dv = s0.shape[1:]
    return pl.pallas_call(
        functools.partial(_ret_body, cp=cp, c_valid=c_valid, mm_dtype=mm_dtype),
        grid=(B, T // cp),
        in_specs=[
            pl.BlockSpec((1, cp, w), lambda b, t: (b, t, 0)),
            pl.BlockSpec((1, cp, w), lambda b, t: (b, t, 1)),
            pl.BlockSpec((1, cp, 2 * w), lambda b, t: (b, t, 1)),
            pl.BlockSpec((1, cp, 2 * w), lambda b, t: (b, t, 2)),
            pl.BlockSpec((cp, w), lambda b, t: (t, 0)),
            pl.BlockSpec((cp, w), lambda b, t: (t, 0)),
            pl.BlockSpec((1, H, dk, dv), lambda b, t: (b, 0, 0, 0)),
        ],
        out_specs=[
            pl.BlockSpec((1, cp, 2 * w), lambda b, t: (b, t, 0)),
            pl.BlockSpec((1, H, dk, dv), lambda b, t: (b, 0, 0, 0)),
        ],
        out_shape=[
            jax.ShapeDtypeStruct((B, T, 2 * w), BF16),
            jax.ShapeDtypeStruct((B, H, dk, dv), F32),
        ],
        scratch_shapes=[pltpu.VMEM((H, dk, dv), F32)],
        compiler_params=_cparams(("arbitrary", "arbitrary")),
        name="retention",
    )(ub, ub, ub, ub, cos, sin, s0)


def _rope_tables(pos, dk, heads):
    half = dk // 2
    freqs = ROPE_BASE ** (-jnp.arange(half, dtype=F32) / half)
    ang = pos.astype(F32)[:, None] * freqs[None, :]
    cos, sin = jnp.cos(ang), jnp.sin(ang)
    cos_t = jnp.tile(jnp.concatenate([cos, cos], axis=1), (1, heads))
    sin_t = jnp.tile(jnp.concatenate([-sin, sin], axis=1), (1, heads))
    return cos_t, sin_t


def _t5_bucket_np(dist):
    n = np.maximum(dist, 0)
    max_exact = NUM_BUCKETS // 2
    ratio = np.log(np.maximum(n, 1).astype(np.float32) / np.float32(max_exact)) / np.float32(math.log(MAX_DISTANCE / max_exact))
    large = max_exact + (ratio * np.float32(NUM_BUCKETS - max_exact)).astype(np.int32)
    large = np.minimum(large, NUM_BUCKETS - 1)
    return np.where(n < max_exact, n, large).astype(np.int32)


def _bias_from_buckets(bucket, rb_ref, head):
    def step(b, out):
        return jnp.where(bucket == b, rb_ref[b, head], out)

    return lax.fori_loop(0, NUM_BUCKETS, step, jnp.zeros(bucket.shape, F32))


def _dsa_prompt_body(rb_ref, qc_ref, qi_ref, wi_ref, kvb_ref, btab_ref, o_ref,
                     keys_ref, bt_ref, wf_ref, qis_ref, qs_ref, mx_ref, mrep_ref, acc_ref, cut_ref, kmax_ref,
                     *, tq, n_keep, idx_bits):
    i = pl.program_id(0)
    G = qc_ref.shape[1] // (N_KV_HEADS * HEAD_DIM_C)
    n_heads = N_KV_HEADS * G
    kcol, vcol, icol = 0, N_KV_HEADS * HEAD_DIM_C, 2 * N_KV_HEADS * HEAD_DIM_C
    lanes = HEAD_DIM_C
    halves = tq // lanes
    n_tiles = i + 1

    @pl.when(i == 0)
    def _():
        for h in range(n_heads):
            far = rb_ref[NUM_BUCKETS - 1, h]
            for r in range(2):
                bt_ref[h // G, r, h % G] = _bias_from_buckets(btab_ref[r], rb_ref, h) - far

    for h in range(N_IDX_HEADS):
        qis_ref[h] = qi_ref[:, h * IDX_DIM:(h + 1) * IDX_DIM]
        wf_ref[h] = jnp.broadcast_to(wi_ref[:, IDX_DIM + h:IDX_DIM + h + 1], (tq, lanes))
    for c in range(N_KV_HEADS):
        for g in range(G):
            h = c * G + g
            qs_ref[c, g * tq:(g + 1) * tq, :] = qc_ref[:, h * HEAD_DIM_C:(h + 1) * HEAD_DIM_C]

    rowi = lax.broadcasted_iota(I32, (tq, tq), 0)
    coli = lax.broadcasted_iota(I32, (tq, tq), 1)

    def key_index(j):
        return j * tq + coli

    def visible(j):
        return key_index(j) <= (i * tq + rowi)

    def score_tile(j, carry):
        ki = kvb_ref[pl.ds(pl.multiple_of(j * tq, tq), tq), icol:icol + IDX_DIM]
        for rh in range(halves):
            rows = slice(rh * lanes, (rh + 1) * lanes)
            acc = jnp.zeros((lanes, tq), F32)
            for h in range(N_IDX_HEADS):
                w = wf_ref[h, rows, :]
                acc = acc + jnp.maximum(_dot_nt(qis_ref[h, rows, :], ki), 0.0) * jnp.concatenate([w] * halves, axis=1)
            score = acc * (N_IDX_HEADS ** -0.5 * IDX_DIM ** -0.5)
            score = jnp.where(visible(j)[rows], score, -jnp.inf)
            keys_ref[j, rows, :] = _sort_key(score)
        return carry

    lax.fori_loop(0, n_tiles, score_tile, 0)

    ones_sq = jnp.ones((lanes, lanes), BF16)

    def count(pred, *row_args):
        totals = []
        for rh in range(halves):
            rows = slice(rh * lanes, (rh + 1) * lanes)
            args = [a[rows] for a in row_args]

            def body(j, cnt):
                for a in range(halves):
                    cols = slice(a * lanes, (a + 1) * lanes)
                    idx = j * tq + a * lanes + lax.broadcasted_iota(I32, (lanes, lanes), 1)
                    cnt = cnt + jnp.where(pred(keys_ref[j, rows, cols], idx, *args), 1, 0)
                return cnt

            totals.append(lax.fori_loop(0, n_tiles, body, jnp.zeros((lanes, lanes), I32)))
        return _dot(jnp.concatenate(totals, axis=0).astype(F32).astype(BF16), ones_sq).astype(I32)

    def bit_step(b, st):
        tau, n_ge = st
        cand = tau + lax.shift_left(jnp.int32(1), 31 - b)
        c = count(lambda k, idx, cnd: k >= cnd, cand)
        ge = c >= n_keep
        return jnp.where(ge, cand, tau), jnp.where(ge, c, n_ge)

    tau, n_ge = lax.fori_loop(0, 32, bit_step, (jnp.full((tq, lanes), INT_MIN, I32),
                                                jnp.broadcast_to(n_tiles * tq, (tq, lanes)).astype(I32)))

    cut_ref[...] = jnp.full((tq, lanes), 2 ** 31 - 1, I32)

    @pl.when(jnp.max(n_ge) > n_keep)
    def _():
        need = n_keep - count(lambda k, idx, t: k > t, tau)

        def idx_step(b, cut):
            cand = cut + lax.shift_left(jnp.int32(1), idx_bits - 1 - b)
            below = count(lambda k, idx, t, c: (k == t) & (idx < c), tau, cand)
            return jnp.where(below < need, cand, cut)

        cut_ref[...] = lax.fori_loop(0, idx_bits, idx_step, jnp.zeros((tq, lanes), I32))

    tau_t = jnp.concatenate([tau] * halves, axis=1)
    cut_t = jnp.concatenate([cut_ref[...]] * halves, axis=1)

    def mask_tile(j, carry):
        k = keys_ref[j]
        sel = ((k > tau_t) | ((k == tau_t) & (key_index(j) <= cut_t))) & visible(j)
        keys_ref[j] = lax.bitcast_convert_type(jnp.where(sel, 0.0, NEG_BIG), I32)
        return carry

    lax.fori_loop(0, n_tiles, mask_tile, 0)

    def logits(j, c, r):
        base = pl.multiple_of(j * tq, tq)
        kc = kvb_ref[pl.ds(base, tq), kcol + c * HEAD_DIM_C:kcol + (c + 1) * HEAD_DIM_C]
        s = _dot_nt(qs_ref[c], kc).reshape(G, tq, tq) + lax.bitcast_convert_type(keys_ref[j], F32)[None]
        if r is not None:
            s = s + bt_ref[c, r]
        return s

    def max_tile(j, r):
        for c in range(N_KV_HEADS):
            s = logits(j, c, r)
            m = s[..., 0:lanes]
            for a in range(1, halves):
                m = jnp.maximum(m, s[..., a * lanes:(a + 1) * lanes])
            mx_ref[c] = jnp.maximum(mx_ref[c], m)

    def acc_tile(j, r):
        base = pl.multiple_of(j * tq, tq)
        for c in range(N_KV_HEADS):
            vc = kvb_ref[pl.ds(base, tq), vcol + c * HEAD_DIM_C:vcol + (c + 1) * HEAD_DIM_C]
            vext = jnp.concatenate([vc, jnp.ones((tq, lanes), BF16)], axis=1)
            m = mrep_ref[c]
            p = jnp.exp(logits(j, c, r) - jnp.concatenate([m] * halves, axis=-1))
            acc_ref[c] += _dot(p.reshape(G * tq, tq).astype(BF16), vext)

    def sweep(tile_fn):
        def far(j, carry):
            tile_fn(j, None)
            return carry

        lax.fori_loop(0, jnp.maximum(i - 1, 0), far, 0)

        @pl.when(i >= 1)
        def _():
            tile_fn(i - 1, 1)

        tile_fn(i, 0)

    @pl.when(i == 0)
    def _():
        kmax_ref[...] = jnp.zeros(kmax_ref.shape, F32)

    for c in range(N_KV_HEADS):
        kt = kvb_ref[pl.ds(pl.multiple_of(i * tq, tq), tq), kcol + c * HEAD_DIM_C:kcol + (c + 1) * HEAD_DIM_C].astype(F32)
        knorm = jnp.sqrt(jnp.max(jnp.sum(kt * kt, axis=1, keepdims=True)))
        kmax_ref[c] = jnp.maximum(kmax_ref[c], knorm)
        qf = qs_ref[c].astype(F32)
        qnorm = jnp.sqrt(jnp.sum(qf * qf, axis=1, keepdims=True)).reshape(G, tq, 1)
        for g in range(G):
            h = c * G + g
            far = rb_ref[NUM_BUCKETS - 1, h]
            bias_max = lax.fori_loop(0, NUM_BUCKETS, lambda b, m: jnp.maximum(m, rb_ref[b, h] - far), jnp.float32(0.0))
            mrep_ref[c, g] = qnorm[g] * kmax_ref[c, 0:1, :] + bias_max

    acc_ref[...] = jnp.zeros(acc_ref.shape, F32)
    sweep(acc_tile)

    @pl.when(jnp.logical_not(jnp.min(acc_ref[:, :, lanes:]) > 1e-30))
    def _():
        mx_ref[...] = jnp.full(mx_ref.shape, NEG_BIG, F32)
        sweep(max_tile)
        mrep_ref[...] = jnp.broadcast_to(jnp.max(mx_ref[...], axis=-1, keepdims=True), mrep_ref.shape)
        acc_ref[...] = jnp.zeros(acc_ref.shape, F32)
        sweep(acc_tile)

    for c in range(N_KV_HEADS):
        for g in range(G):
            h = c * G + g
            a = acc_ref[c, g * tq:(g + 1) * tq, :]
            o_ref[:, h * HEAD_DIM_C:(h + 1) * HEAD_DIM_C] = (a[:, :lanes] / a[:, lanes:]).astype(o_ref.dtype)


def _dsa_prompt(qcqi, tail, kvi_bf16, rel_bias, *, tq, n_keep):
    S = qcqi.shape[0]
    dc = qcqi.shape[1] - N_IDX_HEADS * IDX_DIM
    n_heads = dc // HEAD_DIM_C
    G = n_heads // N_KV_HEADS
    wkv = kvi_bf16.shape[1]
    d = np.arange(tq)[:, None] - np.arange(tq)[None, :]
    btab = jnp.asarray(np.stack([_t5_bucket_np(d + r * tq) for r in range(2)]))
    assert _t5_bucket_np(np.array([tq + 1]))[0] == NUM_BUCKETS - 1
    lanes = HEAD_DIM_C
    grid_spec = pltpu.PrefetchScalarGridSpec(
        num_scalar_prefetch=0,
        grid=(S // tq,),
        in_specs=[
            pl.BlockSpec(memory_space=pltpu.SMEM),
            pl.BlockSpec((tq, dc), lambda i: (i, 0)),
            pl.BlockSpec((tq, N_IDX_HEADS * IDX_DIM), lambda i: (i, dc // (N_IDX_HEADS * IDX_DIM))),
            pl.BlockSpec((tq, tail.shape[1]), lambda i: (i, 0)),
            pl.BlockSpec((S, wkv), lambda i: (0, 0), pipeline_mode=pl.Buffered(1)),
            pl.BlockSpec((2, tq, tq), lambda i: (0, 0, 0)),
        ],
        out_specs=pl.BlockSpec((tq, dc), lambda i: (i, 0)),
        scratch_shapes=[
            pltpu.VMEM((S // tq, tq, tq), I32),
            pltpu.VMEM((N_KV_HEADS, 2, G, tq, tq), F32),
            pltpu.VMEM((N_IDX_HEADS, tq, lanes), F32),
            pltpu.VMEM((N_IDX_HEADS, tq, IDX_DIM), BF16),
            pltpu.VMEM((N_KV_HEADS, G * tq, HEAD_DIM_C), BF16),
            pltpu.VMEM((N_KV_HEADS, G, tq, lanes), F32),
            pltpu.VMEM((N_KV_HEADS, G, tq, lanes), F32),
            pltpu.VMEM((N_KV_HEADS, G * tq, 2 * lanes), F32),
            pltpu.VMEM((tq, lanes), I32),
            pltpu.VMEM((N_KV_HEADS, 8, lanes), F32),
        ],
    )
    return pl.pallas_call(
        functools.partial(_dsa_prompt_body, tq=tq, n_keep=n_keep, idx_bits=(S - 1).bit_length()),
        grid_spec=grid_spec,
        out_shape=jax.ShapeDtypeStruct((S, dc), BF16),
        compiler_params=_cparams(("arbitrary",)),
        name="dsa_prompt",
    )(rel_bias, qcqi, qcqi, tail, kvi_bf16, btab)


def _dsa_sel_body(pt_ref, qi_ref, w_ref, kin_ref, *rest, t_new, n_keep):
    pages = rest[:PAGES_PER_STEP]
    mb_ref, keys_ref, arg_ref, cnt_ref = rest[PAGES_PER_STEP:]
    b = pl.program_id(0)
    p = pl.program_id(1)
    B, n_chunks, tp, P = keys_ref.shape
    last = p == pl.num_programs(1) - 1
    q = qi_ref[0]
    w = w_ref[0]
    rowi = lax.broadcasted_iota(I32, (tp, P), 0)
    coli = lax.broadcasted_iota(I32, (tp, P), 1)
    vis_new = (coli <= rowi) & (coli < t_new)

    def chunk_scores(s):
        s = jnp.maximum(s, 0.0) * w
        s = s.reshape(tp, N_IDX_HEADS, s.shape[-1]).sum(axis=1)
        return s * (N_IDX_HEADS ** -0.5 * IDX_DIM ** -0.5)

    for r in range(PAGES_PER_STEP):
        keys_ref[b, p * PAGES_PER_STEP + r] = _sort_key(chunk_scores(_dot(q, pages[r][0].astype(BF16))))

    @pl.when(last)
    def _():
        s_new = jnp.where(vis_new, chunk_scores(_dot_nt(q, kin_ref[0])), -jnp.inf)
        keys_ref[b, n_chunks - 1] = _sort_key(s_new)

    @pl.when(last & (b == B - 1))
    def _():
        key_index = (lax.broadcasted_iota(I32, (n_chunks, tp, P), 0) * P
                     + lax.broadcasted_iota(I32, (n_chunks, tp, P), 2))

        def count(pred, *row_args):
            for n, a in enumerate(row_args):
                arg_ref[n] = a

            def per_seq(s, carry):
                args = [arg_ref[n, s][None] for n in range(len(row_args))]
                cnt_ref[s] = jnp.sum(jnp.where(pred(keys_ref[s], key_index, *args), 1, 0), axis=0)
                return carry

            lax.fori_loop(0, B, per_seq, 0)
            return jnp.broadcast_to(jnp.sum(cnt_ref[...], axis=-1, keepdims=True), (B, tp, P))

        def bit_step(bit, tau):
            cand = tau + lax.shift_left(jnp.int32(1), 31 - bit)
            return jnp.where(count(lambda k, idx, c: k >= c, cand) >= n_keep, cand, tau)

        tau = lax.fori_loop(0, 32, bit_step, jnp.full((B, tp, P), INT_MIN, I32))

        arg_ref[2] = jnp.full((B, tp, P), 2 ** 31 - 1, I32)

        @pl.when(jnp.max(count(lambda k, idx, t: k >= t, tau)) > n_keep)
        def _():
            need = n_keep - count(lambda k, idx, t: k > t, tau)
            idx_bits = (n_chunks * P - 1).bit_length()

            def idx_step(bit, cut):
                cand = cut + lax.shift_left(jnp.int32(1), idx_bits - 1 - bit)
                below = count(lambda k, idx, t, c: (k == t) & (idx < c), tau, cand)
                return jnp.where(below < need, cand, cut)

            arg_ref[2] = lax.fori_loop(0, idx_bits, idx_step, jnp.zeros((B, tp, P), I32))

        arg_ref[0] = tau

        def write_mask(s, carry):
            k = keys_ref[s]
            t = arg_ref[0, s][None]
            sel = (k > t) | ((k == t) & (key_index <= arg_ref[2, s][None]))
            mb_ref[s] = jnp.where(sel, 0.0, NEG_BIG)
            mb_ref[s, n_chunks - 1] = jnp.where(sel[n_chunks - 1] & vis_new, 0.0, NEG_BIG)
            return carry

        lax.fori_loop(0, B, write_mask, 0)


def _dsa_sample_select(pt_flat, qi, w, ki_new, pool_ki, *, n_pages, t_new, n_keep):
    B = qi.shape[0]
    P = pool_ki.shape[2]
    steps = n_pages // PAGES_PER_STEP

    def page_spec(r):
        return pl.BlockSpec((1, IDX_DIM, P), lambda b, p, pt: (pt[b * n_pages + p * PAGES_PER_STEP + r], 0, 0))

    grid_spec = pltpu.PrefetchScalarGridSpec(
        num_scalar_prefetch=1,
        grid=(B, steps),
        in_specs=[
            pl.BlockSpec((1,) + qi.shape[1:], lambda b, p, pt: (b, 0, 0)),
            pl.BlockSpec((1,) + w.shape[1:], lambda b, p, pt: (b, 0, 0)),
            pl.BlockSpec((1,) + ki_new.shape[1:], lambda b, p, pt: (b, 0, 0)),
        ] + [page_spec(r) for r in range(PAGES_PER_STEP)],
        out_specs=pl.BlockSpec((B, n_pages + 1, SAMPLE_PAD_T, P), lambda b, p, pt: (0, 0, 0, 0)),
        scratch_shapes=[
            pltpu.VMEM((B, n_pages + 1, SAMPLE_PAD_T, P), I32),
            pltpu.VMEM((3, B, SAMPLE_PAD_T, P), I32),
            pltpu.VMEM((B, SAMPLE_PAD_T, P), I32),
        ],
    )
    return pl.pallas_call(
        functools.partial(_dsa_sel_body, t_new=t_new, n_keep=n_keep),
        grid_spec=grid_spec,
        out_shape=jax.ShapeDtypeStruct((B, n_pages + 1, SAMPLE_PAD_T, P), F32),
        compiler_params=_cparams(("arbitrary", "arbitrary")),
        name="dsa_sample_select",
    )(pt_flat, qi, w, ki_new, *([pool_ki] * PAGES_PER_STEP))


def _dsa_att_body(pt_ref, rb_ref, q_ref, mb_ref, kn_ref, vn_ref, btab_ref, *rest):
    kp = rest[:PAGES_PER_STEP]
    vp = rest[PAGES_PER_STEP:2 * PAGES_PER_STEP]
    o_ref, bt_ref, m_ref, acc_ref = rest[2 * PAGES_PER_STEP:]
    b = pl.program_id(0)
    p = pl.program_id(1)
    n_chunks = mb_ref.shape[1]
    tp = SAMPLE_PAD_T
    G = q_ref.shape[2] // tp
    rows = G * tp
    P = kn_ref.shape[1]
    lanes = HEAD_DIM_C
    last = p == pl.num_programs(1) - 1

    @pl.when((b == 0) & (p == 0))
    def _():
        for c in range(N_KV_HEADS):
            for kind in range(2):
                for g in range(G):
                    h = c * G + g
                    bt_ref[c, kind, g * tp:(g + 1) * tp, :] = (_bias_from_buckets(btab_ref[kind], rb_ref, h)
                                                              - rb_ref[NUM_BUCKETS - 1, h])

    @pl.when(p == 0)
    def _():
        m_ref[...] = jnp.full(m_ref.shape, NEG_BIG, F32)
        acc_ref[...] = jnp.zeros(acc_ref.shape, F32)

    def attend(c, s, vext):
        m_old = m_ref[c]
        m_new = jnp.maximum(m_old, jnp.max(s, axis=1, keepdims=True))
        alpha = jnp.exp(m_old - m_new)
        pr = jnp.exp(s - jnp.concatenate([m_new] * (s.shape[1] // lanes), axis=1))
        acc_ref[c] = jnp.concatenate([alpha, alpha], axis=1) * acc_ref[c] + _dot(pr.astype(BF16), vext)
        m_ref[c] = m_new

    ones = jnp.ones((P, lanes), BF16)
    mb_step = jnp.concatenate([mb_ref[0, p * PAGES_PER_STEP + r] for r in range(PAGES_PER_STEP)], axis=1)
    mb_step = jnp.concatenate([mb_step] * G, axis=0)
    for c in range(N_KV_HEADS):
        head_rows = pl.ds(c, P, stride=N_KV_HEADS)
        k_all = jnp.concatenate([kp[r][0, head_rows, :].astype(BF16) for r in range(PAGES_PER_STEP)], axis=0)
        v_all = jnp.concatenate(
            [jnp.concatenate([vp[r][0, head_rows, :].astype(BF16), ones], axis=1) for r in range(PAGES_PER_STEP)], axis=0)
        s = _dot_nt(q_ref[0, c], k_all) + mb_step
        s = jnp.concatenate([s[:, :-P], s[:, -P:] + jnp.where(last, bt_ref[c, 0], 0.0)], axis=1)
        attend(c, s, v_all)

    @pl.when(last)
    def _():
        mb_new = jnp.concatenate([mb_ref[0, n_chunks - 1]] * G, axis=0)
        for c in range(N_KV_HEADS):
            cols = slice(c * HEAD_DIM_C, (c + 1) * HEAD_DIM_C)
            s = _dot_nt(q_ref[0, c], kn_ref[0, :, cols]) + mb_new + bt_ref[c, 1]
            attend(c, s, jnp.concatenate([vn_ref[0, :, cols], ones], axis=1))
            a = acc_ref[c]
            o_ref[0, c] = a[:, :lanes] / a[:, lanes:]


def _dsa_sample_attend(pt_flat, rel_bias, q, mb, k_new, v_new, pool_k, pool_v, *, n_pages, past):
    B, kvh, rows, dh = q.shape
    P = pool_k.shape[1] // kvh
    steps = n_pages // PAGES_PER_STEP
    tp = SAMPLE_PAD_T
    t = np.arange(tp)[:, None]
    col = np.arange(P)[None, :]
    assert past - (n_pages - 1) * P >= MAX_DISTANCE
    last_page = _t5_bucket_np(past + t - ((n_pages - 1) * P + col))
    new = _t5_bucket_np(t - col)
    btab = jnp.asarray(np.stack([last_page, new]))

    def page_spec(r):
        return pl.BlockSpec((1, P * kvh, dh), lambda b, p, pt: (pt[b * n_pages + p * PAGES_PER_STEP + r], 0, 0))

    grid_spec = pltpu.PrefetchScalarGridSpec(
        num_scalar_prefetch=1,
        grid=(B, steps),
        in_specs=[
            pl.BlockSpec(memory_space=pltpu.SMEM),
            pl.BlockSpec((1, kvh, rows, dh), lambda b, p, pt: (b, 0, 0, 0)),
            pl.BlockSpec((1,) + mb.shape[1:], lambda b, p, pt: (b, 0, 0, 0)),
            pl.BlockSpec((1,) + k_new.shape[1:], lambda b, p, pt: (b, 0, 0)),
            pl.BlockSpec((1,) + v_new.shape[1:], lambda b, p, pt: (b, 0, 0)),
            pl.BlockSpec((2, tp, P), lambda b, p, pt: (0, 0, 0)),
        ] + [page_spec(r) for r in range(PAGES_PER_STEP)] * 2,
        out_specs=pl.BlockSpec((1, kvh, rows, dh), lambda b, p, pt: (b, 0, 0, 0)),
        scratch_shapes=[
            pltpu.VMEM((kvh, 2, rows, P), F32),
            pltpu.VMEM((kvh, rows, dh), F32),
            pltpu.VMEM((kvh, rows, 2 * dh), F32),
        ],
    )
    return pl.pallas_call(
        _dsa_att_body,
        grid_spec=grid_spec,
        out_shape=jax.ShapeDtypeStruct((B, kvh, rows, dh), F32),
        compiler_params=_cparams(("arbitrary", "arbitrary")),
        name="dsa_sample_attend",
    )(pt_flat, rel_bias, q, mb, k_new, v_new, btab, *([pool_k] * PAGES_PER_STEP), *([pool_v] * PAGES_PER_STEP))


def _dsa_sample(c1, tail, c2b, pt_flat, rel_bias, pool_k, pool_v, pool_ki, *, Bs, Ts, n_pages):
    tp = SAMPLE_PAD_T
    kv_w = N_KV_HEADS * HEAD_DIM_C
    d_c = c1.shape[1] - N_IDX_HEADS * IDX_DIM
    G = d_c // kv_w
    P = pool_ki.shape[2]
    past = n_pages * P
    qi_s = c1[:, d_c:].reshape(Bs, tp * N_IDX_HEADS, IDX_DIM)
    w_s = tail[:, IDX_DIM:IDX_DIM + N_IDX_HEADS].reshape(Bs, tp * N_IDX_HEADS, 1)
    new_rows = jnp.pad(c2b.reshape(Bs, tp, -1), ((0, 0), (0, P - tp), (0, 0)))
    mb = _dsa_sample_select(pt_flat, qi_s, w_s, new_rows[:, :, 2 * kv_w:2 * kv_w + IDX_DIM], pool_ki,
                            n_pages=n_pages, t_new=Ts, n_keep=min(TOPK_MAX, (past + Ts) // 4))
    q_s = c1[:, :d_c].reshape(Bs, tp, N_KV_HEADS, G, HEAD_DIM_C).transpose(0, 2, 3, 1, 4)
    q_s = q_s.reshape(Bs, N_KV_HEADS, G * tp, HEAD_DIM_C)
    o_s = _dsa_sample_attend(pt_flat, rel_bias, q_s, mb, new_rows[:, :, :kv_w], new_rows[:, :, kv_w:2 * kv_w],
                             pool_k, pool_v, n_pages=n_pages, past=past)
    return o_s.reshape(Bs, N_KV_HEADS, G, tp, HEAD_DIM_C).transpose(0, 3, 1, 2, 4).reshape(Bs * tp, d_c).astype(BF16)


def _split_w_in(w, d_a, d_b, d_c, rk, kv_w, qi_w):
    o = np.cumsum([0, d_a, d_a, rk, rk, d_b, d_b, d_c, kv_w, kv_w, qi_w, IDX_DIM, N_IDX_HEADS])
    wb = w.astype(BF16)
    grp_a = wb[:, o[0]:o[2]]
    grp_b = wb[:, o[2]:o[6]]
    grp_c1 = jnp.concatenate([wb[:, o[6]:o[7]], wb[:, o[9]:o[10]]], axis=1)
    tail = wb[:, o[10]:o[12]]
    pad = 128 - tail.shape[1]
    grp_c2 = jnp.concatenate([wb[:, o[7]:o[9]], tail, jnp.zeros((w.shape[0], pad), BF16)], axis=1)
    return grp_a, grp_b, grp_c1, grp_c2


def kernel(x_prompt, x_sample, cache_k, cache_v, cache_kidx, page_table, state_lru_h, state_conv, state_ret,
           norm_ffn1, ffn1_gate, ffn1_up, ffn1_down, norm_mix, w_in, w_out, conv_w, conv_b,
           lru_wa, lru_ba, lru_wx, lru_bx, lru_lambda, rel_bias,
           norm_ffn2, ffn2_gate, ffn2_up, ffn2_down, norm_final):
    depth = norm_ffn1.shape[0]
    _, S, D = x_prompt.shape
    Bs, Ts, _ = x_sample.shape
    n_pool, P = cache_k.shape[1], cache_k.shape[2]
    n_pages = page_table.shape[1]
    past = n_pages * P
    d_a = state_lru_h.shape[-1]
    H_r, rdk, rdv = state_ret.shape[2:]
    d_b = H_r * rdv
    d_c = D - d_a - d_b
    kv_w = N_KV_HEADS * HEAD_DIM_C
    qi_w = N_IDX_HEADS * IDX_DIM
    G = d_c // HEAD_DIM_C // N_KV_HEADS
    tp = SAMPLE_PAD_T
    Rs = Bs * tp

    xp = x_prompt.reshape(S, D)
    xs = jnp.pad(x_sample, ((0, 0), (0, tp - Ts), (0, 0))).reshape(Rs, D)

    cos_p, sin_p = _rope_tables(jnp.arange(S, dtype=I32), rdk, H_r)
    cos_s, sin_s = _rope_tables(past + jnp.arange(tp, dtype=I32), rdk, H_r)
    colscale = jnp.concatenate([jnp.full((d_c,), HEAD_DIM_C ** -0.5, F32), jnp.ones((qi_w,), F32)])

    pool_k = cache_k.reshape(depth * n_pool, P * N_KV_HEADS, HEAD_DIM_C)
    pool_v = cache_v.reshape(depth * n_pool, P * N_KV_HEADS, HEAD_DIM_C)
    pool_ki = jnp.swapaxes(cache_kidx, 2, 3).reshape(depth * n_pool, IDX_DIM, P)

    zeros_conv = jnp.zeros((1, 8, d_a), F32)
    zeros_h = jnp.zeros((1, 1, d_a), F32)
    zeros_s = jnp.zeros((1, H_r, rdk, rdv), F32)

    outs_p, outs_s = [], []
    y_prompt = y_sample = None
    for l in range(depth):
        bf = lambda a: a[l].astype(BF16)
        f1 = (bf(ffn1_gate), bf(ffn1_up), bf(ffn1_down))
        f2 = (bf(ffn2_gate), bf(ffn2_up), bf(ffn2_down))
        w_a, w_b, w_c1, w_c2 = _split_w_in(w_in[l], d_a, d_b, d_c, H_r * rdk, kv_w, qi_w)
        wo = w_out[l].astype(BF16)
        wo_a, wo_b, wo_c = wo[:d_a], wo[d_a:d_a + d_b], wo[d_a + d_b:]
        lwa, lwx = lru_wa[l].astype(BF16), lru_wx[l].astype(BF16)
        last = l == depth - 1
        pt_flat = (page_table + l * n_pool).reshape(-1).astype(I32)

        xp = _ffn(xp, norm_ffn1[l], *f1, tm=512, tf=512)
        proj = functools.partial(_norm_matmul, xp, norm_mix[l])
        (ua,) = proj(w_a, tm=512, tn=1024, out_f32=True, out_bf16=False)
        (ub,) = proj(w_b, tm=512, tn=768, out_f32=True, out_bf16=False)
        (c1,) = proj(w_c1, colscale, tm=512, tn=1024, out_f32=False, out_bf16=True)
        k_new, v_new, ki_new, tail, c2b = _kv_proj(xp, norm_mix[l], w_c2, tm=512)
        ya, h_last, conv_new = _lru(ua.reshape(1, S, 2 * d_a), zeros_conv, zeros_h, conv_w[l], conv_b[l],
                                    lwa, lru_ba[l], lwx, lru_bx[l], lru_lambda[l], tc=1024, t_valid_last=1024)
        yb, s_last = _retention(ub.reshape(1, S, -1), cos_p, sin_p, zeros_s, cp=128, c_valid=128, mm_dtype=BF16)
        yc = _dsa_prompt(c1, tail, c2b, rel_bias, tq=256, n_keep=min(TOPK_MAX, S // 4))
        xp = _out_proj(xp, ya.reshape(S, d_a), yb.reshape(S, d_b), yc, wo_a, wo_b, wo_c, tm=512, tn=1024)
        if last:
            xp, y_prompt = _ffn(xp, norm_ffn2[l], *f2, norm_final, tm=512, tf=512)
        else:
            xp = _ffn(xp, norm_ffn2[l], *f2, tm=512, tf=512)
        outs_p.append((k_new.reshape(1, S, N_KV_HEADS, HEAD_DIM_C),
                       v_new.reshape(1, S, N_KV_HEADS, HEAD_DIM_C),
                       ki_new.reshape(1, S, IDX_DIM),
                       h_last.reshape(1, d_a), conv_new, s_last))

        xs = _ffn(xs, norm_ffn1[l], *f1, tm=Rs, tf=512)
        proj = functools.partial(_norm_matmul, xs, norm_mix[l])
        (ua,) = proj(w_a, tm=Rs, tn=1024, out_f32=True, out_bf16=False)
        (ub,) = proj(w_b, tm=Rs, tn=768, out_f32=True, out_bf16=False)
        (c1,) = proj(w_c1, colscale, tm=Rs, tn=1024, out_f32=False, out_bf16=True)
        k_new, v_new, ki_new, tail, c2b = _kv_proj(xs, norm_mix[l], w_c2, tm=Rs)
        conv0 = jnp.pad(state_conv[l], ((0, 0), (8 - (CONV_WIDTH - 1), 0), (0, 0)))
        ya, h_last, conv_new = _lru(ua.reshape(Bs, tp, 2 * d_a), conv0, state_lru_h[l].reshape(Bs, 1, d_a),
                                    conv_w[l], conv_b[l], lwa, lru_ba[l], lwx, lru_bx[l], lru_lambda[l],
                                    tc=tp, t_valid_last=Ts)
        yb, s_last = _retention(ub.reshape(Bs, tp, -1), cos_s, sin_s, state_ret[l], cp=tp, c_valid=Ts, mm_dtype=F32)
        yc = _dsa_sample(c1, tail, c2b, pt_flat, rel_bias, pool_k, pool_v, pool_ki, Bs=Bs, Ts=Ts, n_pages=n_pages)
        xs = _out_proj(xs, ya.reshape(Rs, d_a), yb.reshape(Rs, d_b), yc, wo_a, wo_b, wo_c, tm=Rs, tn=1024)
        if last:
            xs, y_sample = _ffn(xs, norm_ffn2[l], *f2, norm_final, tm=Rs, tf=512)
        else:
            xs = _ffn(xs, norm_ffn2[l], *f2, tm=Rs, tf=512)
        valid = lambda a: a.reshape(Bs, tp, -1)[:, :Ts]
        outs_s.append((valid(k_new).reshape(Bs, Ts, N_KV_HEADS, HEAD_DIM_C),
                       valid(v_new).reshape(Bs, Ts, N_KV_HEADS, HEAD_DIM_C),
                       valid(ki_new),
                       h_last.reshape(Bs, d_a), conv_new, s_last))

    stack = lambda outs, k: jnp.stack([o[k] for o in outs])
    return ((y_prompt.reshape(1, S, D), y_sample.reshape(Bs, tp, D)[:, :Ts])
            + tuple(stack(outs_p, k) for k in range(6))
            + tuple(stack(outs_s, k) for k in range(6)))
```

```python
import functools
import math

import numpy as np
import jax
import jax.numpy as jnp
from jax import lax
from jax.experimental import pallas as pl
from jax.experimental.pallas import tpu as pltpu

F32 = jnp.float32
BF16 = jnp.bfloat16
I32 = jnp.int32

EPS = 1e-6
N_A_BLOCKS = 4
CONV_WIDTH = 4
LRU_C = 8.0
N_RET_HEADS = 4
ROPE_BASE = 10000.0
HEAD_DIM_C = 128
N_KV_HEADS = 2
N_IDX_HEADS = 16
IDX_DIM = 64
TOPK_MAX = 256
NUM_BUCKETS = 32
MAX_DISTANCE = 128
SAMPLE_PAD_T = 8
PAGES_PER_STEP = 16
NEG_BIG = -1e30
INT_MIN = -2 ** 31
SELECT_BIT_GROUPS = (0, 22, 25, 28, 32)

VMEM_LIMIT = 56 * 1024 * 1024


def _cparams(sem):
    return pltpu.CompilerParams(dimension_semantics=sem, vmem_limit_bytes=VMEM_LIMIT)


def _rms(x, g):
    return x * lax.rsqrt(jnp.mean(x * x, axis=-1, keepdims=True) + EPS) * g


def _dot(a, b):
    return jnp.dot(a, b, preferred_element_type=F32)


def _dot_nt(a, b):
    return lax.dot_general(a, b, (((1,), (1,)), ((), ())), preferred_element_type=F32)


def _dot_tn(a, b):
    return lax.dot_general(a, b, (((0,), (0,)), ((), ())), preferred_element_type=F32)


def _sort_key(score):
    bits = lax.bitcast_convert_type(score, I32)
    return jnp.where(bits < 0, bits ^ jnp.int32(0x7FFFFFFF), bits)


def _ffn_body(x_ref, g_ref, wg_ref, wu_ref, wd_ref, *rest, final_norm):
    if final_norm:
        gf_ref, o_ref, on_ref, h_ref = rest
    else:
        o_ref, h_ref = rest
    j = pl.program_id(1)

    @pl.when(j == 0)
    def _():
        h_ref[...] = _rms(x_ref[...], g_ref[...]).astype(BF16)
        o_ref[...] = jnp.zeros_like(o_ref)

    h = h_ref[...]
    g = _dot(h, wg_ref[...])
    u = _dot(h, wu_ref[...])
    a = (g * jax.nn.sigmoid(g) * u).astype(BF16)
    o_ref[...] += _dot(a, wd_ref[...])

    @pl.when(j == pl.num_programs(1) - 1)
    def _():
        y = x_ref[...] + 0.5 * o_ref[...]
        o_ref[...] = y
        if final_norm:
            on_ref[...] = _rms(y, gf_ref[...])


def _ffn(x, g, wg, wu, wd, gf=None, *, tm, tf):
    R, D = x.shape
    FF = wg.shape[1]
    final_norm = gf is not None
    in_specs = [
        pl.BlockSpec((tm, D), lambda i, j: (i, 0)),
        pl.BlockSpec((1, D), lambda i, j: (0, 0)),
        pl.BlockSpec((D, tf), lambda i, j: (0, j)),
        pl.BlockSpec((D, tf), lambda i, j: (0, j)),
        pl.BlockSpec((tf, D), lambda i, j: (j, 0)),
    ]
    args = [x, g.reshape(1, D), wg, wu, wd]
    out_shape = [jax.ShapeDtypeStruct((R, D), F32)]
    out_specs = [pl.BlockSpec((tm, D), lambda i, j: (i, 0))]
    if final_norm:
        in_specs.append(pl.BlockSpec((1, D), lambda i, j: (0, 0)))
        args.append(gf.reshape(1, D))
        out_shape.append(jax.ShapeDtypeStruct((R, D), F32))
        out_specs.append(pl.BlockSpec((tm, D), lambda i, j: (i, 0)))
    res = pl.pallas_call(
        functools.partial(_ffn_body, final_norm=final_norm),
        grid=(R // tm, FF // tf),
        in_specs=in_specs,
        out_specs=out_specs,
        out_shape=out_shape,
        scratch_shapes=[pltpu.VMEM((tm, D), BF16)],
        compiler_params=_cparams(("parallel", "arbitrary")),
        name="ffn",
    )(*args)
    return res if final_norm else res[0]


def _ffn_cast_body(x_ref, g_ref, wg_ref, wu_ref, wd_ref, *rest, final_norm):
    if final_norm:
        gf_ref, o_ref, on_ref, wgb_ref, wub_ref, wdb_ref, h_ref = rest
    else:
        o_ref, wgb_ref, wub_ref, wdb_ref, h_ref = rest
    j = pl.program_id(0)

    @pl.when(j == 0)
    def _():
        h_ref[...] = _rms(x_ref[...], g_ref[...]).astype(BF16)
        o_ref[...] = jnp.zeros_like(o_ref)

    wg, wu, wd = wg_ref[0].astype(BF16), wu_ref[0].astype(BF16), wd_ref[0].astype(BF16)
    wgb_ref[...] = wg
    wub_ref[...] = wu
    wdb_ref[...] = wd
    h = h_ref[...]
    g = _dot(h, wg)
    a = (g * jax.nn.sigmoid(g) * _dot(h, wu)).astype(BF16)
    o_ref[...] += _dot(a, wd)

    @pl.when(j == pl.num_programs(0) - 1)
    def _():
        y = x_ref[...] + 0.5 * o_ref[...]
        o_ref[...] = y
        if final_norm:
            on_ref[...] = _rms(y, gf_ref[...])


def _ffn_cast(x, g, wg, wu, wd, layer, gf=None, *, tf):
    R, D = x.shape
    FF = wg.shape[2]
    final_norm = gf is not None
    in_specs = [
        pl.BlockSpec((R, D), lambda j: (0, 0)),
        pl.BlockSpec((1, D), lambda j: (0, 0)),
        pl.BlockSpec((1, D, tf), lambda j: (layer, 0, j)),
        pl.BlockSpec((1, D, tf), lambda j: (layer, 0, j)),
        pl.BlockSpec((1, tf, D), lambda j: (layer, j, 0)),
    ]
    args = [x, g.reshape(1, D), wg, wu, wd]
    out_shape = [jax.ShapeDtypeStruct((R, D), F32)]
    out_specs = [pl.BlockSpec((R, D), lambda j: (0, 0))]
    if final_norm:
        in_specs.append(pl.BlockSpec((1, D), lambda j: (0, 0)))
        args.append(gf.reshape(1, D))
        out_shape.append(jax.ShapeDtypeStruct((R, D), F32))
        out_specs.append(pl.BlockSpec((R, D), lambda j: (0, 0)))
    out_shape += [jax.ShapeDtypeStruct((D, FF), BF16), jax.ShapeDtypeStruct((D, FF), BF16),
                  jax.ShapeDtypeStruct((FF, D), BF16)]
    out_specs += [pl.BlockSpec((D, tf), lambda j: (0, j)), pl.BlockSpec((D, tf), lambda j: (0, j)),
                  pl.BlockSpec((tf, D), lambda j: (j, 0))]
    return pl.pallas_call(
        functools.partial(_ffn_cast_body, final_norm=final_norm),
        grid=(FF // tf,),
        in_specs=in_specs,
        out_specs=out_specs,
        out_shape=out_shape,
        scratch_shapes=[pltpu.VMEM((R, D), BF16)],
        compiler_params=_cparams(("arbitrary",)),
        name="ffn_cast",
    )(*args)


def _in_proj_body(x_ref, g_ref, w_ref, ua_ref, ub_ref, c1_ref, k_ref, v_ref, ki_ref, tail_ref, c2b_ref, *, q_scale):
    h = _rms(x_ref[...], g_ref[...]).astype(BF16)
    wa, wb, wc1 = ua_ref.shape[1], ub_ref.shape[1], c1_ref.shape[1]
    kv_w = k_ref.shape[1]
    d_c = wc1 - N_IDX_HEADS * IDX_DIM
    o = 0
    ua_ref[...] = _dot(h, w_ref[:, o:o + wa])
    o += wa
    ub_ref[...] = _dot(h, w_ref[:, o:o + wb])
    o += wb
    c1_ref[:, :d_c] = (_dot(h, w_ref[:, o:o + d_c]) * q_scale).astype(BF16)
    c1_ref[:, d_c:] = _dot(h, w_ref[:, o + d_c:o + wc1]).astype(BF16)
    o += wc1
    r = _dot(h, w_ref[:, o:])
    k_ref[...] = r[:, :kv_w]
    v_ref[...] = r[:, kv_w:2 * kv_w]
    ki_ref[...] = r[:, 2 * kv_w:2 * kv_w + IDX_DIM]
    tail_ref[...] = r[:, 2 * kv_w:]
    c2b_ref[...] = r.astype(BF16)


def _in_proj(x, g, w, *, tm, widths, q_scale):
    R, D = x.shape
    wa, wb, wc1, wc2 = widths
    kv_w = N_KV_HEADS * HEAD_DIM_C
    outs = [(wa, F32), (wb, F32), (wc1, BF16), (kv_w, F32), (kv_w, F32), (IDX_DIM, F32), (wc2 - 2 * kv_w, F32), (wc2, BF16)]
    return pl.pallas_call(
        functools.partial(_in_proj_body, q_scale=q_scale),
        grid=(R // tm,),
        in_specs=[pl.BlockSpec((tm, D), lambda i: (i, 0)), pl.BlockSpec((1, D), lambda i: (0, 0)),
                  pl.BlockSpec((D, w.shape[1]), lambda i: (0, 0), pipeline_mode=pl.Buffered(1))],
        out_specs=[pl.BlockSpec((tm, wd), lambda i: (i, 0)) for wd, _ in outs],
        out_shape=[jax.ShapeDtypeStruct((R, wd), dt) for wd, dt in outs],
        compiler_params=_cparams(("parallel",)),
        name="in_proj",
    )(x, g.reshape(1, D), w)


def _oproj_body(x_ref, ya_ref, yb_ref, yc_ref, w_ref, o_ref):
    da, db = ya_ref.shape[1], yb_ref.shape[1]
    o_ref[...] = (x_ref[...] + _dot(ya_ref[...], w_ref[:da]) + _dot(yb_ref[...], w_ref[da:da + db])
                  + _dot(yc_ref[...], w_ref[da + db:]))


def _out_proj(x, ya, yb, yc, w, *, tm):
    R, D = x.shape
    row = lambda a: pl.BlockSpec((tm, a.shape[1]), lambda i: (i, 0))
    return pl.pallas_call(
        _oproj_body,
        grid=(R // tm,),
        in_specs=[row(x), row(ya), row(yb), row(yc),
                  pl.BlockSpec(w.shape, lambda i: (0, 0), pipeline_mode=pl.Buffered(1))],
        out_specs=row(x),
        out_shape=jax.ShapeDtypeStruct((R, D), F32),
        compiler_params=_cparams(("parallel",)),
        name="out_proj",
    )(x, ya, yb, yc, w)


def _lru_body(xa_ref, ga_ref, conv0_ref, h0_ref, cw_ref, cb_ref, wa_ref, ba_ref, wx_ref, bx_ref, lam_ref,
              ya_ref, hl_ref, cn_ref, xs_ref, a_ref, u_ref, hs_ref, hc_ref, *, tc, t_valid_last):
    t = pl.program_id(1)
    da = xa_ref.shape[-1]
    blk = da // N_A_BLOCKS
    tail = 8

    @pl.when(t == 0)
    def _():
        xs_ref[0:tail, :] = conv0_ref[0]
        hc_ref[0:1, :] = h0_ref[0]

    xs_ref[tail:tail + tc, :] = xa_ref[0]
    xc = cb_ref[...]
    for j in range(CONV_WIDTH):
        off = tail - (CONV_WIDTH - 1) + j
        xc = xc + xs_ref[off:off + tc, :] * cw_ref[j:j + 1, :]
    xcb = xc.astype(BF16)
    pre_r = jnp.concatenate([_dot(xcb[:, n * blk:(n + 1) * blk], wa_ref[n]) for n in range(N_A_BLOCKS)], axis=1)
    pre_i = jnp.concatenate([_dot(xcb[:, n * blk:(n + 1) * blk], wx_ref[n]) for n in range(N_A_BLOCKS)], axis=1)
    r = jax.nn.sigmoid(pre_r + ba_ref[...])
    gi = jax.nn.sigmoid(pre_i + bx_ref[...])
    z = -lam_ref[...]
    softplus = jnp.maximum(z, 0.0) + jnp.log1p(jnp.exp(-jnp.abs(z)))
    log_a = (-LRU_C) * r * softplus
    a = jnp.exp(log_a)
    a_ref[...] = a
    u_ref[...] = jnp.sqrt(-jnp.tanh(log_a) * (a * a + 1.0)) * gi * xc

    def step(k, h):
        base = pl.multiple_of(k * 8, 8)
        for rr in range(8):
            h = a_ref[pl.ds(base + rr, 1), :] * h + u_ref[pl.ds(base + rr, 1), :]
            hs_ref[pl.ds(base + rr, 1), :] = h
        return h

    h = lax.fori_loop(0, tc // 8, step, hc_ref[0:1, :])
    hc_ref[0:1, :] = h
    ya_ref[0] = (hs_ref[...] * jax.nn.gelu(ga_ref[0])).astype(ya_ref.dtype)

    @pl.when(t == pl.num_programs(1) - 1)
    def _():
        hl_ref[0] = hs_ref[t_valid_last - 1:t_valid_last, :]
        lo = tail - (CONV_WIDTH - 1) + t_valid_last
        cn_ref[0] = xs_ref[lo:lo + CONV_WIDTH - 1, :]

    xs_ref[0:tail, :] = xs_ref[tc:tc + tail, :]


def _lru(ua, conv0, h0, cw, cb, wa, ba, wx, bx, lam, *, tc, t_valid_last):
    B, T, da2 = ua.shape
    da = da2 // 2
    vec = lambda: pl.BlockSpec((1, da), lambda b, t: (0, 0))
    blk = da // N_A_BLOCKS
    return pl.pallas_call(
        functools.partial(_lru_body, tc=tc, t_valid_last=t_valid_last),
        grid=(B, T // tc),
        in_specs=[
            pl.BlockSpec((1, tc, da), lambda b, t: (b, t, 0)),
            pl.BlockSpec((1, tc, da), lambda b, t: (b, t, 1)),
            pl.BlockSpec((1, 8, da), lambda b, t: (b, 0, 0)),
            pl.BlockSpec((1, 1, da), lambda b, t: (b, 0, 0)),
            pl.BlockSpec((CONV_WIDTH, da), lambda b, t: (0, 0)),
            vec(),
            pl.BlockSpec((N_A_BLOCKS, blk, blk), lambda b, t: (0, 0, 0)),
            vec(),
            pl.BlockSpec((N_A_BLOCKS, blk, blk), lambda b, t: (0, 0, 0)),
            vec(),
            vec(),
        ],
        out_specs=[
            pl.BlockSpec((1, tc, da), lambda b, t: (b, t, 0)),
            pl.BlockSpec((1, 1, da), lambda b, t: (b, 0, 0)),
            pl.BlockSpec((1, CONV_WIDTH - 1, da), lambda b, t: (b, 0, 0)),
        ],
        out_shape=[
            jax.ShapeDtypeStruct((B, T, da), BF16),
            jax.ShapeDtypeStruct((B, 1, da), F32),
            jax.ShapeDtypeStruct((B, CONV_WIDTH - 1, da), F32),
        ],
        scratch_shapes=[
            pltpu.VMEM((tc + 8, da), F32),
            pltpu.VMEM((tc, da), F32),
            pltpu.VMEM((tc, da), F32),
            pltpu.VMEM((tc, da), F32),
            pltpu.VMEM((8, da), F32),
        ],
        compiler_params=_cparams(("arbitrary", "arbitrary")),
        name="rglru",
    )(ua, ua, conv0, h0, cw, cb.reshape(1, da), wa, ba.reshape(1, da), wx, bx.reshape(1, da), lam.reshape(1, da))


def _ret_body(q_ref, k_ref, v_ref, gb_ref, cos_ref, sin_ref, s0_ref, yb_ref, sl_ref, s_ref, *, cp, c_valid, mm_dtype):
    t = pl.program_id(1)
    dk = q_ref.shape[-1] // N_RET_HEADS
    dv = v_ref.shape[-1] // N_RET_HEADS

    @pl.when(t == 0)
    def _():
        s_ref[...] = s0_ref[0]

    cos = cos_ref[...]
    sin = sin_ref[...]
    first_half = (lax.broadcasted_iota(I32, cos.shape, 1) % dk) < (dk // 2)
    width = cos.shape[1]

    def rot(x):
        partner = jnp.where(first_half, pltpu.roll(x, width - dk // 2, 1), pltpu.roll(x, dk // 2, 1))
        return x * cos + partner * sin

    row = lax.broadcasted_iota(I32, (cp, 1), 0)
    q = rot(q_ref[0])
    k = rot(k_ref[0]) * (dk ** -0.5)
    if c_valid < cp:
        k = jnp.where(row < c_valid, k, 0.0)
    v = v_ref[0]
    gb = gb_ref[0]
    ri = lax.broadcasted_iota(I32, (cp, cp), 0)
    ci = lax.broadcasted_iota(I32, (cp, cp), 1)
    diff = (ri - ci).astype(F32)
    rowf = row.astype(F32)
    for h in range(N_RET_HEADS):
        lg = math.log1p(-(2.0 ** (-5.0 - h)))
        dmask = jnp.where(diff >= 0, jnp.exp(jnp.maximum(diff, 0.0) * lg), 0.0)
        cross_dec = jnp.exp((rowf + 1.0) * lg)
        state_dec = jnp.exp((c_valid - 1.0 - rowf) * lg)
        chunk_dec = math.exp(c_valid * lg)
        qh = q[:, h * dk:(h + 1) * dk].astype(mm_dtype)
        kh = k[:, h * dk:(h + 1) * dk]
        vh = v[:, h * dv:(h + 1) * dv].astype(mm_dtype)
        s_old = s_ref[h]
        att = _dot_nt(qh, kh.astype(mm_dtype)) * dmask
        inner = _dot(att.astype(mm_dtype), vh)
        cross = _dot(qh, s_old.astype(mm_dtype)) * cross_dec
        s_ref[h] = s_old * chunk_dec + _dot_tn((kh * state_dec).astype(mm_dtype), vh)
        o = inner + cross
        o = o * lax.rsqrt(jnp.mean(o * o, axis=-1, keepdims=True) + EPS)
        gh = gb[:, h * dv:(h + 1) * dv]
        yb_ref[0, :, h * dv:(h + 1) * dv] = (o * (gh * jax.nn.sigmoid(gh))).astype(yb_ref.dtype)

    @pl.when(t == pl.num_programs(1) - 1)
    def _():
        sl_ref[0] = s_ref[...]


def _retention(ub, cos, sin, s0, *, cp, c_valid, mm_dtype):
    B, T, wtot = ub.shape
    w = wtot // 6
    H, dk, dv = s0.shape[1:]
    return pl.pallas_call(
        functools.partial(_ret_body, cp=cp, c_valid=c_valid, mm_dtype=mm_dtype),
        grid=(B, T // cp),
        in_specs=[
            pl.BlockSpec((1, cp, w), lambda b, t: (b, t, 0)),
            pl.BlockSpec((1, cp, w), lambda b, t: (b, t, 1)),
            pl.BlockSpec((1, cp, 2 * w), lambda b, t: (b, t, 1)),
            pl.BlockSpec((1, cp, 2 * w), lambda b, t: (b, t, 2)),
            pl.BlockSpec((cp, w), lambda b, t: (t, 0)),
            pl.BlockSpec((cp, w), lambda b, t: (t, 0)),
            pl.BlockSpec((1, H, dk, dv), lambda b, t: (b, 0, 0, 0)),
        ],
        out_specs=[
            pl.BlockSpec((1, cp, 2 * w), lambda b, t: (b, t, 0)),
            pl.BlockSpec((1, H, dk, dv), lambda b, t: (b, 0, 0, 0)),
        ],
        out_shape=[
            jax.ShapeDtypeStruct((B, T, 2 * w), BF16),
            jax.ShapeDtypeStruct((B, H, dk, dv), F32),
        ],
        scratch_shapes=[pltpu.VMEM((H, dk, dv), F32)],
        compiler_params=_cparams(("arbitrary", "arbitrary")),
        name="retention",
    )(ub, ub, ub, ub, cos, sin, s0)


def _rope_tables(pos, dk, heads):
    half = dk // 2
    freqs = ROPE_BASE ** (-jnp.arange(half, dtype=F32) / half)
    ang = pos.astype(F32)[:, None] * freqs[None, :]
    cos, sin = jnp.cos(ang), jnp.sin(ang)
    cos_t = jnp.tile(jnp.concatenate([cos, cos], axis=1), (1, heads))
    sin_t = jnp.tile(jnp.concatenate([-sin, sin], axis=1), (1, heads))
    return cos_t, sin_t


def _t5_bucket_np(dist):
    n = np.maximum(dist, 0)
    max_exact = NUM_BUCKETS // 2
    ratio = np.log(np.maximum(n, 1).astype(np.float32) / np.float32(max_exact)) / np.float32(math.log(MAX_DISTANCE / max_exact))
    large = max_exact + (ratio * np.float32(NUM_BUCKETS - max_exact)).astype(np.int32)
    large = np.minimum(large, NUM_BUCKETS - 1)
    return np.where(n < max_exact, n, large).astype(np.int32)


def _bias_from_buckets(bucket, rb_ref, head):
    def step(b, out):
        return jnp.where(bucket == b, rb_ref[b, head], out)

    return lax.fori_loop(0, NUM_BUCKETS, step, jnp.zeros(bucket.shape, F32))


def _dsa_prompt_body(rb_ref, qc_ref, qi_ref, wi_ref, kvb_ref, btab_ref, o_ref,
                     keys_ref, bt_ref, wf_ref, qis_ref, qs_ref, mx_ref, mrep_ref, acc_ref, cut_ref, kmax_ref, sel_ref,
                     *, tq, n_keep, idx_bits):
    i = pl.program_id(0)
    G = qc_ref.shape[1] // (N_KV_HEADS * HEAD_DIM_C)
    n_heads = N_KV_HEADS * G
    kcol, vcol, icol = 0, N_KV_HEADS * HEAD_DIM_C, 2 * N_KV_HEADS * HEAD_DIM_C
    lanes = HEAD_DIM_C
    halves = tq // lanes
    n_tiles = i + 1

    @pl.when(i == 0)
    def _():
        for h in range(n_heads):
            far = rb_ref[NUM_BUCKETS - 1, h]
            for r in range(2):
                bt_ref[h // G, r, h % G] = _bias_from_buckets(btab_ref[r], rb_ref, h) - far

    for h in range(N_IDX_HEADS):
        qis_ref[h] = qi_ref[:, h * IDX_DIM:(h + 1) * IDX_DIM]
        wf_ref[h] = jnp.broadcast_to(wi_ref[:, IDX_DIM + h:IDX_DIM + h + 1], (tq, lanes))
    for c in range(N_KV_HEADS):
        for g in range(G):
            h = c * G + g
            qs_ref[c, g * tq:(g + 1) * tq, :] = qc_ref[:, h * HEAD_DIM_C:(h + 1) * HEAD_DIM_C]

    rowi = lax.broadcasted_iota(I32, (tq, tq), 0)
    coli = lax.broadcasted_iota(I32, (tq, tq), 1)

    def key_index(j):
        return j * tq + coli

    def visible(j):
        return key_index(j) <= (i * tq + rowi)

    def score_tile(j, carry):
        ki = kvb_ref[pl.ds(pl.multiple_of(j * tq, tq), tq), icol:icol + IDX_DIM]
        for rh in range(halves):
            rows = slice(rh * lanes, (rh + 1) * lanes)
            acc = jnp.zeros((lanes, tq), F32)
            for h in range(N_IDX_HEADS):
                w = wf_ref[h, rows, :]
                acc = acc + jnp.maximum(_dot_nt(qis_ref[h, rows, :], ki), 0.0) * jnp.concatenate([w] * halves, axis=1)
            score = acc * (N_IDX_HEADS ** -0.5 * IDX_DIM ** -0.5)
            score = jnp.where(visible(j)[rows], score, -jnp.inf)
            keys_ref[j, rows, :] = _sort_key(score)
        return carry

    lax.fori_loop(0, n_tiles, score_tile, 0)

    ones_sq = jnp.ones((lanes, lanes), BF16)

    def count(pred, *row_args):
        totals = []
        for rh in range(halves):
            rows = slice(rh * lanes, (rh + 1) * lanes)
            args = [a[rows] for a in row_args]

            def body(j, cnt):
                for a in range(halves):
                    cols = slice(a * lanes, (a + 1) * lanes)
                    idx = j * tq + a * lanes + lax.broadcasted_iota(I32, (lanes, lanes), 1)
                    cnt = cnt + jnp.where(pred(keys_ref[j, rows, cols], idx, *args), 1, 0)
                return cnt

            totals.append(lax.fori_loop(0, n_tiles, body, jnp.zeros((lanes, lanes), I32)))
        return _dot(jnp.concatenate(totals, axis=0).astype(F32).astype(BF16), ones_sq).astype(I32)

    def bit_step(b, st):
        tau, n_ge = st
        cand = tau + lax.shift_left(jnp.int32(1), 31 - b)
        c = count(lambda k, idx, cnd: k >= cnd, cand)
        ge = c >= n_keep
        return jnp.where(ge, cand, tau), jnp.where(ge, c, n_ge)

    sel_ref[0] = jnp.full((tq, lanes), INT_MIN, I32)
    sel_ref[1] = jnp.broadcast_to(n_tiles * tq, (tq, lanes)).astype(I32)
    for b0, b1 in zip(SELECT_BIT_GROUPS[:-1], SELECT_BIT_GROUPS[1:]):
        @pl.when(jnp.max(sel_ref[1]) > n_keep)
        def _():
            tau_g, n_ge_g = lax.fori_loop(b0, b1, bit_step, (sel_ref[0], sel_ref[1]))
            sel_ref[0] = tau_g
            sel_ref[1] = n_ge_g

    tau, n_ge = sel_ref[0], sel_ref[1]

    cut_ref[...] = jnp.full((tq, lanes), 2 ** 31 - 1, I32)

    @pl.when(jnp.max(n_ge) > n_keep)
    def _():
        need = n_keep - count(lambda k, idx, t: k > t, tau)

        def idx_step(b, cut):
            cand = cut + lax.shift_left(jnp.int32(1), idx_bits - 1 - b)
            below = count(lambda k, idx, t, c: (k == t) & (idx < c), tau, cand)
            return jnp.where(below < need, cand, cut)

        cut_ref[...] = lax.fori_loop(0, idx_bits, idx_step, jnp.zeros((tq, lanes), I32))

    tau_t = jnp.concatenate([tau] * halves, axis=1)
    cut_t = jnp.concatenate([cut_ref[...]] * halves, axis=1)

    def mask_tile(j, carry):
        k = keys_ref[j]
        sel = ((k > tau_t) | ((k == tau_t) & (key_index(j) <= cut_t))) & visible(j)
        keys_ref[j] = lax.bitcast_convert_type(jnp.where(sel, 0.0, NEG_BIG), I32)
        return carry

    lax.fori_loop(0, n_tiles, mask_tile, 0)

    def logits(j, c, r):
        base = pl.multiple_of(j * tq, tq)
        kc = kvb_ref[pl.ds(base, tq), kcol + c * HEAD_DIM_C:kcol + (c + 1) * HEAD_DIM_C]
        s = _dot_nt(qs_ref[c], kc).reshape(G, tq, tq) + lax.bitcast_convert_type(keys_ref[j], F32)[None]
        if r is not None:
            s = s + bt_ref[c, r]
        return s

    def max_tile(j, r):
        for c in range(N_KV_HEADS):
            s = logits(j, c, r)
            m = s[..., 0:lanes]
            for a in range(1, halves):
                m = jnp.maximum(m, s[..., a * lanes:(a + 1) * lanes])
            mx_ref[c] = jnp.maximum(mx_ref[c], m)

    def acc_tile(j, r):
        base = pl.multiple_of(j * tq, tq)
        for c in range(N_KV_HEADS):
            vc = kvb_ref[pl.ds(base, tq), vcol + c * HEAD_DIM_C:vcol + (c + 1) * HEAD_DIM_C]
            vext = jnp.concatenate([vc, jnp.ones((tq, lanes), BF16)], axis=1)
            m = mrep_ref[c]
            p = jnp.exp(logits(j, c, r) - jnp.concatenate([m] * halves, axis=-1))
            acc_ref[c] += _dot(p.reshape(G * tq, tq).astype(BF16), vext)

    def sweep(tile_fn):
        def far(j, carry):
            tile_fn(j, None)
            return carry

        lax.fori_loop(0, jnp.maximum(i - 1, 0), far, 0)

        @pl.when(i >= 1)
        def _():
            tile_fn(i - 1, 1)

        tile_fn(i, 0)

    @pl.when(i == 0)
    def _():
        kmax_ref[...] = jnp.zeros(kmax_ref.shape, F32)

    for c in range(N_KV_HEADS):
        kt = kvb_ref[pl.ds(pl.multiple_of(i * tq, tq), tq), kcol + c * HEAD_DIM_C:kcol + (c + 1) * HEAD_DIM_C].astype(F32)
        knorm = jnp.sqrt(jnp.max(jnp.sum(kt * kt, axis=1, keepdims=True)))
        kmax_ref[c] = jnp.maximum(kmax_ref[c], knorm)
        qf = qs_ref[c].astype(F32)
        qnorm = jnp.sqrt(jnp.sum(qf * qf, axis=1, keepdims=True)).reshape(G, tq, 1)
        for g in range(G):
            h = c * G + g
            far = rb_ref[NUM_BUCKETS - 1, h]
            bias_max = lax.fori_loop(0, NUM_BUCKETS, lambda b, m: jnp.maximum(m, rb_ref[b, h] - far), jnp.float32(0.0))
            mrep_ref[c, g] = qnorm[g] * kmax_ref[c, 0:1, :] + bias_max

    acc_ref[...] = jnp.zeros(acc_ref.shape, F32)
    sweep(acc_tile)

    @pl.when(jnp.logical_not(jnp.min(acc_ref[:, :, lanes:]) > 1e-30))
    def _():
        mx_ref[...] = jnp.full(mx_ref.shape, NEG_BIG, F32)
        sweep(max_tile)
        mrep_ref[...] = jnp.broadcast_to(jnp.max(mx_ref[...], axis=-1, keepdims=True), mrep_ref.shape)
        acc_ref[...] = jnp.zeros(acc_ref.shape, F32)
        sweep(acc_tile)

    for c in range(N_KV_HEADS):
        for g in range(G):
            h = c * G + g
            a = acc_ref[c, g * tq:(g + 1) * tq, :]
            o_ref[:, h * HEAD_DIM_C:(h + 1) * HEAD_DIM_C] = (a[:, :lanes] / a[:, lanes:]).astype(o_ref.dtype)


def _dsa_prompt(qcqi, tail, kvi_bf16, rel_bias, *, tq, n_keep):
    S = qcqi.shape[0]
    dc = qcqi.shape[1] - N_IDX_HEADS * IDX_DIM
    n_heads = dc // HEAD_DIM_C
    G = n_heads // N_KV_HEADS
    wkv = kvi_bf16.shape[1]
    d = np.arange(tq)[:, None] - np.arange(tq)[None, :]
    btab = jnp.asarray(np.stack([_t5_bucket_np(d + r * tq) for r in range(2)]))
    assert _t5_bucket_np(np.array([tq + 1]))[0] == NUM_BUCKETS - 1
    lanes = HEAD_DIM_C
    grid_spec = pltpu.PrefetchScalarGridSpec(
        num_scalar_prefetch=0,
        grid=(S // tq,),
        in_specs=[
            pl.BlockSpec(memory_space=pltpu.SMEM),
            pl.BlockSpec((tq, dc), lambda i: (i, 0)),
            pl.BlockSpec((tq, N_IDX_HEADS * IDX_DIM), lambda i: (i, dc // (N_IDX_HEADS * IDX_DIM))),
            pl.BlockSpec((tq, tail.shape[1]), lambda i: (i, 0)),
            pl.BlockSpec((S, wkv), lambda i: (0, 0), pipeline_mode=pl.Buffered(1)),
            pl.BlockSpec((2, tq, tq), lambda i: (0, 0, 0)),
        ],
        out_specs=pl.BlockSpec((tq, dc), lambda i: (i, 0)),
        scratch_shapes=[
            pltpu.VMEM((S // tq, tq, tq), I32),
            pltpu.VMEM((N_KV_HEADS, 2, G, tq, tq), F32),
            pltpu.VMEM((N_IDX_HEADS, tq, lanes), F32),
            pltpu.VMEM((N_IDX_HEADS, tq, IDX_DIM), BF16),
            pltpu.VMEM((N_KV_HEADS, G * tq, HEAD_DIM_C), BF16),
            pltpu.VMEM((N_KV_HEADS, G, tq, lanes), F32),
            pltpu.VMEM((N_KV_HEADS, G, tq, lanes), F32),
            pltpu.VMEM((N_KV_HEADS, G * tq, 2 * lanes), F32),
            pltpu.VMEM((tq, lanes), I32),
            pltpu.VMEM((N_KV_HEADS, 8, lanes), F32),
            pltpu.VMEM((2, tq, lanes), I32),
        ],
    )
    return pl.pallas_call(
        functools.partial(_dsa_prompt_body, tq=tq, n_keep=n_keep, idx_bits=(S - 1).bit_length()),
        grid_spec=grid_spec,
        out_shape=jax.ShapeDtypeStruct((S, dc), BF16),
        compiler_params=_cparams(("arbitrary",)),
        name="dsa_prompt",
    )(rel_bias, qcqi, qcqi, tail, kvi_bf16, btab)


def _dsa_sel_body(pt_ref, qi_ref, w_ref, kin_ref, *rest, t_new, n_keep):
    pages = rest[:PAGES_PER_STEP]
    mb_ref, keys_ref, arg_ref, cnt_ref = rest[PAGES_PER_STEP:]
    b = pl.program_id(0)
    p = pl.program_id(1)
    B, n_chunks, tp, P = keys_ref.shape
    last = p == pl.num_programs(1) - 1
    q = qi_ref[0]
    w = w_ref[0]
    rowi = lax.broadcasted_iota(I32, (tp, P), 0)
    coli = lax.broadcasted_iota(I32, (tp, P), 1)
    vis_new = (coli <= rowi) & (coli < t_new)

    def chunk_scores(s):
        s = jnp.maximum(s, 0.0) * w
        s = s.reshape(tp, N_IDX_HEADS, s.shape[-1]).sum(axis=1)
        return s * (N_IDX_HEADS ** -0.5 * IDX_DIM ** -0.5)

    for r in range(PAGES_PER_STEP):
        keys_ref[b, p * PAGES_PER_STEP + r] = _sort_key(chunk_scores(_dot(q, pages[r][0].astype(BF16))))

    @pl.when(last)
    def _():
        s_new = jnp.where(vis_new, chunk_scores(_dot_nt(q, kin_ref[0])), -jnp.inf)
        keys_ref[b, n_chunks - 1] = _sort_key(s_new)

    @pl.when(last & (b == B - 1))
    def _():
        key_index = (lax.broadcasted_iota(I32, (n_chunks, tp, P), 0) * P
                     + lax.broadcasted_iota(I32, (n_chunks, tp, P), 2))

        def count(pred, *row_args):
            for n, a in enumerate(row_args):
                arg_ref[n] = a

            def per_seq(s, carry):
                args = [arg_ref[n, s][None] for n in range(len(row_args))]
                cnt_ref[s] = jnp.sum(jnp.where(pred(keys_ref[s], key_index, *args), 1, 0), axis=0)
                return carry

            lax.fori_loop(0, B, per_seq, 0)
            return jnp.broadcast_to(jnp.sum(cnt_ref[...], axis=-1, keepdims=True), (B, tp, P))

        def bit_step(bit, tau):
            cand = tau + lax.shift_left(jnp.int32(1), 31 - bit)
            return jnp.where(count(lambda k, idx, c: k >= c, cand) >= n_keep, cand, tau)

        tau = lax.fori_loop(0, 32, bit_step, jnp.full((B, tp, P), INT_MIN, I32))

        arg_ref[2] = jnp.full((B, tp, P), 2 ** 31 - 1, I32)

        @pl.when(jnp.max(count(lambda k, idx, t: k >= t, tau)) > n_keep)
        def _():
            need = n_keep - count(lambda k, idx, t: k > t, tau)
            idx_bits = (n_chunks * P - 1).bit_length()

            def idx_step(bit, cut):
                cand = cut + lax.shift_left(jnp.int32(1), idx_bits - 1 - bit)
                below = count(lambda k, idx, t, c: (k == t) & (idx < c), tau, cand)
                return jnp.where(below < need, cand, cut)

            arg_ref[2] = lax.fori_loop(0, idx_bits, idx_step, jnp.zeros((B, tp, P), I32))

        arg_ref[0] = tau

        def write_mask(s, carry):
            k = keys_ref[s]
            t = arg_ref[0, s][None]
            sel = (k > t) | ((k == t) & (key_index <= arg_ref[2, s][None]))
            mb_ref[s] = jnp.where(sel, 0.0, NEG_BIG)
            mb_ref[s, n_chunks - 1] = jnp.where(sel[n_chunks - 1] & vis_new, 0.0, NEG_BIG)
            return carry

        lax.fori_loop(0, B, write_mask, 0)


def _dsa_sample_select(pt_flat, qi, w, ki_new, pool_ki, *, n_pages, t_new, n_keep):
    B = qi.shape[0]
    P = pool_ki.shape[2]
    steps = n_pages // PAGES_PER_STEP

    def page_spec(r):
        return pl.BlockSpec((1, IDX_DIM, P), lambda b, p, pt: (pt[b * n_pages + p * PAGES_PER_STEP + r], 0, 0))

    grid_spec = pltpu.PrefetchScalarGridSpec(
        num_scalar_prefetch=1,
        grid=(B, steps),
        in_specs=[
            pl.BlockSpec((1,) + qi.shape[1:], lambda b, p, pt: (b, 0, 0)),
            pl.BlockSpec((1,) + w.shape[1:], lambda b, p, pt: (b, 0, 0)),
            pl.BlockSpec((1,) + ki_new.shape[1:], lambda b, p, pt: (b, 0, 0)),
        ] + [page_spec(r) for r in range(PAGES_PER_STEP)],
        out_specs=pl.BlockSpec((B, n_pages + 1, SAMPLE_PAD_T, P), lambda b, p, pt: (0, 0, 0, 0)),
        scratch_shapes=[
            pltpu.VMEM((B, n_pages + 1, SAMPLE_PAD_T, P), I32),
            pltpu.VMEM((3, B, SAMPLE_PAD_T, P), I32),
            pltpu.VMEM((B, SAMPLE_PAD_T, P), I32),
        ],
    )
    return pl.pallas_call(
        functools.partial(_dsa_sel_body, t_new=t_new, n_keep=n_keep),
        grid_spec=grid_spec,
        out_shape=jax.ShapeDtypeStruct((B, n_pages + 1, SAMPLE_PAD_T, P), F32),
        compiler_params=_cparams(("arbitrary", "arbitrary")),
        name="dsa_sample_select",
    )(pt_flat, qi, w, ki_new, *([pool_ki] * PAGES_PER_STEP))


def _dsa_att_body(pt_ref, rb_ref, q_ref, mb_ref, kn_ref, vn_ref, btab_ref, *rest):
    kp = rest[:PAGES_PER_STEP]
    vp = rest[PAGES_PER_STEP:2 * PAGES_PER_STEP]
    o_ref, bt_ref, m_ref, acc_ref = rest[2 * PAGES_PER_STEP:]
    b = pl.program_id(0)
    p = pl.program_id(1)
    n_chunks = mb_ref.shape[1]
    tp = SAMPLE_PAD_T
    G = q_ref.shape[2] // tp
    rows = G * tp
    P = kn_ref.shape[1]
    lanes = HEAD_DIM_C
    last = p == pl.num_programs(1) - 1

    @pl.when((b == 0) & (p == 0))
    def _():
        for c in range(N_KV_HEADS):
            for kind in range(2):
                for g in range(G):
                    h = c * G + g
                    bt_ref[c, kind, g * tp:(g + 1) * tp, :] = (_bias_from_buckets(btab_ref[kind], rb_ref, h)
                                                              - rb_ref[NUM_BUCKETS - 1, h])

    @pl.when(p == 0)
    def _():
        m_ref[...] = jnp.full(m_ref.shape, NEG_BIG, F32)
        acc_ref[...] = jnp.zeros(acc_ref.shape, F32)

    def attend(c, s, vext):
        m_old = m_ref[c]
        m_new = jnp.maximum(m_old, jnp.max(s, axis=1, keepdims=True))
        alpha = jnp.exp(m_old - m_new)
        pr = jnp.exp(s - jnp.concatenate([m_new] * (s.shape[1] // lanes), axis=1))
        acc_ref[c] = jnp.concatenate([alpha, alpha], axis=1) * acc_ref[c] + _dot(pr.astype(BF16), vext)
        m_ref[c] = m_new

    ones = jnp.ones((P, lanes), BF16)
    mb_step = jnp.concatenate([mb_ref[0, p * PAGES_PER_STEP + r] for r in range(PAGES_PER_STEP)], axis=1)
    mb_step = jnp.concatenate([mb_step] * G, axis=0)
    for c in range(N_KV_HEADS):
        head_rows = pl.ds(c, P, stride=N_KV_HEADS)
        k_all = jnp.concatenate([kp[r][0, head_rows, :].astype(BF16) for r in range(PAGES_PER_STEP)], axis=0)
        v_all = jnp.concatenate(
            [jnp.concatenate([vp[r][0, head_rows, :].astype(BF16), ones], axis=1) for r in range(PAGES_PER_STEP)], axis=0)
        s = _dot_nt(q_ref[0, c], k_all) + mb_step
        s = jnp.concatenate([s[:, :-P], s[:, -P:] + jnp.where(last, bt_ref[c, 0], 0.0)], axis=1)
        attend(c, s, v_all)

    @pl.when(last)
    def _():
        mb_new = jnp.concatenate([mb_ref[0, n_chunks - 1]] * G, axis=0)
        for c in range(N_KV_HEADS):
            cols = slice(c * HEAD_DIM_C, (c + 1) * HEAD_DIM_C)
            s = _dot_nt(q_ref[0, c], kn_ref[0, :, cols]) + mb_new + bt_ref[c, 1]
            attend(c, s, jnp.concatenate([vn_ref[0, :, cols], ones], axis=1))
            a = acc_ref[c]
            o_ref[0, c] = a[:, :lanes] / a[:, lanes:]


def _dsa_sample_attend(pt_flat, rel_bias, q, mb, k_new, v_new, pool_k, pool_v, *, n_pages, past):
    B, kvh, rows, dh = q.shape
    P = pool_k.shape[1] // kvh
    steps = n_pages // PAGES_PER_STEP
    tp = SAMPLE_PAD_T
    t = np.arange(tp)[:, None]
    col = np.arange(P)[None, :]
    assert past - (n_pages - 1) * P >= MAX_DISTANCE
    last_page = _t5_bucket_np(past + t - ((n_pages - 1) * P + col))
    new = _t5_bucket_np(t - col)
    btab = jnp.asarray(np.stack([last_page, new]))

    def page_spec(r):
        return pl.BlockSpec((1, P * kvh, dh), lambda b, p, pt: (pt[b * n_pages + p * PAGES_PER_STEP + r], 0, 0))

    grid_spec = pltpu.PrefetchScalarGridSpec(
        num_scalar_prefetch=1,
        grid=(B, steps),
        in_specs=[
            pl.BlockSpec(memory_space=pltpu.SMEM),
            pl.BlockSpec((1, kvh, rows, dh), lambda b, p, pt: (b, 0, 0, 0)),
            pl.BlockSpec((1,) + mb.shape[1:], lambda b, p, pt: (b, 0, 0, 0)),
            pl.BlockSpec((1,) + k_new.shape[1:], lambda b, p, pt: (b, 0, 0)),
            pl.BlockSpec((1,) + v_new.shape[1:], lambda b, p, pt: (b, 0, 0)),
            pl.BlockSpec((2, tp, P), lambda b, p, pt: (0, 0, 0)),
        ] + [page_spec(r) for r in range(PAGES_PER_STEP)] * 2,
        out_specs=pl.BlockSpec((1, kvh, rows, dh), lambda b, p, pt: (b, 0, 0, 0)),
        scratch_shapes=[
            pltpu.VMEM((kvh, 2, rows, P), F32),
            pltpu.VMEM((kvh, rows, dh), F32),
            pltpu.VMEM((kvh, rows, 2 * dh), F32),
        ],
    )
    return pl.pallas_call(
        _dsa_att_body,
        grid_spec=grid_spec,
        out_shape=jax.ShapeDtypeStruct((B, kvh, rows, dh), F32),
        compiler_params=_cparams(("arbitrary", "arbitrary")),
        name="dsa_sample_attend",
    )(pt_flat, rel_bias, q, mb, k_new, v_new, btab, *([pool_k] * PAGES_PER_STEP), *([pool_v] * PAGES_PER_STEP))


def _dsa_sample(c1, tail, c2b, pt_flat, rel_bias, pool_k, pool_v, pool_ki, *, Bs, Ts, n_pages):
    tp = SAMPLE_PAD_T
    kv_w = N_KV_HEADS * HEAD_DIM_C
    d_c = c1.shape[1] - N_IDX_HEADS * IDX_DIM
    G = d_c // kv_w
    P = pool_ki.shape[2]
    past = n_pages * P
    qi_s = c1[:, d_c:].reshape(Bs, tp * N_IDX_HEADS, IDX_DIM)
    w_s = tail[:, IDX_DIM:IDX_DIM + N_IDX_HEADS].reshape(Bs, tp * N_IDX_HEADS, 1)
    new_rows = jnp.pad(c2b.reshape(Bs, tp, -1), ((0, 0), (0, P - tp), (0, 0)))
    mb = _dsa_sample_select(pt_flat, qi_s, w_s, new_rows[:, :, 2 * kv_w:2 * kv_w + IDX_DIM], pool_ki,
                            n_pages=n_pages, t_new=Ts, n_keep=min(TOPK_MAX, (past + Ts) // 4))
    q_s = c1[:, :d_c].reshape(Bs, tp, N_KV_HEADS, G, HEAD_DIM_C).transpose(0, 2, 3, 1, 4)
    q_s = q_s.reshape(Bs, N_KV_HEADS, G * tp, HEAD_DIM_C)
    o_s = _dsa_sample_attend(pt_flat, rel_bias, q_s, mb, new_rows[:, :, :kv_w], new_rows[:, :, kv_w:2 * kv_w],
                             pool_k, pool_v, n_pages=n_pages, past=past)
    return o_s.reshape(Bs, N_KV_HEADS, G, tp, HEAD_DIM_C).transpose(0, 3, 1, 2, 4).reshape(Bs * tp, d_c).astype(BF16)


def _regroup_w_in(w, d_a, d_b, d_c, rk, kv_w, qi_w):
    o = np.cumsum([0, d_a, d_a, rk, rk, d_b, d_b, d_c, kv_w, kv_w, qi_w, IDX_DIM, N_IDX_HEADS])
    wb = w.astype(BF16)
    tail = wb[:, o[10]:o[12]]
    pad = jnp.zeros((w.shape[0], 128 - tail.shape[1]), BF16)
    return jnp.concatenate([wb[:, o[0]:o[6]], wb[:, o[6]:o[7]], wb[:, o[9]:o[10]], wb[:, o[7]:o[9]], tail, pad], axis=1)


def kernel(x_prompt, x_sample, cache_k, cache_v, cache_kidx, page_table, state_lru_h, state_conv, state_ret,
           norm_ffn1, ffn1_gate, ffn1_up, ffn1_down, norm_mix, w_in, w_out, conv_w, conv_b,
           lru_wa, lru_ba, lru_wx, lru_bx, lru_lambda, rel_bias,
           norm_ffn2, ffn2_gate, ffn2_up, ffn2_down, norm_final):
    depth = norm_ffn1.shape[0]
    _, S, D = x_prompt.shape
    Bs, Ts, _ = x_sample.shape
    n_pool, P = cache_k.shape[1], cache_k.shape[2]
    n_pages = page_table.shape[1]
    past = n_pages * P
    d_a = state_lru_h.shape[-1]
    H_r, rdk, rdv = state_ret.shape[2:]
    d_b = H_r * rdv
    d_c = D - d_a - d_b
    kv_w = N_KV_HEADS * HEAD_DIM_C
    qi_w = N_IDX_HEADS * IDX_DIM
    G = d_c // HEAD_DIM_C // N_KV_HEADS
    tp = SAMPLE_PAD_T
    Rs = Bs * tp

    xp = x_prompt.reshape(S, D)
    xs = jnp.pad(x_sample, ((0, 0), (0, tp - Ts), (0, 0))).reshape(Rs, D)

    cos_p, sin_p = _rope_tables(jnp.arange(S, dtype=I32), rdk, H_r)
    cos_s, sin_s = _rope_tables(past + jnp.arange(tp, dtype=I32), rdk, H_r)

    pool_k = cache_k.reshape(depth * n_pool, P * N_KV_HEADS, HEAD_DIM_C)
    pool_v = cache_v.reshape(depth * n_pool, P * N_KV_HEADS, HEAD_DIM_C)
    pool_ki = jnp.swapaxes(cache_kidx, 2, 3).reshape(depth * n_pool, IDX_DIM, P)

    zeros_conv = jnp.zeros((1, 8, d_a), F32)
    zeros_h = jnp.zeros((1, 1, d_a), F32)
    zeros_s = jnp.zeros((1, H_r, rdk, rdv), F32)

    outs_p, outs_s = [], []
    y_prompt = y_sample = None
    for l in range(depth):
        w_p = _regroup_w_in(w_in[l], d_a, d_b, d_c, H_r * rdk, kv_w, qi_w)
        widths = (2 * d_a, 2 * H_r * rdk + 2 * d_b, d_c + qi_w, w_p.shape[1] - 2 * d_a - 2 * H_r * rdk - 2 * d_b - d_c - qi_w)
        in_proj = functools.partial(_in_proj, g=norm_mix[l], w=w_p, widths=widths, q_scale=HEAD_DIM_C ** -0.5)
        wo = w_out[l].astype(BF16)
        lwa, lwx = lru_wa[l].astype(BF16), lru_wx[l].astype(BF16)
        last = l == depth - 1
        pt_flat = (page_table + l * n_pool).reshape(-1).astype(I32)

        xs, *f1 = _ffn_cast(xs, norm_ffn1[l], ffn1_gate, ffn1_up, ffn1_down, l, tf=512)
        ua, ub, c1, k_new, v_new, ki_new, tail, c2b = in_proj(xs, tm=Rs)
        conv0 = jnp.pad(state_conv[l], ((0, 0), (8 - (CONV_WIDTH - 1), 0), (0, 0)))
        ya, h_last, conv_new = _lru(ua.reshape(Bs, tp, 2 * d_a), conv0, state_lru_h[l].reshape(Bs, 1, d_a),
                                    conv_w[l], conv_b[l], lwa, lru_ba[l], lwx, lru_bx[l], lru_lambda[l],
                                    tc=tp, t_valid_last=Ts)
        yb, s_last = _retention(ub.reshape(Bs, tp, -1), cos_s, sin_s, state_ret[l], cp=tp, c_valid=Ts, mm_dtype=F32)
        yc = _dsa_sample(c1, tail, c2b, pt_flat, rel_bias, pool_k, pool_v, pool_ki, Bs=Bs, Ts=Ts, n_pages=n_pages)
        xs = _out_proj(xs, ya.reshape(Rs, d_a), yb.reshape(Rs, d_b), yc, wo, tm=Rs)
        if last:
            xs, y_sample, *f2 = _ffn_cast(xs, norm_ffn2[l], ffn2_gate, ffn2_up, ffn2_down, l, norm_final, tf=512)
        else:
            xs, *f2 = _ffn_cast(xs, norm_ffn2[l], ffn2_gate, ffn2_up, ffn2_down, l, tf=512)
        valid = lambda a: a.reshape(Bs, tp, -1)[:, :Ts]
        outs_s.append((valid(k_new).reshape(Bs, Ts, N_KV_HEADS, HEAD_DIM_C),
                       valid(v_new).reshape(Bs, Ts, N_KV_HEADS, HEAD_DIM_C),
                       valid(ki_new),
                       h_last.reshape(Bs, d_a), conv_new, s_last))

        xp = _ffn(xp, norm_ffn1[l], *f1, tm=512, tf=512)
        ua, ub, c1, k_new, v_new, ki_new, tail, c2b = in_proj(xp, tm=256)
        ya, h_last, conv_new = _lru(ua.reshape(1, S, 2 * d_a), zeros_conv, zeros_h, conv_w[l], conv_b[l],
                                    lwa, lru_ba[l], lwx, lru_bx[l], lru_lambda[l], tc=1024, t_valid_last=1024)
        yb, s_last = _retention(ub.reshape(1, S, -1), cos_p, sin_p, zeros_s, cp=128, c_valid=128, mm_dtype=BF16)
        yc = _dsa_prompt(c1, tail, c2b, rel_bias, tq=256, n_keep=min(TOPK_MAX, S // 4))
        xp = _out_proj(xp, ya.reshape(S, d_a), yb.reshape(S, d_b), yc, wo, tm=512)
        if last:
            xp, y_prompt = _ffn(xp, norm_ffn2[l], *f2, norm_final, tm=512, tf=512)
        else:
            xp = _ffn(xp, norm_ffn2[l], *f2, tm=512, tf=512)
        outs_p.append((k_new.reshape(1, S, N_KV_HEADS, HEAD_DIM_C),
                       v_new.reshape(1, S, N_KV_HEADS, HEAD_DIM_C),
                       ki_new.reshape(1, S, IDX_DIM),
                       h_last.reshape(1, d_a), conv_new, s_last))

    stack = lambda outs, k: jnp.stack([o[k] for o in outs])
    return ((y_prompt.reshape(1, S, D), y_sample.reshape(Bs, tp, D)[:, :Ts])
            + tuple(stack(outs_p, k) for k in range(6))
            + tuple(stack(outs_s, k) for k in range(6)))
```

```python
import functools
import math

import numpy as np
import jax
import jax.numpy as jnp
from jax import lax
from jax.experimental import pallas as pl
from jax.experimental.pallas import tpu as pltpu

F32 = jnp.float32
BF16 = jnp.bfloat16
I32 = jnp.int32

EPS = 1e-6
N_A_BLOCKS = 4
CONV_WIDTH = 4
LRU_C = 8.0
N_RET_HEADS = 4
ROPE_BASE = 10000.0
HEAD_DIM_C = 128
N_KV_HEADS = 2
N_IDX_HEADS = 16
IDX_DIM = 64
TOPK_MAX = 256
NUM_BUCKETS = 32
MAX_DISTANCE = 128
SAMPLE_PAD_T = 8
PAGES_PER_STEP = 16
NEG_BIG = -1e30
INT_MIN = -2 ** 31
SELECT_BIT_GROUPS = (0, 22, 25, 28, 32)

VMEM_LIMIT = 56 * 1024 * 1024


def _cparams(sem):
    return pltpu.CompilerParams(dimension_semantics=sem, vmem_limit_bytes=VMEM_LIMIT)


def _rms(x, g):
    return x * lax.rsqrt(jnp.mean(x * x, axis=-1, keepdims=True) + EPS) * g


def _dot(a, b):
    return jnp.dot(a, b, preferred_element_type=F32)


def _dot_nt(a, b):
    return lax.dot_general(a, b, (((1,), (1,)), ((), ())), preferred_element_type=F32)


def _dot_tn(a, b):
    return lax.dot_general(a, b, (((0,), (0,)), ((), ())), preferred_element_type=F32)


def _sort_key(score):
    bits = lax.bitcast_convert_type(score, I32)
    return jnp.where(bits < 0, bits ^ jnp.int32(0x7FFFFFFF), bits)


def _ffn_body(x_ref, g_ref, wg_ref, wu_ref, wd_ref, *rest, final_norm):
    if final_norm:
        gf_ref, o_ref, on_ref, h_ref = rest
    else:
        o_ref, h_ref = rest
    j = pl.program_id(1)

    @pl.when(j == 0)
    def _():
        h_ref[...] = _rms(x_ref[...], g_ref[...]).astype(BF16)
        o_ref[...] = jnp.zeros_like(o_ref)

    h = h_ref[...]
    g = _dot(h, wg_ref[...])
    u = _dot(h, wu_ref[...])
    a = (g * jax.nn.sigmoid(g) * u).astype(BF16)
    o_ref[...] += _dot(a, wd_ref[...])

    @pl.when(j == pl.num_programs(1) - 1)
    def _():
        y = x_ref[...] + 0.5 * o_ref[...]
        o_ref[...] = y
        if final_norm:
            on_ref[...] = _rms(y, gf_ref[...])


def _ffn(x, g, wg, wu, wd, gf=None, *, tm, tf):
    R, D = x.shape
    FF = wg.shape[1]
    final_norm = gf is not None
    in_specs = [
        pl.BlockSpec((tm, D), lambda i, j: (i, 0)),
        pl.BlockSpec((1, D), lambda i, j: (0, 0)),
        pl.BlockSpec((D, tf), lambda i, j: (0, j)),
        pl.BlockSpec((D, tf), lambda i, j: (0, j)),
        pl.BlockSpec((tf, D), lambda i, j: (j, 0)),
    ]
    args = [x, g.reshape(1, D), wg, wu, wd]
    out_shape = [jax.ShapeDtypeStruct((R, D), F32)]
    out_specs = [pl.BlockSpec((tm, D), lambda i, j: (i, 0))]
    if final_norm:
        in_specs.append(pl.BlockSpec((1, D), lambda i, j: (0, 0)))
        args.append(gf.reshape(1, D))
        out_shape.append(jax.ShapeDtypeStruct((R, D), F32))
        out_specs.append(pl.BlockSpec((tm, D), lambda i, j: (i, 0)))
    res = pl.pallas_call(
        functools.partial(_ffn_body, final_norm=final_norm),
        grid=(R // tm, FF // tf),
        in_specs=in_specs,
        out_specs=out_specs,
        out_shape=out_shape,
        scratch_shapes=[pltpu.VMEM((tm, D), BF16)],
        compiler_params=_cparams(("parallel", "arbitrary")),
        name="ffn",
    )(*args)
    return res if final_norm else res[0]


def _ffn_cast_body(x_ref, g_ref, wg_ref, wu_ref, wd_ref, *rest, final_norm):
    if final_norm:
        gf_ref, o_ref, on_ref, wgb_ref, wub_ref, wdb_ref, h_ref = rest
    else:
        o_ref, wgb_ref, wub_ref, wdb_ref, h_ref = rest
    j = pl.program_id(0)

    @pl.when(j == 0)
    def _():
        h_ref[...] = _rms(x_ref[...], g_ref[...]).astype(BF16)
        o_ref[...] = jnp.zeros_like(o_ref)

    wg, wu, wd = wg_ref[0].astype(BF16), wu_ref[0].astype(BF16), wd_ref[0].astype(BF16)
    wgb_ref[...] = wg
    wub_ref[...] = wu
    wdb_ref[...] = wd
    h = h_ref[...]
    g = _dot(h, wg)
    a = (g * jax.nn.sigmoid(g) * _dot(h, wu)).astype(BF16)
    o_ref[...] += _dot(a, wd)

    @pl.when(j == pl.num_programs(0) - 1)
    def _():
        y = x_ref[...] + 0.5 * o_ref[...]
        o_ref[...] = y
        if final_norm:
            on_ref[...] = _rms(y, gf_ref[...])


def _ffn_cast(x, g, wg, wu, wd, layer, gf=None, *, tf):
    R, D = x.shape
    FF = wg.shape[2]
    final_norm = gf is not None
    in_specs = [
        pl.BlockSpec((R, D), lambda j: (0, 0)),
        pl.BlockSpec((1, D), lambda j: (0, 0)),
        pl.BlockSpec((1, D, tf), lambda j: (layer, 0, j)),
        pl.BlockSpec((1, D, tf), lambda j: (layer, 0, j)),
        pl.BlockSpec((1, tf, D), lambda j: (layer, j, 0)),
    ]
    args = [x, g.reshape(1, D), wg, wu, wd]
    out_shape = [jax.ShapeDtypeStruct((R, D), F32)]
    out_specs = [pl.BlockSpec((R, D), lambda j: (0, 0))]
    if final_norm:
        in_specs.append(pl.BlockSpec((1, D), lambda j: (0, 0)))
        args.append(gf.reshape(1, D))
        out_shape.append(jax.ShapeDtypeStruct((R, D), F32))
        out_specs.append(pl.BlockSpec((R, D), lambda j: (0, 0)))
    out_shape += [jax.ShapeDtypeStruct((D, FF), BF16), jax.ShapeDtypeStruct((D, FF), BF16),
                  jax.ShapeDtypeStruct((FF, D), BF16)]
    out_specs += [pl.BlockSpec((D, tf), lambda j: (0, j)), pl.BlockSpec((D, tf), lambda j: (0, j)),
                  pl.BlockSpec((tf, D), lambda j: (j, 0))]
    return pl.pallas_call(
        functools.partial(_ffn_cast_body, final_norm=final_norm),
        grid=(FF // tf,),
        in_specs=in_specs,
        out_specs=out_specs,
        out_shape=out_shape,
        scratch_shapes=[pltpu.VMEM((R, D), BF16)],
        compiler_params=_cparams(("arbitrary",)),
        name="ffn_cast",
    )(*args)


def _in_proj_body(x_ref, g_ref, w_ref, wt_ref, ua_ref, ub_ref, c1_ref, k_ref, v_ref, ki_ref, tail_ref, c2b_ref,
                  *, q_scale, cols):
    h = _rms(x_ref[...], g_ref[...]).astype(BF16)
    o_a, o_b, o_qc, o_kv, o_qi, o_end = cols
    kv_w = k_ref.shape[1]
    d_c = o_kv - o_qc
    ua_ref[...] = _dot(h, w_ref[0, :, o_a:o_b])
    ub_ref[...] = _dot(h, w_ref[0, :, o_b:o_qc])
    c1_ref[:, :d_c] = (_dot(h, w_ref[0, :, o_qc:o_kv]) * q_scale).astype(BF16)
    c1_ref[:, d_c:] = _dot(h, w_ref[0, :, o_qi:o_end]).astype(BF16)
    kv = _dot(h, w_ref[0, :, o_kv:o_qi])
    t = _dot(h, wt_ref[0])
    k_ref[...] = kv[:, :kv_w]
    v_ref[...] = kv[:, kv_w:]
    ki_ref[...] = t[:, :IDX_DIM]
    tail_ref[...] = t
    c2b_ref[:, :2 * kv_w] = kv.astype(BF16)
    c2b_ref[:, 2 * kv_w:] = t.astype(BF16)


def _in_proj(x, g, w_all, w_tail, layer, *, tm, cols, q_scale):
    R, D = x.shape
    o_a, o_b, o_qc, o_kv, o_qi, o_end = cols
    kv_w = N_KV_HEADS * HEAD_DIM_C
    tw = w_tail.shape[2]
    outs = [(o_b - o_a, F32), (o_qc - o_b, F32), (o_kv - o_qc + o_end - o_qi, BF16), (kv_w, F32), (kv_w, F32),
            (IDX_DIM, F32), (tw, F32), (2 * kv_w + tw, BF16)]
    return pl.pallas_call(
        functools.partial(_in_proj_body, q_scale=q_scale, cols=cols),
        grid=(R // tm,),
        in_specs=[pl.BlockSpec((tm, D), lambda i: (i, 0)), pl.BlockSpec((1, D), lambda i: (0, 0)),
                  pl.BlockSpec((1,) + w_all.shape[1:], lambda i: (layer, 0, 0), pipeline_mode=pl.Buffered(1)),
                  pl.BlockSpec((1,) + w_tail.shape[1:], lambda i: (layer, 0, 0), pipeline_mode=pl.Buffered(1))],
        out_specs=[pl.BlockSpec((tm, wd), lambda i: (i, 0)) for wd, _ in outs],
        out_shape=[jax.ShapeDtypeStruct((R, wd), dt) for wd, dt in outs],
        compiler_params=_cparams(("parallel",)),
        name="in_proj",
    )(x, g.reshape(1, D), w_all, w_tail)


def _oproj_body(x_ref, ya_ref, yb_ref, yc_ref, w_ref, o_ref):
    da, db = ya_ref.shape[1], yb_ref.shape[1]
    o_ref[...] = (x_ref[...] + _dot(ya_ref[...], w_ref[:da]) + _dot(yb_ref[...], w_ref[da:da + db])
                  + _dot(yc_ref[...], w_ref[da + db:]))


def _out_proj(x, ya, yb, yc, w, *, tm):
    R, D = x.shape
    row = lambda a: pl.BlockSpec((tm, a.shape[1]), lambda i: (i, 0))
    return pl.pallas_call(
        _oproj_body,
        grid=(R // tm,),
        in_specs=[row(x), row(ya), row(yb), row(yc),
                  pl.BlockSpec(w.shape, lambda i: (0, 0), pipeline_mode=pl.Buffered(1))],
        out_specs=row(x),
        out_shape=jax.ShapeDtypeStruct((R, D), F32),
        compiler_params=_cparams(("parallel",)),
        name="out_proj",
    )(x, ya, yb, yc, w)


def _lru_body(xa_ref, ga_ref, conv0_ref, h0_ref, cw_ref, cb_ref, wa_ref, ba_ref, wx_ref, bx_ref, lam_ref,
              ya_ref, hl_ref, cn_ref, xs_ref, a_ref, u_ref, hs_ref, hc_ref, *, tc, t_valid_last):
    t = pl.program_id(1)
    da = xa_ref.shape[-1]
    blk = da // N_A_BLOCKS
    tail = 8

    @pl.when(t == 0)
    def _():
        xs_ref[0:tail, :] = conv0_ref[0]
        hc_ref[0:1, :] = h0_ref[0]

    xs_ref[tail:tail + tc, :] = xa_ref[0]
    xc = cb_ref[...]
    for j in range(CONV_WIDTH):
        off = tail - (CONV_WIDTH - 1) + j
        xc = xc + xs_ref[off:off + tc, :] * cw_ref[j:j + 1, :]
    xcb = xc.astype(BF16)
    pre_r = jnp.concatenate([_dot(xcb[:, n * blk:(n + 1) * blk], wa_ref[n]) for n in range(N_A_BLOCKS)], axis=1)
    pre_i = jnp.concatenate([_dot(xcb[:, n * blk:(n + 1) * blk], wx_ref[n]) for n in range(N_A_BLOCKS)], axis=1)
    r = jax.nn.sigmoid(pre_r + ba_ref[...])
    gi = jax.nn.sigmoid(pre_i + bx_ref[...])
    z = -lam_ref[...]
    softplus = jnp.maximum(z, 0.0) + jnp.log1p(jnp.exp(-jnp.abs(z)))
    log_a = (-LRU_C) * r * softplus
    a = jnp.exp(log_a)
    u = jnp.sqrt(-jnp.tanh(log_a) * (a * a + 1.0)) * gi * xc

    row_in_group = lax.broadcasted_iota(I32, (tc, da), 0) % 8
    for s in (1, 2, 4):
        first = row_in_group < s
        a_lo = jnp.where(first, 1.0, pltpu.roll(a, s, 0))
        u_lo = jnp.where(first, 0.0, pltpu.roll(u, s, 0))
        u = a * u_lo + u
        a = a * a_lo
    a_ref[...] = a
    u_ref[...] = u

    def step(k, h):
        base = pl.multiple_of(k * 8, 8)
        hg = a_ref[pl.ds(base, 8), :] * h + u_ref[pl.ds(base, 8), :]
        hs_ref[pl.ds(base, 8), :] = hg
        return hg[7:8, :]

    h = lax.fori_loop(0, tc // 8, step, hc_ref[0:1, :])
    hc_ref[0:1, :] = h
    ya_ref[0] = (hs_ref[...] * jax.nn.gelu(ga_ref[0])).astype(ya_ref.dtype)

    @pl.when(t == pl.num_programs(1) - 1)
    def _():
        hl_ref[0] = hs_ref[t_valid_last - 1:t_valid_last, :]
        lo = tail - (CONV_WIDTH - 1) + t_valid_last
        cn_ref[0] = xs_ref[lo:lo + CONV_WIDTH - 1, :]

    xs_ref[0:tail, :] = xs_ref[tc:tc + tail, :]


def _lru(ua, conv0, h0, cw, cb, wa, ba, wx, bx, lam, *, tc, t_valid_last):
    B, T, da2 = ua.shape
    da = da2 // 2
    vec = lambda: pl.BlockSpec((1, da), lambda b, t: (0, 0))
    blk = da // N_A_BLOCKS
    return pl.pallas_call(
        functools.partial(_lru_body, tc=tc, t_valid_last=t_valid_last),
        grid=(B, T // tc),
        in_specs=[
            pl.BlockSpec((1, tc, da), lambda b, t: (b, t, 0)),
            pl.BlockSpec((1, tc, da), lambda b, t: (b, t, 1)),
            pl.BlockSpec((1, 8, da), lambda b, t: (b, 0, 0)),
            pl.BlockSpec((1, 1, da), lambda b, t: (b, 0, 0)),
            pl.BlockSpec((CONV_WIDTH, da), lambda b, t: (0, 0)),
            vec(),
            pl.BlockSpec((N_A_BLOCKS, blk, blk), lambda b, t: (0, 0, 0)),
            vec(),
            pl.BlockSpec((N_A_BLOCKS, blk, blk), lambda b, t: (0, 0, 0)),
            vec(),
            vec(),
        ],
        out_specs=[
            pl.BlockSpec((1, tc, da), lambda b, t: (b, t, 0)),
            pl.BlockSpec((1, 1, da), lambda b, t: (b, 0, 0)),
            pl.BlockSpec((1, CONV_WIDTH - 1, da), lambda b, t: (b, 0, 0)),
        ],
        out_shape=[
            jax.ShapeDtypeStruct((B, T, da), BF16),
            jax.ShapeDtypeStruct((B, 1, da), F32),
            jax.ShapeDtypeStruct((B, CONV_WIDTH - 1, da), F32),
        ],
        scratch_shapes=[
            pltpu.VMEM((tc + 8, da), F32),
            pltpu.VMEM((tc, da), F32),
            pltpu.VMEM((tc, da), F32),
            pltpu.VMEM((tc, da), F32),
            pltpu.VMEM((8, da), F32),
        ],
        compiler_params=_cparams(("arbitrary", "arbitrary")),
        name="rglru",
    )(ua, ua, conv0, h0, cw, cb.reshape(1, da), wa, ba.reshape(1, da), wx, bx.reshape(1, da), lam.reshape(1, da))


def _ret_body(q_ref, k_ref, v_ref, gb_ref, cos_ref, sin_ref, s0_ref, yb_ref, sl_ref, s_ref, *, cp, c_valid, mm_dtype):
    t = pl.program_id(1)
    dk = q_ref.shape[-1] // N_RET_HEADS
    dv = v_ref.shape[-1] // N_RET_HEADS

    @pl.when(t == 0)
    def _():
        s_ref[...] = s0_ref[0]

    cos = cos_ref[...]
    sin = sin_ref[...]
    first_half = (lax.broadcasted_iota(I32, cos.shape, 1) % dk) < (dk // 2)
    width = cos.shape[1]

    def rot(x):
        partner = jnp.where(first_half, pltpu.roll(x, width - dk // 2, 1), pltpu.roll(x, dk // 2, 1))
        return x * cos + partner * sin

    row = lax.broadcasted_iota(I32, (cp, 1), 0)
    q_all = rot(q_ref[0])
    k_all = rot(k_ref[0]) * (dk ** -0.5)
    ri = lax.broadcasted_iota(I32, (cp, cp), 0)
    ci = lax.broadcasted_iota(I32, (cp, cp), 1)
    diff = (ri - ci).astype(F32)
    rowf = row.astype(F32)
    for h in range(N_RET_HEADS):
        lg = math.log1p(-(2.0 ** (-5.0 - h)))
        dmask = jnp.where(diff >= 0, jnp.exp(jnp.maximum(diff, 0.0) * lg), 0.0)
        cross_dec = jnp.exp((rowf + 1.0) * lg)
        state_dec = jnp.exp((c_valid - 1.0 - rowf) * lg)
        chunk_dec = math.exp(c_valid * lg)
        state = s_ref[h]
        for n in range(q_all.shape[0] // cp):
            rows = slice(n * cp, (n + 1) * cp)
            qh = q_all[rows, h * dk:(h + 1) * dk].astype(mm_dtype)
            kh = k_all[rows, h * dk:(h + 1) * dk]
            if c_valid < cp:
                kh = jnp.where(row < c_valid, kh, 0.0)
            vh = v_ref[0, rows, h * dv:(h + 1) * dv].astype(mm_dtype)
            att = _dot_nt(qh, kh.astype(mm_dtype)) * dmask
            inner = _dot(att.astype(mm_dtype), vh)
            cross = _dot(qh, state.astype(mm_dtype)) * cross_dec
            state = state * chunk_dec + _dot_tn((kh * state_dec).astype(mm_dtype), vh)
            o = inner + cross
            o = o * lax.rsqrt(jnp.mean(o * o, axis=-1, keepdims=True) + EPS)
            gh = gb_ref[0, rows, h * dv:(h + 1) * dv]
            yb_ref[0, rows, h * dv:(h + 1) * dv] = (o * (gh * jax.nn.sigmoid(gh))).astype(yb_ref.dtype)
        s_ref[h] = state

    @pl.when(t == pl.num_programs(1) - 1)
    def _():
        sl_ref[0] = s_ref[...]


def _retention(ub, cos, sin, s0, *, cp, c_valid, mm_dtype, chunks_per_step=1):
    B, T, wtot = ub.shape
    w = wtot // 6
    H, dk, dv = s0.shape[1:]
    blk = cp * chunks_per_step
    return pl.pallas_call(
        functools.partial(_ret_body, cp=cp, c_valid=c_valid, mm_dtype=mm_dtype),
        grid=(B, T // blk),
        in_specs=[
            pl.BlockSpec((1, blk, w), lambda b, t: (b, t, 0)),
            pl.BlockSpec((1, blk, w), lambda b, t: (b, t, 1)),
            pl.BlockSpec((1, blk, 2 * w), lambda b, t: (b, t, 1)),
            pl.BlockSpec((1, blk, 2 * w), lambda b, t: (b, t, 2)),
            pl.BlockSpec((blk, w), lambda b, t: (t, 0)),
            pl.BlockSpec((blk, w), lambda b, t: (t, 0)),
            pl.BlockSpec((1, H, dk, dv), lambda b, t: (b, 0, 0, 0)),
        ],
        out_specs=[
            pl.BlockSpec((1, blk, 2 * w), lambda b, t: (b, t, 0)),
            pl.BlockSpec((1, H, dk, dv), lambda b, t: (b, 0, 0, 0)),
        ],
        out_shape=[
            jax.ShapeDtypeStruct((B, T, 2 * w), BF16),
            jax.ShapeDtypeStruct((B, H, dk, dv), F32),
        ],
        scratch_shapes=[pltpu.VMEM((H, dk, dv), F32)],
        compiler_params=_cparams(("arbitrary", "arbitrary")),
        name="retention",
    )(ub, ub, ub, ub, cos, sin, s0)


def _rope_tables(pos, dk, heads):
    half = dk // 2
    freqs = ROPE_BASE ** (-jnp.arange(half, dtype=F32) / half)
    ang = pos.astype(F32)[:, None] * freqs[None, :]
    cos, sin = jnp.cos(ang), jnp.sin(ang)
    cos_t = jnp.tile(jnp.concatenate([cos, cos], axis=1), (1, heads))
    sin_t = jnp.tile(jnp.concatenate([-sin, sin], axis=1), (1, heads))
    return cos_t, sin_t


def _t5_bucket_np(dist):
    n = np.maximum(dist, 0)
    max_exact = NUM_BUCKETS // 2
    ratio = np.log(np.maximum(n, 1).astype(np.float32) / np.float32(max_exact)) / np.float32(math.log(MAX_DISTANCE / max_exact))
    large = max_exact + (ratio * np.float32(NUM_BUCKETS - max_exact)).astype(np.int32)
    large = np.minimum(large, NUM_BUCKETS - 1)
    return np.where(n < max_exact, n, large).astype(np.int32)


def _bias_from_buckets(bucket, rb_ref, head):
    def step(b, out):
        return jnp.where(bucket == b, rb_ref[b, head], out)

    return lax.fori_loop(0, NUM_BUCKETS, step, jnp.zeros(bucket.shape, F32))


def _dsa_prompt_body(rb_ref, qc_ref, qi_ref, wi_ref, kvb_ref, btab_ref, o_ref,
                     keys_ref, bt_ref, wf_ref, qis_ref, qs_ref, mx_ref, mrep_ref, acc_ref, cut_ref, kmax_ref, sel_ref,
                     *, tq, n_keep, idx_bits):
    i = pl.program_id(0)
    G = qc_ref.shape[1] // (N_KV_HEADS * HEAD_DIM_C)
    n_heads = N_KV_HEADS * G
    kcol, vcol, icol = 0, N_KV_HEADS * HEAD_DIM_C, 2 * N_KV_HEADS * HEAD_DIM_C
    lanes = HEAD_DIM_C
    halves = tq // lanes
    n_tiles = i + 1

    @pl.when(i == 0)
    def _():
        for h in range(n_heads):
            far = rb_ref[NUM_BUCKETS - 1, h]
            for r in range(2):
                bt_ref[h // G, r, h % G] = _bias_from_buckets(btab_ref[r], rb_ref, h) - far

    for h in range(N_IDX_HEADS):
        qis_ref[h] = qi_ref[:, h * IDX_DIM:(h + 1) * IDX_DIM]
        wf_ref[h] = jnp.broadcast_to(wi_ref[:, IDX_DIM + h:IDX_DIM + h + 1], (tq, lanes))
    for c in range(N_KV_HEADS):
        for g in range(G):
            h = c * G + g
            qs_ref[c, g * tq:(g + 1) * tq, :] = qc_ref[:, h * HEAD_DIM_C:(h + 1) * HEAD_DIM_C]

    rowi = lax.broadcasted_iota(I32, (tq, tq), 0)
    coli = lax.broadcasted_iota(I32, (tq, tq), 1)

    def key_index(j):
        return j * tq + coli

    def visible(j):
        return key_index(j) <= (i * tq + rowi)

    def score_tile(j, carry):
        ki = kvb_ref[pl.ds(pl.multiple_of(j * tq, tq), tq), icol:icol + IDX_DIM]
        for rh in range(halves):
            rows = slice(rh * lanes, (rh + 1) * lanes)
            acc = jnp.zeros((lanes, tq), F32)
            for h in range(N_IDX_HEADS):
                w = wf_ref[h, rows, :]
                acc = acc + jnp.maximum(_dot_nt(qis_ref[h, rows, :], ki), 0.0) * jnp.concatenate([w] * halves, axis=1)
            score = acc * (N_IDX_HEADS ** -0.5 * IDX_DIM ** -0.5)
            score = jnp.where(visible(j)[rows], score, -jnp.inf)
            keys_ref[j, rows, :] = _sort_key(score)
        return carry

    lax.fori_loop(0, n_tiles, score_tile, 0)

    ones_sq = jnp.ones((lanes, lanes), BF16)

    def count(pred, *row_args):
        totals = []
        for rh in range(halves):
            rows = slice(rh * lanes, (rh + 1) * lanes)
            args = [a[rows] for a in row_args]

            def body(j, cnt):
                for a in range(halves):
                    cols = slice(a * lanes, (a + 1) * lanes)
                    idx = j * tq + a * lanes + lax.broadcasted_iota(I32, (lanes, lanes), 1)
                    cnt = cnt + jnp.where(pred(keys_ref[j, rows, cols], idx, *args), 1, 0)
                return cnt

            totals.append(lax.fori_loop(0, n_tiles, body, jnp.zeros((lanes, lanes), I32)))
        return _dot(jnp.concatenate(totals, axis=0).astype(F32).astype(BF16), ones_sq).astype(I32)

    def bit_step(b, st):
        tau, n_ge = st
        cand = tau + lax.shift_left(jnp.int32(1), 31 - b)
        c = count(lambda k, idx, cnd: k >= cnd, cand)
        ge = c >= n_keep
        return jnp.where(ge, cand, tau), jnp.where(ge, c, n_ge)

    sel_ref[0] = jnp.full((tq, lanes), INT_MIN, I32)
    sel_ref[1] = jnp.broadcast_to(n_tiles * tq, (tq, lanes)).astype(I32)
    for b0, b1 in zip(SELECT_BIT_GROUPS[:-1], SELECT_BIT_GROUPS[1:]):
        @pl.when(jnp.max(sel_ref[1]) > n_keep)
        def _():
            tau_g, n_ge_g = lax.fori_loop(b0, b1, bit_step, (sel_ref[0], sel_ref[1]))
            sel_ref[0] = tau_g
            sel_ref[1] = n_ge_g

    tau, n_ge = sel_ref[0], sel_ref[1]

    cut_ref[...] = jnp.full((tq, lanes), 2 ** 31 - 1, I32)

    @pl.when(jnp.max(n_ge) > n_keep)
    def _():
        need = n_keep - count(lambda k, idx, t: k > t, tau)

        def idx_step(b, cut):
            cand = cut + lax.shift_left(jnp.int32(1), idx_bits - 1 - b)
            below = count(lambda k, idx, t, c: (k == t) & (idx < c), tau, cand)
            return jnp.where(below < need, cand, cut)

        cut_ref[...] = lax.fori_loop(0, idx_bits, idx_step, jnp.zeros((tq, lanes), I32))

    tau_t = jnp.concatenate([tau] * halves, axis=1)
    cut_t = jnp.concatenate([cut_ref[...]] * halves, axis=1)

    def mask_tile(j, carry):
        k = keys_ref[j]
        sel = ((k > tau_t) | ((k == tau_t) & (key_index(j) <= cut_t))) & visible(j)
        keys_ref[j] = lax.bitcast_convert_type(jnp.where(sel, 0.0, NEG_BIG), I32)
        return carry

    lax.fori_loop(0, n_tiles, mask_tile, 0)

    def logits(j, c, r):
        base = pl.multiple_of(j * tq, tq)
        kc = kvb_ref[pl.ds(base, tq), kcol + c * HEAD_DIM_C:kcol + (c + 1) * HEAD_DIM_C]
        s = _dot_nt(qs_ref[c], kc).reshape(G, tq, tq) + lax.bitcast_convert_type(keys_ref[j], F32)[None]
        if r is not None:
            s = s + bt_ref[c, r]
        return s

    def max_tile(j, r):
        for c in range(N_KV_HEADS):
            s = logits(j, c, r)
            m = s[..., 0:lanes]
            for a in range(1, halves):
                m = jnp.maximum(m, s[..., a * lanes:(a + 1) * lanes])
            mx_ref[c] = jnp.maximum(mx_ref[c], m)

    def acc_tile(j, r):
        base = pl.multiple_of(j * tq, tq)
        for c in range(N_KV_HEADS):
            vc = kvb_ref[pl.ds(base, tq), vcol + c * HEAD_DIM_C:vcol + (c + 1) * HEAD_DIM_C]
            vext = jnp.concatenate([vc, jnp.ones((tq, lanes), BF16)], axis=1)
            m = mrep_ref[c]
            p = jnp.exp(logits(j, c, r) - jnp.concatenate([m] * halves, axis=-1))
            acc_ref[c] += _dot(p.reshape(G * tq, tq).astype(BF16), vext)

    def sweep(tile_fn):
        def far(j, carry):
            tile_fn(j, None)
            return carry

        lax.fori_loop(0, jnp.maximum(i - 1, 0), far, 0)

        @pl.when(i >= 1)
        def _():
            tile_fn(i - 1, 1)

        tile_fn(i, 0)

    @pl.when(i == 0)
    def _():
        kmax_ref[...] = jnp.zeros(kmax_ref.shape, F32)

    for c in range(N_KV_HEADS):
        kt = kvb_ref[pl.ds(pl.multiple_of(i * tq, tq), tq), kcol + c * HEAD_DIM_C:kcol + (c + 1) * HEAD_DIM_C].astype(F32)
        knorm = jnp.sqrt(jnp.max(jnp.sum(kt * kt, axis=1, keepdims=True)))
        kmax_ref[c] = jnp.maximum(kmax_ref[c], knorm)
        qf = qs_ref[c].astype(F32)
        qnorm = jnp.sqrt(jnp.sum(qf * qf, axis=1, keepdims=True)).reshape(G, tq, 1)
        for g in range(G):
            h = c * G + g
            far = rb_ref[NUM_BUCKETS - 1, h]
            bias_max = lax.fori_loop(0, NUM_BUCKETS, lambda b, m: jnp.maximum(m, rb_ref[b, h] - far), jnp.float32(0.0))
            mrep_ref[c, g] = qnorm[g] * kmax_ref[c, 0:1, :] + bias_max

    acc_ref[...] = jnp.zeros(acc_ref.shape, F32)
    sweep(acc_tile)

    @pl.when(jnp.logical_not(jnp.min(acc_ref[:, :, lanes:]) > 1e-30))
    def _():
        mx_ref[...] = jnp.full(mx_ref.shape, NEG_BIG, F32)
        sweep(max_tile)
        mrep_ref[...] = jnp.broadcast_to(jnp.max(mx_ref[...], axis=-1, keepdims=True), mrep_ref.shape)
        acc_ref[...] = jnp.zeros(acc_ref.shape, F32)
        sweep(acc_tile)

    for c in range(N_KV_HEADS):
        for g in range(G):
            h = c * G + g
            a = acc_ref[c, g * tq:(g + 1) * tq, :]
            o_ref[:, h * HEAD_DIM_C:(h + 1) * HEAD_DIM_C] = (a[:, :lanes] / a[:, lanes:]).astype(o_ref.dtype)


def _dsa_prompt(qcqi, tail, kvi_bf16, rel_bias, *, tq, n_keep):
    S = qcqi.shape[0]
    dc = qcqi.shape[1] - N_IDX_HEADS * IDX_DIM
    n_heads = dc // HEAD_DIM_C
    G = n_heads // N_KV_HEADS
    wkv = kvi_bf16.shape[1]
    d = np.arange(tq)[:, None] - np.arange(tq)[None, :]
    btab = jnp.asarray(np.stack([_t5_bucket_np(d + r * tq) for r in range(2)]))
    assert _t5_bucket_np(np.array([tq + 1]))[0] == NUM_BUCKETS - 1
    lanes = HEAD_DIM_C
    grid_spec = pltpu.PrefetchScalarGridSpec(
        num_scalar_prefetch=0,
        grid=(S // tq,),
        in_specs=[
            pl.BlockSpec(memory_space=pltpu.SMEM),
            pl.BlockSpec((tq, dc), lambda i: (i, 0)),
            pl.BlockSpec((tq, N_IDX_HEADS * IDX_DIM), lambda i: (i, dc // (N_IDX_HEADS * IDX_DIM))),
            pl.BlockSpec((tq, tail.shape[1]), lambda i: (i, 0)),
            pl.BlockSpec((S, wkv), lambda i: (0, 0), pipeline_mode=pl.Buffered(1)),
            pl.BlockSpec((2, tq, tq), lambda i: (0, 0, 0)),
        ],
        out_specs=pl.BlockSpec((tq, dc), lambda i: (i, 0)),
        scratch_shapes=[
            pltpu.VMEM((S // tq, tq, tq), I32),
            pltpu.VMEM((N_KV_HEADS, 2, G, tq, tq), F32),
            pltpu.VMEM((N_IDX_HEADS, tq, lanes), F32),
            pltpu.VMEM((N_IDX_HEADS, tq, IDX_DIM), BF16),
            pltpu.VMEM((N_KV_HEADS, G * tq, HEAD_DIM_C), BF16),
            pltpu.VMEM((N_KV_HEADS, G, tq, lanes), F32),
            pltpu.VMEM((N_KV_HEADS, G, tq, lanes), F32),
            pltpu.VMEM((N_KV_HEADS, G * tq, 2 * lanes), F32),
            pltpu.VMEM((tq, lanes), I32),
            pltpu.VMEM((N_KV_HEADS, 8, lanes), F32),
            pltpu.VMEM((2, tq, lanes), I32),
        ],
    )
    return pl.pallas_call(
        functools.partial(_dsa_prompt_body, tq=tq, n_keep=n_keep, idx_bits=(S - 1).bit_length()),
        grid_spec=grid_spec,
        out_shape=jax.ShapeDtypeStruct((S, dc), BF16),
        compiler_params=_cparams(("arbitrary",)),
        name="dsa_prompt",
    )(rel_bias, qcqi, qcqi, tail, kvi_bf16, btab)


def _dsa_sel_body(pt_ref, qi_ref, w_ref, kin_ref, *rest, t_new, n_keep):
    pages = rest[:PAGES_PER_STEP]
    mb_ref, keys_ref, arg_ref, cnt_ref = rest[PAGES_PER_STEP:]
    b = pl.program_id(0)
    p = pl.program_id(1)
    B, n_chunks, tp, P = keys_ref.shape
    last = p == pl.num_programs(1) - 1
    q = qi_ref[0]
    w = w_ref[0]
    rowi = lax.broadcasted_iota(I32, (tp, P), 0)
    coli = lax.broadcasted_iota(I32, (tp, P), 1)
    vis_new = (coli <= rowi) & (coli < t_new)

    def chunk_scores(s):
        s = jnp.maximum(s, 0.0) * w
        s = s.reshape(tp, N_IDX_HEADS, s.shape[-1]).sum(axis=1)
        return s * (N_IDX_HEADS ** -0.5 * IDX_DIM ** -0.5)

    for r in range(PAGES_PER_STEP):
        keys_ref[b, p * PAGES_PER_STEP + r] = _sort_key(chunk_scores(_dot(q, pages[r][0].astype(BF16))))

    @pl.when(last)
    def _():
        s_new = jnp.where(vis_new, chunk_scores(_dot_nt(q, kin_ref[0])), -jnp.inf)
        keys_ref[b, n_chunks - 1] = _sort_key(s_new)

    @pl.when(last & (b == B - 1))
    def _():
        key_index = (lax.broadcasted_iota(I32, (n_chunks, tp, P), 0) * P
                     + lax.broadcasted_iota(I32, (n_chunks, tp, P), 2))

        def count(pred, *row_args):
            for n, a in enumerate(row_args):
                arg_ref[n] = a

            def per_seq(s, carry):
                args = [arg_ref[n, s][None] for n in range(len(row_args))]
                cnt_ref[s] = jnp.sum(jnp.where(pred(keys_ref[s], key_index, *args), 1, 0), axis=0)
                return carry

            lax.fori_loop(0, B, per_seq, 0)
            return jnp.broadcast_to(jnp.sum(cnt_ref[...], axis=-1, keepdims=True), (B, tp, P))

        def bit_step(bit, tau):
            cand = tau + lax.shift_left(jnp.int32(1), 31 - bit)
            return jnp.where(count(lambda k, idx, c: k >= c, cand) >= n_keep, cand, tau)

        tau = lax.fori_loop(0, 32, bit_step, jnp.full((B, tp, P), INT_MIN, I32))

        arg_ref[2] = jnp.full((B, tp, P), 2 ** 31 - 1, I32)

        @pl.when(jnp.max(count(lambda k, idx, t: k >= t, tau)) > n_keep)
        def _():
            need = n_keep - count(lambda k, idx, t: k > t, tau)
            idx_bits = (n_chunks * P - 1).bit_length()

            def idx_step(bit, cut):
                cand = cut + lax.shift_left(jnp.int32(1), idx_bits - 1 - bit)
                below = count(lambda k, idx, t, c: (k == t) & (idx < c), tau, cand)
                return jnp.where(below < need, cand, cut)

            arg_ref[2] = lax.fori_loop(0, idx_bits, idx_step, jnp.zeros((B, tp, P), I32))

        arg_ref[0] = tau

        def write_mask(s, carry):
            k = keys_ref[s]
            t = arg_ref[0, s][None]
            sel = (k > t) | ((k == t) & (key_index <= arg_ref[2, s][None]))
            mb_ref[s] = jnp.where(sel, 0.0, NEG_BIG)
            mb_ref[s, n_chunks - 1] = jnp.where(sel[n_chunks - 1] & vis_new, 0.0, NEG_BIG)
            return carry

        lax.fori_loop(0, B, write_mask, 0)


def _dsa_sample_select(pt_flat, qi, w, ki_new, pool_ki, *, n_pages, t_new, n_keep):
    B = qi.shape[0]
    P = pool_ki.shape[2]
    steps = n_pages // PAGES_PER_STEP

    def page_spec(r):
        return pl.BlockSpec((1, IDX_DIM, P), lambda b, p, pt: (pt[b * n_pages + p * PAGES_PER_STEP + r], 0, 0))

    grid_spec = pltpu.PrefetchScalarGridSpec(
        num_scalar_prefetch=1,
        grid=(B, steps),
        in_specs=[
            pl.BlockSpec((1,) + qi.shape[1:], lambda b, p, pt: (b, 0, 0)),
            pl.BlockSpec((1,) + w.shape[1:], lambda b, p, pt: (b, 0, 0)),
            pl.BlockSpec((1,) + ki_new.shape[1:], lambda b, p, pt: (b, 0, 0)),
        ] + [page_spec(r) for r in range(PAGES_PER_STEP)],
        out_specs=pl.BlockSpec((B, n_pages + 1, SAMPLE_PAD_T, P), lambda b, p, pt: (0, 0, 0, 0)),
        scratch_shapes=[
            pltpu.VMEM((B, n_pages + 1, SAMPLE_PAD_T, P), I32),
            pltpu.VMEM((3, B, SAMPLE_PAD_T, P), I32),
            pltpu.VMEM((B, SAMPLE_PAD_T, P), I32),
        ],
    )
    return pl.pallas_call(
        functools.partial(_dsa_sel_body, t_new=t_new, n_keep=n_keep),
        grid_spec=grid_spec,
        out_shape=jax.ShapeDtypeStruct((B, n_pages + 1, SAMPLE_PAD_T, P), F32),
        compiler_params=_cparams(("arbitrary", "arbitrary")),
        name="dsa_sample_select",
    )(pt_flat, qi, w, ki_new, *([pool_ki] * PAGES_PER_STEP))


def _dsa_att_body(pt_ref, rb_ref, q_ref, mb_ref, kn_ref, vn_ref, btab_ref, *rest):
    kp = rest[:PAGES_PER_STEP]
    vp = rest[PAGES_PER_STEP:2 * PAGES_PER_STEP]
    o_ref, bt_ref, m_ref, acc_ref = rest[2 * PAGES_PER_STEP:]
    b = pl.program_id(0)
    p = pl.program_id(1)
    n_chunks = mb_ref.shape[1]
    tp = SAMPLE_PAD_T
    G = q_ref.shape[2] // tp
    rows = G * tp
    P = kn_ref.shape[1]
    lanes = HEAD_DIM_C
    last = p == pl.num_programs(1) - 1

    @pl.when((b == 0) & (p == 0))
    def _():
        for c in range(N_KV_HEADS):
            for kind in range(2):
                for g in range(G):
                    h = c * G + g
                    bt_ref[c, kind, g * tp:(g + 1) * tp, :] = (_bias_from_buckets(btab_ref[kind], rb_ref, h)
                                                              - rb_ref[NUM_BUCKETS - 1, h])

    @pl.when(p == 0)
    def _():
        m_ref[...] = jnp.full(m_ref.shape, NEG_BIG, F32)
        acc_ref[...] = jnp.zeros(acc_ref.shape, F32)

    def attend(c, s, vext):
        m_old = m_ref[c]
        m_new = jnp.maximum(m_old, jnp.max(s, axis=1, keepdims=True))
        alpha = jnp.exp(m_old - m_new)
        pr = jnp.exp(s - jnp.concatenate([m_new] * (s.shape[1] // lanes), axis=1))
        acc_ref[c] = jnp.concatenate([alpha, alpha], axis=1) * acc_ref[c] + _dot(pr.astype(BF16), vext)
        m_ref[c] = m_new

    ones = jnp.ones((P, lanes), BF16)
    mb_step = jnp.concatenate([mb_ref[0, p * PAGES_PER_STEP + r] for r in range(PAGES_PER_STEP)], axis=1)
    mb_step = jnp.concatenate([mb_step] * G, axis=0)
    for c in range(N_KV_HEADS):
        head_rows = pl.ds(c, P, stride=N_KV_HEADS)
        k_all = jnp.concatenate([kp[r][0, head_rows, :].astype(BF16) for r in range(PAGES_PER_STEP)], axis=0)
        v_all = jnp.concatenate(
            [jnp.concatenate([vp[r][0, head_rows, :].astype(BF16), ones], axis=1) for r in range(PAGES_PER_STEP)], axis=0)
        s = _dot_nt(q_ref[0, c], k_all) + mb_step
        s = jnp.concatenate([s[:, :-P], s[:, -P:] + jnp.where(last, bt_ref[c, 0], 0.0)], axis=1)
        attend(c, s, v_all)

    @pl.when(last)
    def _():
        mb_new = jnp.concatenate([mb_ref[0, n_chunks - 1]] * G, axis=0)
        for c in range(N_KV_HEADS):
            cols = slice(c * HEAD_DIM_C, (c + 1) * HEAD_DIM_C)
            s = _dot_nt(q_ref[0, c], kn_ref[0, :, cols]) + mb_new + bt_ref[c, 1]
            attend(c, s, jnp.concatenate([vn_ref[0, :, cols], ones], axis=1))
            a = acc_ref[c]
            o_ref[0, c] = a[:, :lanes] / a[:, lanes:]


def _dsa_sample_attend(pt_flat, rel_bias, q, mb, k_new, v_new, pool_k, pool_v, *, n_pages, past):
    B, kvh, rows, dh = q.shape
    P = pool_k.shape[1] // kvh
    steps = n_pages // PAGES_PER_STEP
    tp = SAMPLE_PAD_T
    t = np.arange(tp)[:, None]
    col = np.arange(P)[None, :]
    assert past - (n_pages - 1) * P >= MAX_DISTANCE
    last_page = _t5_bucket_np(past + t - ((n_pages - 1) * P + col))
    new = _t5_bucket_np(t - col)
    btab = jnp.asarray(np.stack([last_page, new]))

    def page_spec(r):
        return pl.BlockSpec((1, P * kvh, dh), lambda b, p, pt: (pt[b * n_pages + p * PAGES_PER_STEP + r], 0, 0))

    grid_spec = pltpu.PrefetchScalarGridSpec(
        num_scalar_prefetch=1,
        grid=(B, steps),
        in_specs=[
            pl.BlockSpec(memory_space=pltpu.SMEM),
            pl.BlockSpec((1, kvh, rows, dh), lambda b, p, pt: (b, 0, 0, 0)),
            pl.BlockSpec((1,) + mb.shape[1:], lambda b, p, pt: (b, 0, 0, 0)),
            pl.BlockSpec((1,) + k_new.shape[1:], lambda b, p, pt: (b, 0, 0)),
            pl.BlockSpec((1,) + v_new.shape[1:], lambda b, p, pt: (b, 0, 0)),
            pl.BlockSpec((2, tp, P), lambda b, p, pt: (0, 0, 0)),
        ] + [page_spec(r) for r in range(PAGES_PER_STEP)] * 2,
        out_specs=pl.BlockSpec((1, kvh, rows, dh), lambda b, p, pt: (b, 0, 0, 0)),
        scratch_shapes=[
            pltpu.VMEM((kvh, 2, rows, P), F32),
            pltpu.VMEM((kvh, rows, dh), F32),
            pltpu.VMEM((kvh, rows, 2 * dh), F32),
        ],
    )
    return pl.pallas_call(
        _dsa_att_body,
        grid_spec=grid_spec,
        out_shape=jax.ShapeDtypeStruct((B, kvh, rows, dh), F32),
        compiler_params=_cparams(("arbitrary", "arbitrary")),
        name="dsa_sample_attend",
    )(pt_flat, rel_bias, q, mb, k_new, v_new, btab, *([pool_k] * PAGES_PER_STEP), *([pool_v] * PAGES_PER_STEP))


def _dsa_sample(c1, tail, c2b, pt_flat, rel_bias, pool_k, pool_v, pool_ki, *, Bs, Ts, n_pages):
    tp = SAMPLE_PAD_T
    kv_w = N_KV_HEADS * HEAD_DIM_C
    d_c = c1.shape[1] - N_IDX_HEADS * IDX_DIM
    G = d_c // kv_w
    P = pool_ki.shape[2]
    past = n_pages * P
    qi_s = c1[:, d_c:].reshape(Bs, tp * N_IDX_HEADS, IDX_DIM)
    w_s = tail[:, IDX_DIM:IDX_DIM + N_IDX_HEADS].reshape(Bs, tp * N_IDX_HEADS, 1)
    new_rows = jnp.pad(c2b.reshape(Bs, tp, -1), ((0, 0), (0, P - tp), (0, 0)))
    mb = _dsa_sample_select(pt_flat, qi_s, w_s, new_rows[:, :, 2 * kv_w:2 * kv_w + IDX_DIM], pool_ki,
                            n_pages=n_pages, t_new=Ts, n_keep=min(TOPK_MAX, (past + Ts) // 4))
    q_s = c1[:, :d_c].reshape(Bs, tp, N_KV_HEADS, G, HEAD_DIM_C).transpose(0, 2, 3, 1, 4)
    q_s = q_s.reshape(Bs, N_KV_HEADS, G * tp, HEAD_DIM_C)
    o_s = _dsa_sample_attend(pt_flat, rel_bias, q_s, mb, new_rows[:, :, :kv_w], new_rows[:, :, kv_w:2 * kv_w],
                             pool_k, pool_v, n_pages=n_pages, past=past)
    return o_s.reshape(Bs, N_KV_HEADS, G, tp, HEAD_DIM_C).transpose(0, 3, 1, 2, 4).reshape(Bs * tp, d_c).astype(BF16)


def kernel(x_prompt, x_sample, cache_k, cache_v, cache_kidx, page_table, state_lru_h, state_conv, state_ret,
           norm_ffn1, ffn1_gate, ffn1_up, ffn1_down, norm_mix, w_in, w_out, conv_w, conv_b,
           lru_wa, lru_ba, lru_wx, lru_bx, lru_lambda, rel_bias,
           norm_ffn2, ffn2_gate, ffn2_up, ffn2_down, norm_final):
    depth = norm_ffn1.shape[0]
    _, S, D = x_prompt.shape
    Bs, Ts, _ = x_sample.shape
    n_pool, P = cache_k.shape[1], cache_k.shape[2]
    n_pages = page_table.shape[1]
    past = n_pages * P
    d_a = state_lru_h.shape[-1]
    H_r, rdk, rdv = state_ret.shape[2:]
    d_b = H_r * rdv
    d_c = D - d_a - d_b
    kv_w = N_KV_HEADS * HEAD_DIM_C
    qi_w = N_IDX_HEADS * IDX_DIM
    G = d_c // HEAD_DIM_C // N_KV_HEADS
    tp = SAMPLE_PAD_T
    Rs = Bs * tp

    xp = x_prompt.reshape(S, D)
    xs = jnp.pad(x_sample, ((0, 0), (0, tp - Ts), (0, 0))).reshape(Rs, D)

    cos_p, sin_p = _rope_tables(jnp.arange(S, dtype=I32), rdk, H_r)
    cos_s, sin_s = _rope_tables(past + jnp.arange(tp, dtype=I32), rdk, H_r)

    pool_k = cache_k.reshape(depth * n_pool, P * N_KV_HEADS, HEAD_DIM_C)
    pool_v = cache_v.reshape(depth * n_pool, P * N_KV_HEADS, HEAD_DIM_C)
    pool_ki = jnp.swapaxes(cache_kidx, 2, 3).reshape(depth * n_pool, IDX_DIM, P)

    o = np.cumsum([0, 2 * d_a, 2 * H_r * rdk + 2 * d_b, d_c, 2 * kv_w, qi_w, IDX_DIM + N_IDX_HEADS])
    proj_cols = tuple(int(c) for c in o[:6])
    w_in_b = w_in.astype(BF16)
    w_in_tail = jnp.pad(w_in_b[:, :, o[5]:o[6]], ((0, 0), (0, 0), (0, 128 - int(o[6] - o[5]))))

    zeros_conv = jnp.zeros((1, 8, d_a), F32)
    zeros_h = jnp.zeros((1, 1, d_a), F32)
    zeros_s = jnp.zeros((1, H_r, rdk, rdv), F32)

    outs_p, outs_s = [], []
    y_prompt = y_sample = None
    for l in range(depth):
        in_proj = functools.partial(_in_proj, g=norm_mix[l], w_all=w_in_b, w_tail=w_in_tail, layer=l,
                                    cols=proj_cols, q_scale=HEAD_DIM_C ** -0.5)
        wo = w_out[l].astype(BF16)
        lwa, lwx = lru_wa[l].astype(BF16), lru_wx[l].astype(BF16)
        last = l == depth - 1
        pt_flat = (page_table + l * n_pool).reshape(-1).astype(I32)

        xs, *f1 = _ffn_cast(xs, norm_ffn1[l], ffn1_gate, ffn1_up, ffn1_down, l, tf=512)
        ua, ub, c1, k_new, v_new, ki_new, tail, c2b = in_proj(xs, tm=Rs)
        conv0 = jnp.pad(state_conv[l], ((0, 0), (8 - (CONV_WIDTH - 1), 0), (0, 0)))
        ya, h_last, conv_new = _lru(ua.reshape(Bs, tp, 2 * d_a), conv0, state_lru_h[l].reshape(Bs, 1, d_a),
                                    conv_w[l], conv_b[l], lwa, lru_ba[l], lwx, lru_bx[l], lru_lambda[l],
                                    tc=tp, t_valid_last=Ts)
        yb, s_last = _retention(ub.reshape(Bs, tp, -1), cos_s, sin_s, state_ret[l], cp=tp, c_valid=Ts, mm_dtype=F32)
        yc = _dsa_sample(c1, tail, c2b, pt_flat, rel_bias, pool_k, pool_v, pool_ki, Bs=Bs, Ts=Ts, n_pages=n_pages)
        xs = _out_proj(xs, ya.reshape(Rs, d_a), yb.reshape(Rs, d_b), yc, wo, tm=Rs)
        if last:
            xs, y_sample, *f2 = _ffn_cast(xs, norm_ffn2[l], ffn2_gate, ffn2_up, ffn2_down, l, norm_final, tf=512)
        else:
            xs, *f2 = _ffn_cast(xs, norm_ffn2[l], ffn2_gate, ffn2_up, ffn2_down, l, tf=512)
        valid = lambda a: a.reshape(Bs, tp, -1)[:, :Ts]
        outs_s.append((valid(k_new).reshape(Bs, Ts, N_KV_HEADS, HEAD_DIM_C),
                       valid(v_new).reshape(Bs, Ts, N_KV_HEADS, HEAD_DIM_C),
                       valid(ki_new),
                       h_last.reshape(Bs, d_a), conv_new, s_last))

        xp = _ffn(xp, norm_ffn1[l], *f1, tm=1024, tf=256)
        ua, ub, c1, k_new, v_new, ki_new, tail, c2b = in_proj(xp, tm=256)
        ya, h_last, conv_new = _lru(ua.reshape(1, S, 2 * d_a), zeros_conv, zeros_h, conv_w[l], conv_b[l],
                                    lwa, lru_ba[l], lwx, lru_bx[l], lru_lambda[l], tc=1024, t_valid_last=1024)
        yb, s_last = _retention(ub.reshape(1, S, -1), cos_p, sin_p, zeros_s, cp=128, c_valid=128, mm_dtype=BF16,
                                chunks_per_step=4)
        yc = _dsa_prompt(c1, tail, c2b, rel_bias, tq=256, n_keep=min(TOPK_MAX, S // 4))
        xp = _out_proj(xp, ya.reshape(S, d_a), yb.reshape(S, d_b), yc, wo, tm=512)
        if last:
            xp, y_prompt = _ffn(xp, norm_ffn2[l], *f2, norm_final, tm=512, tf=512)
        else:
            xp = _ffn(xp, norm_ffn2[l], *f2, tm=1024, tf=256)
        outs_p.append((k_new.reshape(1, S, N_KV_HEADS, HEAD_DIM_C),
                       v_new.reshape(1, S, N_KV_HEADS, HEAD_DIM_C),
                       ki_new.reshape(1, S, IDX_DIM),
                       h_last.reshape(1, d_a), conv_new, s_last))

    stack = lambda outs, k: jnp.stack([o[k] for o in outs])
    return ((y_prompt.reshape(1, S, D), y_sample.reshape(Bs, tp, D)[:, :Ts])
            + tuple(stack(outs_p, k) for k in range(6))
            + tuple(stack(outs_s, k) for k in range(6)))
```

```python
import functools
import math

import numpy as np
import jax
import jax.numpy as jnp
from jax import lax
from jax.experimental import pallas as pl
from jax.experimental.pallas import tpu as pltpu

F32 = jnp.float32
BF16 = jnp.bfloat16
I32 = jnp.int32

EPS = 1e-6
N_A_BLOCKS = 4
CONV_WIDTH = 4
LRU_C = 8.0
N_RET_HEADS = 4
ROPE_BASE = 10000.0
HEAD_DIM_C = 128
N_KV_HEADS = 2
N_IDX_HEADS = 16
IDX_DIM = 64
TOPK_MAX = 256
NUM_BUCKETS = 32
MAX_DISTANCE = 128
SAMPLE_PAD_T = 8
PAGES_PER_STEP = 16
NEG_BIG = -1e30
INT_MIN = -2 ** 31
SELECT_BIT_GROUPS = (0, 21, 23, 25, 27, 29, 32)

VMEM_LIMIT = 56 * 1024 * 1024


def _cparams(sem):
    return pltpu.CompilerParams(dimension_semantics=sem, vmem_limit_bytes=VMEM_LIMIT)


def _rms(x, g):
    return x * lax.rsqrt(jnp.mean(x * x, axis=-1, keepdims=True) + EPS) * g


def _dot(a, b):
    return jnp.dot(a, b, preferred_element_type=F32)


def _dot_nt(a, b):
    return lax.dot_general(a, b, (((1,), (1,)), ((), ())), preferred_element_type=F32)


def _dot_tn(a, b):
    return lax.dot_general(a, b, (((0,), (0,)), ((), ())), preferred_element_type=F32)


def _sort_key(score):
    bits = lax.bitcast_convert_type(score, I32)
    return jnp.where(bits < 0, bits ^ jnp.int32(0x7FFFFFFF), bits)


def _ffn_body(x_ref, g_ref, wg_ref, wu_ref, wd_ref, *rest, final_norm):
    if final_norm:
        gf_ref, o_ref, on_ref, h_ref = rest
    else:
        o_ref, h_ref = rest
    j = pl.program_id(1)

    @pl.when(j == 0)
    def _():
        h_ref[...] = _rms(x_ref[...], g_ref[...]).astype(BF16)
        o_ref[...] = jnp.zeros_like(o_ref)

    h = h_ref[...]
    g = _dot(h, wg_ref[...])
    u = _dot(h, wu_ref[...])
    a = (g * jax.nn.sigmoid(g) * u).astype(BF16)
    o_ref[...] += _dot(a, wd_ref[...])

    @pl.when(j == pl.num_programs(1) - 1)
    def _():
        y = x_ref[...] + 0.5 * o_ref[...]
        o_ref[...] = y
        if final_norm:
            on_ref[...] = _rms(y, gf_ref[...])


def _ffn(x, g, wg, wu, wd, gf=None, *, tm, tf):
    R, D = x.shape
    FF = wg.shape[1]
    final_norm = gf is not None
    in_specs = [
        pl.BlockSpec((tm, D), lambda i, j: (i, 0)),
        pl.BlockSpec((1, D), lambda i, j: (0, 0)),
        pl.BlockSpec((D, tf), lambda i, j: (0, j)),
        pl.BlockSpec((D, tf), lambda i, j: (0, j)),
        pl.BlockSpec((tf, D), lambda i, j: (j, 0)),
    ]
    args = [x, g.reshape(1, D), wg, wu, wd]
    out_shape = [jax.ShapeDtypeStruct((R, D), F32)]
    out_specs = [pl.BlockSpec((tm, D), lambda i, j: (i, 0))]
    if final_norm:
        in_specs.append(pl.BlockSpec((1, D), lambda i, j: (0, 0)))
        args.append(gf.reshape(1, D))
        out_shape.append(jax.ShapeDtypeStruct((R, D), F32))
        out_specs.append(pl.BlockSpec((tm, D), lambda i, j: (i, 0)))
    res = pl.pallas_call(
        functools.partial(_ffn_body, final_norm=final_norm),
        grid=(R // tm, FF // tf),
        in_specs=in_specs,
        out_specs=out_specs,
        out_shape=out_shape,
        scratch_shapes=[pltpu.VMEM((tm, D), BF16)],
        compiler_params=_cparams(("parallel", "arbitrary")),
        name="ffn",
    )(*args)
    return res if final_norm else res[0]


def _ffn_cast_body(x_ref, g_ref, wg_ref, wu_ref, wd_ref, *rest, final_norm):
    if final_norm:
        gf_ref, o_ref, on_ref, wgb_ref, wub_ref, wdb_ref, h_ref = rest
    else:
        o_ref, wgb_ref, wub_ref, wdb_ref, h_ref = rest
    j = pl.program_id(0)

    @pl.when(j == 0)
    def _():
        h_ref[...] = _rms(x_ref[...], g_ref[...]).astype(BF16)
        o_ref[...] = jnp.zeros_like(o_ref)

    wg, wu, wd = wg_ref[0].astype(BF16), wu_ref[0].astype(BF16), wd_ref[0].astype(BF16)
    wgb_ref[...] = wg
    wub_ref[...] = wu
    wdb_ref[...] = wd
    h = h_ref[...]
    g = _dot(h, wg)
    a = (g * jax.nn.sigmoid(g) * _dot(h, wu)).astype(BF16)
    o_ref[...] += _dot(a, wd)

    @pl.when(j == pl.num_programs(0) - 1)
    def _():
        y = x_ref[...] + 0.5 * o_ref[...]
        o_ref[...] = y
        if final_norm:
            on_ref[...] = _rms(y, gf_ref[...])


def _ffn_cast(x, g, wg, wu, wd, layer, gf=None, *, tf):
    R, D = x.shape
    FF = wg.shape[2]
    final_norm = gf is not None
    in_specs = [
        pl.BlockSpec((R, D), lambda j: (0, 0)),
        pl.BlockSpec((1, D), lambda j: (0, 0)),
        pl.BlockSpec((1, D, tf), lambda j: (layer, 0, j)),
        pl.BlockSpec((1, D, tf), lambda j: (layer, 0, j)),
        pl.BlockSpec((1, tf, D), lambda j: (layer, j, 0)),
    ]
    args = [x, g.reshape(1, D), wg, wu, wd]
    out_shape = [jax.ShapeDtypeStruct((R, D), F32)]
    out_specs = [pl.BlockSpec((R, D), lambda j: (0, 0))]
    if final_norm:
        in_specs.append(pl.BlockSpec((1, D), lambda j: (0, 0)))
        args.append(gf.reshape(1, D))
        out_shape.append(jax.ShapeDtypeStruct((R, D), F32))
        out_specs.append(pl.BlockSpec((R, D), lambda j: (0, 0)))
    out_shape += [jax.ShapeDtypeStruct((D, FF), BF16), jax.ShapeDtypeStruct((D, FF), BF16),
                  jax.ShapeDtypeStruct((FF, D), BF16)]
    out_specs += [pl.BlockSpec((D, tf), lambda j: (0, j)), pl.BlockSpec((D, tf), lambda j: (0, j)),
                  pl.BlockSpec((tf, D), lambda j: (j, 0))]
    return pl.pallas_call(
        functools.partial(_ffn_cast_body, final_norm=final_norm),
        grid=(FF // tf,),
        in_specs=in_specs,
        out_specs=out_specs,
        out_shape=out_shape,
        scratch_shapes=[pltpu.VMEM((R, D), BF16)],
        compiler_params=_cparams(("arbitrary",)),
        name="ffn_cast",
    )(*args)


def _in_proj_body(x_ref, g_ref, w_ref, wt_ref, ua_ref, ub_ref, c1_ref, k_ref, v_ref, ki_ref, tail_ref, c2b_ref,
                  *, q_scale, cols):
    h = _rms(x_ref[...], g_ref[...]).astype(BF16)
    o_a, o_b, o_qc, o_kv, o_qi, o_end = cols
    kv_w = k_ref.shape[1]
    d_c = o_kv - o_qc
    ua_ref[...] = _dot(h, w_ref[0, :, o_a:o_b])
    ub_ref[...] = _dot(h, w_ref[0, :, o_b:o_qc])
    c1_ref[:, :d_c] = (_dot(h, w_ref[0, :, o_qc:o_kv]) * q_scale).astype(BF16)
    c1_ref[:, d_c:] = _dot(h, w_ref[0, :, o_qi:o_end]).astype(BF16)
    kv = _dot(h, w_ref[0, :, o_kv:o_qi])
    t = _dot(h, wt_ref[0])
    k_ref[...] = kv[:, :kv_w]
    v_ref[...] = kv[:, kv_w:]
    ki_ref[...] = t[:, :IDX_DIM]
    tail_ref[...] = t
    c2b_ref[:, :2 * kv_w] = kv.astype(BF16)
    c2b_ref[:, 2 * kv_w:] = t.astype(BF16)


def _in_proj(x, g, w_all, w_tail, layer, *, tm, cols, q_scale):
    R, D = x.shape
    o_a, o_b, o_qc, o_kv, o_qi, o_end = cols
    kv_w = N_KV_HEADS * HEAD_DIM_C
    tw = w_tail.shape[2]
    outs = [(o_b - o_a, F32), (o_qc - o_b, F32), (o_kv - o_qc + o_end - o_qi, BF16), (kv_w, F32), (kv_w, F32),
            (IDX_DIM, F32), (tw, F32), (2 * kv_w + tw, BF16)]
    return pl.pallas_call(
        functools.partial(_in_proj_body, q_scale=q_scale, cols=cols),
        grid=(R // tm,),
        in_specs=[pl.BlockSpec((tm, D), lambda i: (i, 0)), pl.BlockSpec((1, D), lambda i: (0, 0)),
                  pl.BlockSpec((1,) + w_all.shape[1:], lambda i: (layer, 0, 0), pipeline_mode=pl.Buffered(1)),
                  pl.BlockSpec((1,) + w_tail.shape[1:], lambda i: (layer, 0, 0), pipeline_mode=pl.Buffered(1))],
        out_specs=[pl.BlockSpec((tm, wd), lambda i: (i, 0)) for wd, _ in outs],
        out_shape=[jax.ShapeDtypeStruct((R, wd), dt) for wd, dt in outs],
        compiler_params=_cparams(("parallel",)),
        name="in_proj",
    )(x, g.reshape(1, D), w_all, w_tail)


def _oproj_body(x_ref, ya_ref, yb_ref, yc_ref, w_ref, o_ref):
    da, db = ya_ref.shape[1], yb_ref.shape[1]
    o_ref[...] = (x_ref[...] + _dot(ya_ref[...], w_ref[:da]) + _dot(yb_ref[...], w_ref[da:da + db])
                  + _dot(yc_ref[...], w_ref[da + db:]))


def _out_proj(x, ya, yb, yc, w, *, tm):
    R, D = x.shape
    row = lambda a: pl.BlockSpec((tm, a.shape[1]), lambda i: (i, 0))
    return pl.pallas_call(
        _oproj_body,
        grid=(R // tm,),
        in_specs=[row(x), row(ya), row(yb), row(yc),
                  pl.BlockSpec(w.shape, lambda i: (0, 0), pipeline_mode=pl.Buffered(1))],
        out_specs=row(x),
        out_shape=jax.ShapeDtypeStruct((R, D), F32),
        compiler_params=_cparams(("parallel",)),
        name="out_proj",
    )(x, ya, yb, yc, w)


def _lru_body(xa_ref, ga_ref, conv0_ref, h0_ref, cw_ref, cb_ref, wa_ref, ba_ref, wx_ref, bx_ref, lam_ref,
              ya_ref, hl_ref, cn_ref, xs_ref, a_ref, u_ref, hs_ref, hc_ref, *, tc, t_valid_last):
    t = pl.program_id(1)
    da = xa_ref.shape[-1]
    blk = da // N_A_BLOCKS
    tail = 8

    @pl.when(t == 0)
    def _():
        xs_ref[0:tail, :] = conv0_ref[0]
        hc_ref[0:1, :] = h0_ref[0]

    xs_ref[tail:tail + tc, :] = xa_ref[0]
    xc = cb_ref[...]
    for j in range(CONV_WIDTH):
        off = tail - (CONV_WIDTH - 1) + j
        xc = xc + xs_ref[off:off + tc, :] * cw_ref[j:j + 1, :]
    xcb = xc.astype(BF16)
    pre_r = jnp.concatenate([_dot(xcb[:, n * blk:(n + 1) * blk], wa_ref[n]) for n in range(N_A_BLOCKS)], axis=1)
    pre_i = jnp.concatenate([_dot(xcb[:, n * blk:(n + 1) * blk], wx_ref[n]) for n in range(N_A_BLOCKS)], axis=1)
    r = jax.nn.sigmoid(pre_r + ba_ref[...])
    gi = jax.nn.sigmoid(pre_i + bx_ref[...])
    z = -lam_ref[...]
    softplus = jnp.maximum(z, 0.0) + jnp.log1p(jnp.exp(-jnp.abs(z)))
    log_a = (-LRU_C) * r * softplus
    a = jnp.exp(log_a)
    u = jnp.sqrt(-jnp.tanh(log_a) * (a * a + 1.0)) * gi * xc

    row_in_group = lax.broadcasted_iota(I32, (tc, da), 0) % 8
    for s in (1, 2, 4):
        first = row_in_group < s
        a_lo = jnp.where(first, 1.0, pltpu.roll(a, s, 0))
        u_lo = jnp.where(first, 0.0, pltpu.roll(u, s, 0))
        u = a * u_lo + u
        a = a * a_lo
    a_ref[...] = a
    u_ref[...] = u

    def step(k, h):
        base = pl.multiple_of(k * 8, 8)
        hg = a_ref[pl.ds(base, 8), :] * h + u_ref[pl.ds(base, 8), :]
        hs_ref[pl.ds(base, 8), :] = hg
        return hg[7:8, :]

    h = lax.fori_loop(0, tc // 8, step, hc_ref[0:1, :])
    hc_ref[0:1, :] = h
    ya_ref[0] = (hs_ref[...] * jax.nn.gelu(ga_ref[0])).astype(ya_ref.dtype)

    @pl.when(t == pl.num_programs(1) - 1)
    def _():
        hl_ref[0] = hs_ref[t_valid_last - 1:t_valid_last, :]
        lo = tail - (CONV_WIDTH - 1) + t_valid_last
        cn_ref[0] = xs_ref[lo:lo + CONV_WIDTH - 1, :]

    xs_ref[0:tail, :] = xs_ref[tc:tc + tail, :]


def _lru(ua, conv0, h0, cw, cb, wa, ba, wx, bx, lam, *, tc, t_valid_last):
    B, T, da2 = ua.shape
    da = da2 // 2
    vec = lambda: pl.BlockSpec((1, da), lambda b, t: (0, 0))
    blk = da // N_A_BLOCKS
    return pl.pallas_call(
        functools.partial(_lru_body, tc=tc, t_valid_last=t_valid_last),
        grid=(B, T // tc),
        in_specs=[
            pl.BlockSpec((1, tc, da), lambda b, t: (b, t, 0)),
            pl.BlockSpec((1, tc, da), lambda b, t: (b, t, 1)),
            pl.BlockSpec((1, 8, da), lambda b, t: (b, 0, 0)),
            pl.BlockSpec((1, 1, da), lambda b, t: (b, 0, 0)),
            pl.BlockSpec((CONV_WIDTH, da), lambda b, t: (0, 0)),
            vec(),
            pl.BlockSpec((N_A_BLOCKS, blk, blk), lambda b, t: (0, 0, 0)),
            vec(),
            pl.BlockSpec((N_A_BLOCKS, blk, blk), lambda b, t: (0, 0, 0)),
            vec(),
            vec(),
        ],
        out_specs=[
            pl.BlockSpec((1, tc, da), lambda b, t: (b, t, 0)),
            pl.BlockSpec((1, 1, da), lambda b, t: (b, 0, 0)),
            pl.BlockSpec((1, CONV_WIDTH - 1, da), lambda b, t: (b, 0, 0)),
        ],
        out_shape=[
            jax.ShapeDtypeStruct((B, T, da), BF16),
            jax.ShapeDtypeStruct((B, 1, da), F32),
            jax.ShapeDtypeStruct((B, CONV_WIDTH - 1, da), F32),
        ],
        scratch_shapes=[
            pltpu.VMEM((tc + 8, da), F32),
            pltpu.VMEM((tc, da), F32),
            pltpu.VMEM((tc, da), F32),
            pltpu.VMEM((tc, da), F32),
            pltpu.VMEM((8, da), F32),
        ],
        compiler_params=_cparams(("arbitrary", "arbitrary")),
        name="rglru",
    )(ua, ua, conv0, h0, cw, cb.reshape(1, da), wa, ba.reshape(1, da), wx, bx.reshape(1, da), lam.reshape(1, da))


def _ret_body(q_ref, k_ref, v_ref, gb_ref, cos_ref, sin_ref, s0_ref, yb_ref, sl_ref, s_ref, *, cp, c_valid, mm_dtype):
    t = pl.program_id(1)
    dk = q_ref.shape[-1] // N_RET_HEADS
    dv = v_ref.shape[-1] // N_RET_HEADS

    @pl.when(t == 0)
    def _():
        s_ref[...] = s0_ref[0]

    cos = cos_ref[...]
    sin = sin_ref[...]
    first_half = (lax.broadcasted_iota(I32, cos.shape, 1) % dk) < (dk // 2)
    width = cos.shape[1]

    def rot(x):
        partner = jnp.where(first_half, pltpu.roll(x, width - dk // 2, 1), pltpu.roll(x, dk // 2, 1))
        return x * cos + partner * sin

    row = lax.broadcasted_iota(I32, (cp, 1), 0)
    q_all = rot(q_ref[0])
    k_all = rot(k_ref[0]) * (dk ** -0.5)
    ri = lax.broadcasted_iota(I32, (cp, cp), 0)
    ci = lax.broadcasted_iota(I32, (cp, cp), 1)
    diff = (ri - ci).astype(F32)
    rowf = row.astype(F32)
    for h in range(N_RET_HEADS):
        lg = math.log1p(-(2.0 ** (-5.0 - h)))
        dmask = jnp.where(diff >= 0, jnp.exp(jnp.maximum(diff, 0.0) * lg), 0.0)
        cross_dec = jnp.exp((rowf + 1.0) * lg)
        state_dec = jnp.exp((c_valid - 1.0 - rowf) * lg)
        chunk_dec = math.exp(c_valid * lg)
        state = s_ref[h]
        for n in range(q_all.shape[0] // cp):
            rows = slice(n * cp, (n + 1) * cp)
            qh = q_all[rows, h * dk:(h + 1) * dk].astype(mm_dtype)
            kh = k_all[rows, h * dk:(h + 1) * dk]
            if c_valid < cp:
                kh = jnp.where(row < c_valid, kh, 0.0)
            vh = v_ref[0, rows, h * dv:(h + 1) * dv].astype(mm_dtype)
            att = _dot_nt(qh, kh.astype(mm_dtype)) * dmask
            inner = _dot(att.astype(mm_dtype), vh)
            cross = _dot(qh, state.astype(mm_dtype)) * cross_dec
            state = state * chunk_dec + _dot_tn((kh * state_dec).astype(mm_dtype), vh)
            o = inner + cross
            o = o * lax.rsqrt(jnp.mean(o * o, axis=-1, keepdims=True) + EPS)
            gh = gb_ref[0, rows, h * dv:(h + 1) * dv]
            yb_ref[0, rows, h * dv:(h + 1) * dv] = (o * (gh * jax.nn.sigmoid(gh))).astype(yb_ref.dtype)
        s_ref[h] = state

    @pl.when(t == pl.num_programs(1) - 1)
    def _():
        sl_ref[0] = s_ref[...]


def _retention(ub, cos, sin, s0, *, cp, c_valid, mm_dtype, chunks_per_step=1):
    B, T, wtot = ub.shape
    w = wtot // 6
    H, dk, dv = s0.shape[1:]
    blk = cp * chunks_per_step
    return pl.pallas_call(
        functools.partial(_ret_body, cp=cp, c_valid=c_valid, mm_dtype=mm_dtype),
        grid=(B, T // blk),
        in_specs=[
            pl.BlockSpec((1, blk, w), lambda b, t: (b, t, 0)),
            pl.BlockSpec((1, blk, w), lambda b, t: (b, t, 1)),
            pl.BlockSpec((1, blk, 2 * w), lambda b, t: (b, t, 1)),
            pl.BlockSpec((1, blk, 2 * w), lambda b, t: (b, t, 2)),
            pl.BlockSpec((blk, w), lambda b, t: (t, 0)),
            pl.BlockSpec((blk, w), lambda b, t: (t, 0)),
            pl.BlockSpec((1, H, dk, dv), lambda b, t: (b, 0, 0, 0)),
        ],
        out_specs=[
            pl.BlockSpec((1, blk, 2 * w), lambda b, t: (b, t, 0)),
            pl.BlockSpec((1, H, dk, dv), lambda b, t: (b, 0, 0, 0)),
        ],
        out_shape=[
            jax.ShapeDtypeStruct((B, T, 2 * w), BF16),
            jax.ShapeDtypeStruct((B, H, dk, dv), F32),
        ],
        scratch_shapes=[pltpu.VMEM((H, dk, dv), F32)],
        compiler_params=_cparams(("arbitrary", "arbitrary")),
        name="retention",
    )(ub, ub, ub, ub, cos, sin, s0)


def _rope_tables(pos, dk, heads):
    half = dk // 2
    freqs = ROPE_BASE ** (-jnp.arange(half, dtype=F32) / half)
    ang = pos.astype(F32)[:, None] * freqs[None, :]
    cos, sin = jnp.cos(ang), jnp.sin(ang)
    cos_t = jnp.tile(jnp.concatenate([cos, cos], axis=1), (1, heads))
    sin_t = jnp.tile(jnp.concatenate([-sin, sin], axis=1), (1, heads))
    return cos_t, sin_t


def _t5_bucket_np(dist):
    n = np.maximum(dist, 0)
    max_exact = NUM_BUCKETS // 2
    ratio = np.log(np.maximum(n, 1).astype(np.float32) / np.float32(max_exact)) / np.float32(math.log(MAX_DISTANCE / max_exact))
    large = max_exact + (ratio * np.float32(NUM_BUCKETS - max_exact)).astype(np.int32)
    large = np.minimum(large, NUM_BUCKETS - 1)
    return np.where(n < max_exact, n, large).astype(np.int32)


def _bias_from_buckets(bucket, rb_ref, head):
    def step(b, out):
        return jnp.where(bucket == b, rb_ref[b, head], out)

    return lax.fori_loop(0, NUM_BUCKETS, step, jnp.zeros(bucket.shape, F32))


def _dsa_prompt_body(rb_ref, qc_ref, qi_ref, wi_ref, kvb_ref, btab_ref, o_ref,
                     keys_ref, bt_ref, wf_ref, qis_ref, qs_ref, mx_ref, mrep_ref, acc_ref, cut_ref, kmax_ref, sel_ref,
                     *, tq, n_keep, idx_bits):
    i = pl.program_id(0)
    G = qc_ref.shape[1] // (N_KV_HEADS * HEAD_DIM_C)
    n_heads = N_KV_HEADS * G
    kcol, vcol, icol = 0, N_KV_HEADS * HEAD_DIM_C, 2 * N_KV_HEADS * HEAD_DIM_C
    lanes = HEAD_DIM_C
    halves = tq // lanes
    n_tiles = i + 1

    @pl.when(i == 0)
    def _():
        for h in range(n_heads):
            far = rb_ref[NUM_BUCKETS - 1, h]
            for r in range(2):
                bt_ref[h // G, r, h % G] = _bias_from_buckets(btab_ref[r], rb_ref, h) - far

    for h in range(N_IDX_HEADS):
        qis_ref[h] = qi_ref[:, h * IDX_DIM:(h + 1) * IDX_DIM]
        wf_ref[h] = jnp.broadcast_to(wi_ref[:, IDX_DIM + h:IDX_DIM + h + 1], (tq, lanes))
    for c in range(N_KV_HEADS):
        for g in range(G):
            h = c * G + g
            qs_ref[c, g * tq:(g + 1) * tq, :] = qc_ref[:, h * HEAD_DIM_C:(h + 1) * HEAD_DIM_C]

    rowi = lax.broadcasted_iota(I32, (tq, tq), 0)
    coli = lax.broadcasted_iota(I32, (tq, tq), 1)

    def key_index(j):
        return j * tq + coli

    def visible(j):
        return key_index(j) <= (i * tq + rowi)

    def score_tile(j, carry):
        ki = kvb_ref[pl.ds(pl.multiple_of(j * tq, tq), tq), icol:icol + IDX_DIM]
        for rh in range(halves):
            rows = slice(rh * lanes, (rh + 1) * lanes)
            acc = jnp.zeros((lanes, tq), F32)
            for h in range(N_IDX_HEADS):
                w = wf_ref[h, rows, :]
                acc = acc + jnp.maximum(_dot_nt(qis_ref[h, rows, :], ki), 0.0) * jnp.concatenate([w] * halves, axis=1)
            score = acc * (N_IDX_HEADS ** -0.5 * IDX_DIM ** -0.5)
            score = jnp.where(visible(j)[rows], score, -jnp.inf)
            keys_ref[j, rows, :] = _sort_key(score)
        return carry

    lax.fori_loop(0, n_tiles, score_tile, 0)

    ones_sq = jnp.ones((lanes, lanes), BF16)

    def count(pred, *row_args):
        totals = []
        for rh in range(halves):
            rows = slice(rh * lanes, (rh + 1) * lanes)
            args = [a[rows] for a in row_args]

            def one(j, cnt):
                for a in range(halves):
                    cols = slice(a * lanes, (a + 1) * lanes)
                    idx = j * tq + a * lanes + lax.broadcasted_iota(I32, (lanes, lanes), 1)
                    cnt = cnt + jnp.where(pred(keys_ref[j, rows, cols], idx, *args), 1, 0)
                return cnt

            n_pairs = n_tiles // 2
            cnt = lax.fori_loop(0, n_pairs, lambda p, c: one(2 * p + 1, one(2 * p, c)), jnp.zeros((lanes, lanes), I32))
            totals.append(lax.fori_loop(2 * n_pairs, n_tiles, one, cnt))
        return _dot(jnp.concatenate(totals, axis=0).astype(F32).astype(BF16), ones_sq).astype(I32)

    def bit_step(b, st):
        tau, n_ge = st
        cand = tau + lax.shift_left(jnp.int32(1), 31 - b)
        c = count(lambda k, idx, cnd: k >= cnd, cand)
        ge = c >= n_keep
        return jnp.where(ge, cand, tau), jnp.where(ge, c, n_ge)

    sel_ref[0] = jnp.full((tq, lanes), INT_MIN, I32)
    sel_ref[1] = jnp.broadcast_to(n_tiles * tq, (tq, lanes)).astype(I32)
    for b0, b1 in zip(SELECT_BIT_GROUPS[:-1], SELECT_BIT_GROUPS[1:]):
        @pl.when(jnp.max(sel_ref[1]) > n_keep)
        def _():
            tau_g, n_ge_g = lax.fori_loop(b0, b1, bit_step, (sel_ref[0], sel_ref[1]))
            sel_ref[0] = tau_g
            sel_ref[1] = n_ge_g

    tau, n_ge = sel_ref[0], sel_ref[1]

    cut_ref[...] = jnp.full((tq, lanes), 2 ** 31 - 1, I32)

    @pl.when(jnp.max(n_ge) > n_keep)
    def _():
        need = n_keep - count(lambda k, idx, t: k > t, tau)

        def idx_step(b, cut):
            cand = cut + lax.shift_left(jnp.int32(1), idx_bits - 1 - b)
            below = count(lambda k, idx, t, c: (k == t) & (idx < c), tau, cand)
            return jnp.where(below < need, cand, cut)

        cut_ref[...] = lax.fori_loop(0, idx_bits, idx_step, jnp.zeros((tq, lanes), I32))

    tau_t = jnp.concatenate([tau] * halves, axis=1)
    cut_t = jnp.concatenate([cut_ref[...]] * halves, axis=1)

    def mask_bias(j, r):
        k = keys_ref[j]
        sel = (k > tau_t) | ((k == tau_t) & (key_index(j) <= cut_t))
        if r is not None:
            sel = sel & visible(j)
        return jnp.where(sel, 0.0, NEG_BIG)

    def logits(j, c, r, mb):
        base = pl.multiple_of(j * tq, tq)
        kc = kvb_ref[pl.ds(base, tq), kcol + c * HEAD_DIM_C:kcol + (c + 1) * HEAD_DIM_C]
        s = _dot_nt(qs_ref[c], kc).reshape(G, tq, tq) + mb[None]
        if r is not None:
            s = s + bt_ref[c, r]
        return s

    def max_tile(j, r):
        mb = mask_bias(j, r)
        for c in range(N_KV_HEADS):
            s = logits(j, c, r, mb)
            m = s[..., 0:lanes]
            for a in range(1, halves):
                m = jnp.maximum(m, s[..., a * lanes:(a + 1) * lanes])
            mx_ref[c] = jnp.maximum(mx_ref[c], m)

    def acc_tile(j, r):
        base = pl.multiple_of(j * tq, tq)
        mb = mask_bias(j, r)
        for c in range(N_KV_HEADS):
            vc = kvb_ref[pl.ds(base, tq), vcol + c * HEAD_DIM_C:vcol + (c + 1) * HEAD_DIM_C]
            vext = jnp.concatenate([vc, jnp.ones((tq, lanes), BF16)], axis=1)
            m = mrep_ref[c]
            p = jnp.exp(logits(j, c, r, mb) - jnp.concatenate([m] * halves, axis=-1))
            acc_ref[c] += _dot(p.reshape(G * tq, tq).astype(BF16), vext)

    def sweep(tile_fn):
        def far(j, carry):
            tile_fn(j, None)
            return carry

        lax.fori_loop(0, jnp.maximum(i - 1, 0), far, 0)

        @pl.when(i >= 1)
        def _():
            tile_fn(i - 1, 1)

        tile_fn(i, 0)

    @pl.when(i == 0)
    def _():
        kmax_ref[...] = jnp.zeros(kmax_ref.shape, F32)

    for c in range(N_KV_HEADS):
        kt = kvb_ref[pl.ds(pl.multiple_of(i * tq, tq), tq), kcol + c * HEAD_DIM_C:kcol + (c + 1) * HEAD_DIM_C].astype(F32)
        knorm = jnp.sqrt(jnp.max(jnp.sum(kt * kt, axis=1, keepdims=True)))
        kmax_ref[c] = jnp.maximum(kmax_ref[c], knorm)
        qf = qs_ref[c].astype(F32)
        qnorm = jnp.sqrt(jnp.sum(qf * qf, axis=1, keepdims=True)).reshape(G, tq, 1)
        for g in range(G):
            h = c * G + g
            far = rb_ref[NUM_BUCKETS - 1, h]
            bias_max = lax.fori_loop(0, NUM_BUCKETS, lambda b, m: jnp.maximum(m, rb_ref[b, h] - far), jnp.float32(0.0))
            mrep_ref[c, g] = qnorm[g] * kmax_ref[c, 0:1, :] + bias_max

    acc_ref[...] = jnp.zeros(acc_ref.shape, F32)
    sweep(acc_tile)

    @pl.when(jnp.logical_not(jnp.min(acc_ref[:, :, lanes:]) > 1e-30))
    def _():
        mx_ref[...] = jnp.full(mx_ref.shape, NEG_BIG, F32)
        sweep(max_tile)
        mrep_ref[...] = jnp.broadcast_to(jnp.max(mx_ref[...], axis=-1, keepdims=True), mrep_ref.shape)
        acc_ref[...] = jnp.zeros(acc_ref.shape, F32)
        sweep(acc_tile)

    for c in range(N_KV_HEADS):
        for g in range(G):
            h = c * G + g
            a = acc_ref[c, g * tq:(g + 1) * tq, :]
            o_ref[:, h * HEAD_DIM_C:(h + 1) * HEAD_DIM_C] = (a[:, :lanes] / a[:, lanes:]).astype(o_ref.dtype)


def _dsa_prompt(qcqi, tail, kvi_bf16, rel_bias, *, tq, n_keep):
    S = qcqi.shape[0]
    dc = qcqi.shape[1] - N_IDX_HEADS * IDX_DIM
    n_heads = dc // HEAD_DIM_C
    G = n_heads // N_KV_HEADS
    wkv = kvi_bf16.shape[1]
    d = np.arange(tq)[:, None] - np.arange(tq)[None, :]
    btab = jnp.asarray(np.stack([_t5_bucket_np(d + r * tq) for r in range(2)]))
    assert _t5_bucket_np(np.array([tq + 1]))[0] == NUM_BUCKETS - 1
    lanes = HEAD_DIM_C
    grid_spec = pltpu.PrefetchScalarGridSpec(
        num_scalar_prefetch=0,
        grid=(S // tq,),
        in_specs=[
            pl.BlockSpec(memory_space=pltpu.SMEM),
            pl.BlockSpec((tq, dc), lambda i: (i, 0)),
            pl.BlockSpec((tq, N_IDX_HEADS * IDX_DIM), lambda i: (i, dc // (N_IDX_HEADS * IDX_DIM))),
            pl.BlockSpec((tq, tail.shape[1]), lambda i: (i, 0)),
            pl.BlockSpec((S, wkv), lambda i: (0, 0), pipeline_mode=pl.Buffered(1)),
            pl.BlockSpec((2, tq, tq), lambda i: (0, 0, 0)),
        ],
        out_specs=pl.BlockSpec((tq, dc), lambda i: (i, 0)),
        scratch_shapes=[
            pltpu.VMEM((S // tq, tq, tq), I32),
            pltpu.VMEM((N_KV_HEADS, 2, G, tq, tq), F32),
            pltpu.VMEM((N_IDX_HEADS, tq, lanes), F32),
            pltpu.VMEM((N_IDX_HEADS, tq, IDX_DIM), BF16),
            pltpu.VMEM((N_KV_HEADS, G * tq, HEAD_DIM_C), BF16),
            pltpu.VMEM((N_KV_HEADS, G, tq, lanes), F32),
            pltpu.VMEM((N_KV_HEADS, G, tq, lanes), F32),
            pltpu.VMEM((N_KV_HEADS, G * tq, 2 * lanes), F32),
            pltpu.VMEM((tq, lanes), I32),
            pltpu.VMEM((N_KV_HEADS, 8, lanes), F32),
            pltpu.VMEM((2, tq, lanes), I32),
        ],
    )
    return pl.pallas_call(
        functools.partial(_dsa_prompt_body, tq=tq, n_keep=n_keep, idx_bits=(S - 1).bit_length()),
        grid_spec=grid_spec,
        out_shape=jax.ShapeDtypeStruct((S, dc), BF16),
        compiler_params=_cparams(("arbitrary",)),
        name="dsa_prompt",
    )(rel_bias, qcqi, qcqi, tail, kvi_bf16, btab)


def _dsa_sel_body(pt_ref, qi_ref, w_ref, kin_ref, *rest, t_new, n_keep):
    pages = rest[:PAGES_PER_STEP]
    mb_ref, keys_ref, arg_ref, cnt_ref = rest[PAGES_PER_STEP:]
    b = pl.program_id(0)
    p = pl.program_id(1)
    B, n_chunks, tp, P = keys_ref.shape
    last = p == pl.num_programs(1) - 1
    q = qi_ref[0]
    w = w_ref[0]
    rowi = lax.broadcasted_iota(I32, (tp, P), 0)
    coli = lax.broadcasted_iota(I32, (tp, P), 1)
    vis_new = (coli <= rowi) & (coli < t_new)

    def chunk_scores(s):
        s = jnp.maximum(s, 0.0) * w
        s = s.reshape(tp, N_IDX_HEADS, s.shape[-1]).sum(axis=1)
        return s * (N_IDX_HEADS ** -0.5 * IDX_DIM ** -0.5)

    for r in range(PAGES_PER_STEP):
        keys_ref[b, p * PAGES_PER_STEP + r] = _sort_key(chunk_scores(_dot(q, pages[r][0].astype(BF16))))

    @pl.when(last)
    def _():
        s_new = jnp.where(vis_new, chunk_scores(_dot_nt(q, kin_ref[0])), -jnp.inf)
        keys_ref[b, n_chunks - 1] = _sort_key(s_new)

    @pl.when(last & (b == B - 1))
    def _():
        key_index = (lax.broadcasted_iota(I32, (n_chunks, tp, P), 0) * P
                     + lax.broadcasted_iota(I32, (n_chunks, tp, P), 2))

        def count(pred, *row_args):
            for n, a in enumerate(row_args):
                arg_ref[n] = a

            def per_seq(s, carry):
                args = [arg_ref[n, s][None] for n in range(len(row_args))]
                cnt_ref[s] = jnp.sum(jnp.where(pred(keys_ref[s], key_index, *args), 1, 0), axis=0)
                return carry

            lax.fori_loop(0, B, per_seq, 0)
            return jnp.broadcast_to(jnp.sum(cnt_ref[...], axis=-1, keepdims=True), (B, tp, P))

        def bit_step(bit, tau):
            cand = tau + lax.shift_left(jnp.int32(1), 31 - bit)
            return jnp.where(count(lambda k, idx, c: k >= c, cand) >= n_keep, cand, tau)

        tau = lax.fori_loop(0, 32, bit_step, jnp.full((B, tp, P), INT_MIN, I32))

        arg_ref[2] = jnp.full((B, tp, P), 2 ** 31 - 1, I32)

        @pl.when(jnp.max(count(lambda k, idx, t: k >= t, tau)) > n_keep)
        def _():
            need = n_keep - count(lambda k, idx, t: k > t, tau)
            idx_bits = (n_chunks * P - 1).bit_length()

            def idx_step(bit, cut):
                cand = cut + lax.shift_left(jnp.int32(1), idx_bits - 1 - bit)
                below = count(lambda k, idx, t, c: (k == t) & (idx < c), tau, cand)
                return jnp.where(below < need, cand, cut)

            arg_ref[2] = lax.fori_loop(0, idx_bits, idx_step, jnp.zeros((B, tp, P), I32))

        arg_ref[0] = tau

        def write_mask(s, carry):
            k = keys_ref[s]
            t = arg_ref[0, s][None]
            sel = (k > t) | ((k == t) & (key_index <= arg_ref[2, s][None]))
            mb_ref[s] = jnp.where(sel, 0.0, NEG_BIG)
            mb_ref[s, n_chunks - 1] = jnp.where(sel[n_chunks - 1] & vis_new, 0.0, NEG_BIG)
            return carry

        lax.fori_loop(0, B, write_mask, 0)


def _dsa_sample_select(pt_flat, qi, w, ki_new, pool_ki, *, n_pages, t_new, n_keep):
    B = qi.shape[0]
    P = pool_ki.shape[2]
    steps = n_pages // PAGES_PER_STEP

    def page_spec(r):
        return pl.BlockSpec((1, IDX_DIM, P), lambda b, p, pt: (pt[b * n_pages + p * PAGES_PER_STEP + r], 0, 0))

    grid_spec = pltpu.PrefetchScalarGridSpec(
        num_scalar_prefetch=1,
        grid=(B, steps),
        in_specs=[
            pl.BlockSpec((1,) + qi.shape[1:], lambda b, p, pt: (b, 0, 0)),
            pl.BlockSpec((1,) + w.shape[1:], lambda b, p, pt: (b, 0, 0)),
            pl.BlockSpec((1,) + ki_new.shape[1:], lambda b, p, pt: (b, 0, 0)),
        ] + [page_spec(r) for r in range(PAGES_PER_STEP)],
        out_specs=pl.BlockSpec((B, n_pages + 1, SAMPLE_PAD_T, P), lambda b, p, pt: (0, 0, 0, 0)),
        scratch_shapes=[
            pltpu.VMEM((B, n_pages + 1, SAMPLE_PAD_T, P), I32),
            pltpu.VMEM((3, B, SAMPLE_PAD_T, P), I32),
            pltpu.VMEM((B, SAMPLE_PAD_T, P), I32),
        ],
    )
    return pl.pallas_call(
        functools.partial(_dsa_sel_body, t_new=t_new, n_keep=n_keep),
        grid_spec=grid_spec,
        out_shape=jax.ShapeDtypeStruct((B, n_pages + 1, SAMPLE_PAD_T, P), F32),
        compiler_params=_cparams(("arbitrary", "arbitrary")),
        name="dsa_sample_select",
    )(pt_flat, qi, w, ki_new, *([pool_ki] * PAGES_PER_STEP))


def _dsa_att_body(pt_ref, rb_ref, q_ref, mb_ref, kn_ref, vn_ref, btab_ref, *rest):
    kp = rest[:PAGES_PER_STEP]
    vp = rest[PAGES_PER_STEP:2 * PAGES_PER_STEP]
    o_ref, bt_ref, m_ref, acc_ref = rest[2 * PAGES_PER_STEP:]
    b = pl.program_id(0)
    p = pl.program_id(1)
    n_chunks = mb_ref.shape[1]
    tp = SAMPLE_PAD_T
    G = q_ref.shape[2] // tp
    rows = G * tp
    P = kn_ref.shape[1]
    lanes = HEAD_DIM_C
    last = p == pl.num_programs(1) - 1

    @pl.when((b == 0) & (p == 0))
    def _():
        for c in range(N_KV_HEADS):
            for kind in range(2):
                for g in range(G):
                    h = c * G + g
                    bt_ref[c, kind, g * tp:(g + 1) * tp, :] = (_bias_from_buckets(btab_ref[kind], rb_ref, h)
                                                              - rb_ref[NUM_BUCKETS - 1, h])

    @pl.when(p == 0)
    def _():
        m_ref[...] = jnp.full(m_ref.shape, NEG_BIG, F32)
        acc_ref[...] = jnp.zeros(acc_ref.shape, F32)

    def attend(c, s, vext):
        m_old = m_ref[c]
        m_new = jnp.maximum(m_old, jnp.max(s, axis=1, keepdims=True))
        alpha = jnp.exp(m_old - m_new)
        pr = jnp.exp(s - jnp.concatenate([m_new] * (s.shape[1] // lanes), axis=1))
        acc_ref[c] = jnp.concatenate([alpha, alpha], axis=1) * acc_ref[c] + _dot(pr.astype(BF16), vext)
        m_ref[c] = m_new

    ones = jnp.ones((P, lanes), BF16)
    mb_step = jnp.concatenate([mb_ref[0, p * PAGES_PER_STEP + r] for r in range(PAGES_PER_STEP)], axis=1)
    mb_step = jnp.concatenate([mb_step] * G, axis=0)
    for c in range(N_KV_HEADS):
        head_rows = pl.ds(c, P, stride=N_KV_HEADS)
        k_all = jnp.concatenate([kp[r][0, head_rows, :].astype(BF16) for r in range(PAGES_PER_STEP)], axis=0)
        v_all = jnp.concatenate(
            [jnp.concatenate([vp[r][0, head_rows, :].astype(BF16), ones], axis=1) for r in range(PAGES_PER_STEP)], axis=0)
        s = _dot_nt(q_ref[0, c], k_all) + mb_step
        s = jnp.concatenate([s[:, :-P], s[:, -P:] + jnp.where(last, bt_ref[c, 0], 0.0)], axis=1)
        attend(c, s, v_all)

    @pl.when(last)
    def _():
        mb_new = jnp.concatenate([mb_ref[0, n_chunks - 1]] * G, axis=0)
        for c in range(N_KV_HEADS):
            cols = slice(c * HEAD_DIM_C, (c + 1) * HEAD_DIM_C)
            s = _dot_nt(q_ref[0, c], kn_ref[0, :, cols]) + mb_new + bt_ref[c, 1]
            attend(c, s, jnp.concatenate([vn_ref[0, :, cols], ones], axis=1))
            a = acc_ref[c]
            o_ref[0, c] = a[:, :lanes] / a[:, lanes:]


def _dsa_sample_attend(pt_flat, rel_bias, q, mb, k_new, v_new, pool_k, pool_v, *, n_pages, past):
    B, kvh, rows, dh = q.shape
    P = pool_k.shape[1] // kvh
    steps = n_pages // PAGES_PER_STEP
    tp = SAMPLE_PAD_T
    t = np.arange(tp)[:, None]
    col = np.arange(P)[None, :]
    assert past - (n_pages - 1) * P >= MAX_DISTANCE
    last_page = _t5_bucket_np(past + t - ((n_pages - 1) * P + col))
    new = _t5_bucket_np(t - col)
    btab = jnp.asarray(np.stack([last_page, new]))

    def page_spec(r):
        return pl.BlockSpec((1, P * kvh, dh), lambda b, p, pt: (pt[b * n_pages + p * PAGES_PER_STEP + r], 0, 0))

    grid_spec = pltpu.PrefetchScalarGridSpec(
        num_scalar_prefetch=1,
        grid=(B, steps),
        in_specs=[
            pl.BlockSpec(memory_space=pltpu.SMEM),
            pl.BlockSpec((1, kvh, rows, dh), lambda b, p, pt: (b, 0, 0, 0)),
            pl.BlockSpec((1,) + mb.shape[1:], lambda b, p, pt: (b, 0, 0, 0)),
            pl.BlockSpec((1,) + k_new.shape[1:], lambda b, p, pt: (b, 0, 0)),
            pl.BlockSpec((1,) + v_new.shape[1:], lambda b, p, pt: (b, 0, 0)),
            pl.BlockSpec((2, tp, P), lambda b, p, pt: (0, 0, 0)),
        ] + [page_spec(r) for r in range(PAGES_PER_STEP)] * 2,
        out_specs=pl.BlockSpec((1, kvh, rows, dh), lambda b, p, pt: (b, 0, 0, 0)),
        scratch_shapes=[
            pltpu.VMEM((kvh, 2, rows, P), F32),
            pltpu.VMEM((kvh, rows, dh), F32),
            pltpu.VMEM((kvh, rows, 2 * dh), F32),
        ],
    )
    return pl.pallas_call(
        _dsa_att_body,
        grid_spec=grid_spec,
        out_shape=jax.ShapeDtypeStruct((B, kvh, rows, dh), F32),
        compiler_params=_cparams(("arbitrary", "arbitrary")),
        name="dsa_sample_attend",
    )(pt_flat, rel_bias, q, mb, k_new, v_new, btab, *([pool_k] * PAGES_PER_STEP), *([pool_v] * PAGES_PER_STEP))


def _dsa_sample(c1, tail, c2b, pt_flat, rel_bias, pool_k, pool_v, pool_ki, *, Bs, Ts, n_pages):
    tp = SAMPLE_PAD_T
    kv_w = N_KV_HEADS * HEAD_DIM_C
    d_c = c1.shape[1] - N_IDX_HEADS * IDX_DIM
    G = d_c // kv_w
    P = pool_ki.shape[2]
    past = n_pages * P
    qi_s = c1[:, d_c:].reshape(Bs, tp * N_IDX_HEADS, IDX_DIM)
    w_s = tail[:, IDX_DIM:IDX_DIM + N_IDX_HEADS].reshape(Bs, tp * N_IDX_HEADS, 1)
    new_rows = jnp.pad(c2b.reshape(Bs, tp, -1), ((0, 0), (0, P - tp), (0, 0)))
    mb = _dsa_sample_select(pt_flat, qi_s, w_s, new_rows[:, :, 2 * kv_w:2 * kv_w + IDX_DIM], pool_ki,
                            n_pages=n_pages, t_new=Ts, n_keep=min(TOPK_MAX, (past + Ts) // 4))
    q_s = c1[:, :d_c].reshape(Bs, tp, N_KV_HEADS, G, HEAD_DIM_C).transpose(0, 2, 3, 1, 4)
    q_s = q_s.reshape(Bs, N_KV_HEADS, G * tp, HEAD_DIM_C)
    o_s = _dsa_sample_attend(pt_flat, rel_bias, q_s, mb, new_rows[:, :, :kv_w], new_rows[:, :, kv_w:2 * kv_w],
                             pool_k, pool_v, n_pages=n_pages, past=past)
    return o_s.reshape(Bs, N_KV_HEADS, G, tp, HEAD_DIM_C).transpose(0, 3, 1, 2, 4).reshape(Bs * tp, d_c).astype(BF16)


def kernel(x_prompt, x_sample, cache_k, cache_v, cache_kidx, page_table, state_lru_h, state_conv, state_ret,
           norm_ffn1, ffn1_gate, ffn1_up, ffn1_down, norm_mix, w_in, w_out, conv_w, conv_b,
           lru_wa, lru_ba, lru_wx, lru_bx, lru_lambda, rel_bias,
           norm_ffn2, ffn2_gate, ffn2_up, ffn2_down, norm_final):
    depth = norm_ffn1.shape[0]
    _, S, D = x_prompt.shape
    Bs, Ts, _ = x_sample.shape
    n_pool, P = cache_k.shape[1], cache_k.shape[2]
    n_pages = page_table.shape[1]
    past = n_pages * P
    d_a = state_lru_h.shape[-1]
    H_r, rdk, rdv = state_ret.shape[2:]
    d_b = H_r * rdv
    d_c = D - d_a - d_b
    kv_w = N_KV_HEADS * HEAD_DIM_C
    qi_w = N_IDX_HEADS * IDX_DIM
    G = d_c // HEAD_DIM_C // N_KV_HEADS
    tp = SAMPLE_PAD_T
    Rs = Bs * tp

    xp = x_prompt.reshape(S, D)
    xs = jnp.pad(x_sample, ((0, 0), (0, tp - Ts), (0, 0))).reshape(Rs, D)

    cos_p, sin_p = _rope_tables(jnp.arange(S, dtype=I32), rdk, H_r)
    cos_s, sin_s = _rope_tables(past + jnp.arange(tp, dtype=I32), rdk, H_r)

    pool_k = cache_k.reshape(depth * n_pool, P * N_KV_HEADS, HEAD_DIM_C)
    pool_v = cache_v.reshape(depth * n_pool, P * N_KV_HEADS, HEAD_DIM_C)
    pool_ki = jnp.swapaxes(cache_kidx, 2, 3).reshape(depth * n_pool, IDX_DIM, P)

    o = np.cumsum([0, 2 * d_a, 2 * H_r * rdk + 2 * d_b, d_c, 2 * kv_w, qi_w, IDX_DIM + N_IDX_HEADS])
    proj_cols = tuple(int(c) for c in o[:6])
    w_in_b = w_in.astype(BF16)
    w_in_tail = jnp.pad(w_in_b[:, :, o[5]:o[6]], ((0, 0), (0, 0), (0, 128 - int(o[6] - o[5]))))

    zeros_conv = jnp.zeros((1, 8, d_a), F32)
    zeros_h = jnp.zeros((1, 1, d_a), F32)
    zeros_s = jnp.zeros((1, H_r, rdk, rdv), F32)

    outs_p, outs_s = [], []
    y_prompt = y_sample = None
    for l in range(depth):
        in_proj = functools.partial(_in_proj, g=norm_mix[l], w_all=w_in_b, w_tail=w_in_tail, layer=l,
                                    cols=proj_cols, q_scale=HEAD_DIM_C ** -0.5)
        wo = w_out[l].astype(BF16)
        lwa, lwx = lru_wa[l].astype(BF16), lru_wx[l].astype(BF16)
        last = l == depth - 1
        pt_flat = (page_table + l * n_pool).reshape(-1).astype(I32)

        xs, *f1 = _ffn_cast(xs, norm_ffn1[l], ffn1_gate, ffn1_up, ffn1_down, l, tf=512)
        ua, ub, c1, k_new, v_new, ki_new, tail, c2b = in_proj(xs, tm=Rs)
        conv0 = jnp.pad(state_conv[l], ((0, 0), (8 - (CONV_WIDTH - 1), 0), (0, 0)))
        ya, h_last, conv_new = _lru(ua.reshape(Bs, tp, 2 * d_a), conv0, state_lru_h[l].reshape(Bs, 1, d_a),
                                    conv_w[l], conv_b[l], lwa, lru_ba[l], lwx, lru_bx[l], lru_lambda[l],
                                    tc=tp, t_valid_last=Ts)
        yb, s_last = _retention(ub.reshape(Bs, tp, -1), cos_s, sin_s, state_ret[l], cp=tp, c_valid=Ts, mm_dtype=F32)
        yc = _dsa_sample(c1, tail, c2b, pt_flat, rel_bias, pool_k, pool_v, pool_ki, Bs=Bs, Ts=Ts, n_pages=n_pages)
        xs = _out_proj(xs, ya.reshape(Rs, d_a), yb.reshape(Rs, d_b), yc, wo, tm=Rs)
        if last:
            xs, y_sample, *f2 = _ffn_cast(xs, norm_ffn2[l], ffn2_gate, ffn2_up, ffn2_down, l, norm_final, tf=512)
        else:
            xs, *f2 = _ffn_cast(xs, norm_ffn2[l], ffn2_gate, ffn2_up, ffn2_down, l, tf=512)
        valid = lambda a: a.reshape(Bs, tp, -1)[:, :Ts]
        outs_s.append((valid(k_new).reshape(Bs, Ts, N_KV_HEADS, HEAD_DIM_C),
                       valid(v_new).reshape(Bs, Ts, N_KV_HEADS, HEAD_DIM_C),
                       valid(ki_new),
                       h_last.reshape(Bs, d_a), conv_new, s_last))

        xp = _ffn(xp, norm_ffn1[l], *f1, tm=1024, tf=256)
        ua, ub, c1, k_new, v_new, ki_new, tail, c2b = in_proj(xp, tm=256)
        ya, h_last, conv_new = _lru(ua.reshape(1, S, 2 * d_a), zeros_conv, zeros_h, conv_w[l], conv_b[l],
                                    lwa, lru_ba[l], lwx, lru_bx[l], lru_lambda[l], tc=1024, t_valid_last=1024)
        yb, s_last = _retention(ub.reshape(1, S, -1), cos_p, sin_p, zeros_s, cp=128, c_valid=128, mm_dtype=BF16,
                                chunks_per_step=4)
        yc = _dsa_prompt(c1, tail, c2b, rel_bias, tq=256, n_keep=min(TOPK_MAX, S // 4))
        xp = _out_proj(xp, ya.reshape(S, d_a), yb.reshape(S, d_b), yc, wo, tm=512)
        if last:
            xp, y_prompt = _ffn(xp, norm_ffn2[l], *f2, norm_final, tm=512, tf=512)
        else:
            xp = _ffn(xp, norm_ffn2[l], *f2, tm=1024, tf=256)
        outs_p.append((k_new.reshape(1, S, N_KV_HEADS, HEAD_DIM_C),
                       v_new.reshape(1, S, N_KV_HEADS, HEAD_DIM_C),
                       ki_new.reshape(1, S, IDX_DIM),
                       h_last.reshape(1, d_a), conv_new, s_last))

    stack = lambda outs, k: jnp.stack([o[k] for o in outs])
    return ((y_prompt.reshape(1, S, D), y_sample.reshape(Bs, tp, D)[:, :Ts])
            + tuple(stack(outs_p, k) for k in range(6))
            + tuple(stack(outs_s, k) for k in range(6)))
```

```python
import functools
import math

import numpy as np
import jax
import jax.numpy as jnp
from jax import lax
from jax.experimental import pallas as pl
from jax.experimental.pallas import tpu as pltpu

F32 = jnp.float32
BF16 = jnp.bfloat16
I32 = jnp.int32

EPS = 1e-6
N_A_BLOCKS = 4
CONV_WIDTH = 4
LRU_C = 8.0
N_RET_HEADS = 4
ROPE_BASE = 10000.0
HEAD_DIM_C = 128
N_KV_HEADS = 2
N_IDX_HEADS = 16
IDX_DIM = 64
TOPK_MAX = 256
NUM_BUCKETS = 32
MAX_DISTANCE = 128
SAMPLE_PAD_T = 8
PAGES_PER_STEP = 16
NEG_BIG = -1e30
INT_MIN = -2 ** 31
SELECT_BIT_GROUPS = (0, 21, 23, 25, 27, 29, 32)

VMEM_LIMIT = 56 * 1024 * 1024


def _cparams(sem):
    return pltpu.CompilerParams(dimension_semantics=sem, vmem_limit_bytes=VMEM_LIMIT)


def _rms(x, g):
    return x * lax.rsqrt(jnp.mean(x * x, axis=-1, keepdims=True) + EPS) * g


def _dot(a, b):
    return jnp.dot(a, b, preferred_element_type=F32)


def _dot_nt(a, b):
    return lax.dot_general(a, b, (((1,), (1,)), ((), ())), preferred_element_type=F32)


def _dot_tn(a, b):
    return lax.dot_general(a, b, (((0,), (0,)), ((), ())), preferred_element_type=F32)


def _sort_key(score):
    bits = lax.bitcast_convert_type(score, I32)
    return jnp.where(bits < 0, bits ^ jnp.int32(0x7FFFFFFF), bits)


def _ffn_body(x_ref, g_ref, wg_ref, wu_ref, wd_ref, *rest, final_norm):
    if final_norm:
        gf_ref, o_ref, on_ref, h_ref = rest
    else:
        o_ref, h_ref = rest
    j = pl.program_id(1)

    @pl.when(j == 0)
    def _():
        h_ref[...] = _rms(x_ref[...], g_ref[...]).astype(BF16)
        o_ref[...] = jnp.zeros_like(o_ref)

    h = h_ref[...]
    g = _dot(h, wg_ref[...])
    u = _dot(h, wu_ref[...])
    a = (g * jax.nn.sigmoid(g) * u).astype(BF16)
    o_ref[...] += _dot(a, wd_ref[...])

    @pl.when(j == pl.num_programs(1) - 1)
    def _():
        y = x_ref[...] + 0.5 * o_ref[...]
        o_ref[...] = y
        if final_norm:
            on_ref[...] = _rms(y, gf_ref[...])


def _ffn(x, g, wg, wu, wd, gf=None, *, tm, tf):
    R, D = x.shape
    FF = wg.shape[1]
    final_norm = gf is not None
    in_specs = [
        pl.BlockSpec((tm, D), lambda i, j: (i, 0)),
        pl.BlockSpec((1, D), lambda i, j: (0, 0)),
        pl.BlockSpec((D, tf), lambda i, j: (0, j)),
        pl.BlockSpec((D, tf), lambda i, j: (0, j)),
        pl.BlockSpec((tf, D), lambda i, j: (j, 0)),
    ]
    args = [x, g.reshape(1, D), wg, wu, wd]
    out_shape = [jax.ShapeDtypeStruct((R, D), F32)]
    out_specs = [pl.BlockSpec((tm, D), lambda i, j: (i, 0))]
    if final_norm:
        in_specs.append(pl.BlockSpec((1, D), lambda i, j: (0, 0)))
        args.append(gf.reshape(1, D))
        out_shape.append(jax.ShapeDtypeStruct((R, D), F32))
        out_specs.append(pl.BlockSpec((tm, D), lambda i, j: (i, 0)))
    res = pl.pallas_call(
        functools.partial(_ffn_body, final_norm=final_norm),
        grid=(R // tm, FF // tf),
        in_specs=in_specs,
        out_specs=out_specs,
        out_shape=out_shape,
        scratch_shapes=[pltpu.VMEM((tm, D), BF16)],
        compiler_params=_cparams(("parallel", "arbitrary")),
        name="ffn",
    )(*args)
    return res if final_norm else res[0]


def _ffn_cast_body(x_ref, g_ref, wg_ref, wu_ref, wd_ref, *rest, final_norm):
    if final_norm:
        gf_ref, o_ref, on_ref, wgb_ref, wub_ref, wdb_ref, h_ref = rest
    else:
        o_ref, wgb_ref, wub_ref, wdb_ref, h_ref = rest
    j = pl.program_id(0)

    @pl.when(j == 0)
    def _():
        h_ref[...] = _rms(x_ref[...], g_ref[...]).astype(BF16)
        o_ref[...] = jnp.zeros_like(o_ref)

    wg, wu, wd = wg_ref[0].astype(BF16), wu_ref[0].astype(BF16), wd_ref[0].astype(BF16)
    wgb_ref[...] = wg
    wub_ref[...] = wu
    wdb_ref[...] = wd
    h = h_ref[...]
    g = _dot(h, wg)
    a = (g * jax.nn.sigmoid(g) * _dot(h, wu)).astype(BF16)
    o_ref[...] += _dot(a, wd)

    @pl.when(j == pl.num_programs(0) - 1)
    def _():
        y = x_ref[...] + 0.5 * o_ref[...]
        o_ref[...] = y
        if final_norm:
            on_ref[...] = _rms(y, gf_ref[...])


def _ffn_cast(x, g, wg, wu, wd, layer, gf=None, *, tf):
    R, D = x.shape
    FF = wg.shape[2]
    final_norm = gf is not None
    in_specs = [
        pl.BlockSpec((R, D), lambda j: (0, 0)),
        pl.BlockSpec((1, D), lambda j: (0, 0)),
        pl.BlockSpec((1, D, tf), lambda j: (layer, 0, j)),
        pl.BlockSpec((1, D, tf), lambda j: (layer, 0, j)),
        pl.BlockSpec((1, tf, D), lambda j: (layer, j, 0)),
    ]
    args = [x, g.reshape(1, D), wg, wu, wd]
    out_shape = [jax.ShapeDtypeStruct((R, D), F32)]
    out_specs = [pl.BlockSpec((R, D), lambda j: (0, 0))]
    if final_norm:
        in_specs.append(pl.BlockSpec((1, D), lambda j: (0, 0)))
        args.append(gf.reshape(1, D))
        out_shape.append(jax.ShapeDtypeStruct((R, D), F32))
        out_specs.append(pl.BlockSpec((R, D), lambda j: (0, 0)))
    out_shape += [jax.ShapeDtypeStruct((D, FF), BF16), jax.ShapeDtypeStruct((D, FF), BF16),
                  jax.ShapeDtypeStruct((FF, D), BF16)]
    out_specs += [pl.BlockSpec((D, tf), lambda j: (0, j)), pl.BlockSpec((D, tf), lambda j: (0, j)),
                  pl.BlockSpec((tf, D), lambda j: (j, 0))]
    return pl.pallas_call(
        functools.partial(_ffn_cast_body, final_norm=final_norm),
        grid=(FF // tf,),
        in_specs=in_specs,
        out_specs=out_specs,
        out_shape=out_shape,
        scratch_shapes=[pltpu.VMEM((R, D), BF16)],
        compiler_params=_cparams(("arbitrary",)),
        name="ffn_cast",
    )(*args)


def _in_proj_body(x_ref, g_ref, w_ref, wt_ref, ua_ref, ub_ref, c1_ref, k_ref, v_ref, ki_ref, tail_ref, c2b_ref,
                  *, q_scale, cols):
    h = _rms(x_ref[...], g_ref[...]).astype(BF16)
    o_a, o_b, o_qc, o_kv, o_qi, o_end = cols
    kv_w = k_ref.shape[1]
    d_c = o_kv - o_qc
    ua_ref[...] = _dot(h, w_ref[0, :, o_a:o_b])
    ub_ref[...] = _dot(h, w_ref[0, :, o_b:o_qc])
    c1_ref[:, :d_c] = (_dot(h, w_ref[0, :, o_qc:o_kv]) * q_scale).astype(BF16)
    c1_ref[:, d_c:] = _dot(h, w_ref[0, :, o_qi:o_end]).astype(BF16)
    kv = _dot(h, w_ref[0, :, o_kv:o_qi])
    t = _dot(h, wt_ref[0])
    k_ref[...] = kv[:, :kv_w]
    v_ref[...] = kv[:, kv_w:]
    ki_ref[...] = t[:, :IDX_DIM]
    tail_ref[...] = t
    c2b_ref[:, :2 * kv_w] = kv.astype(BF16)
    c2b_ref[:, 2 * kv_w:] = t.astype(BF16)


def _in_proj(x, g, w_all, w_tail, layer, *, tm, cols, q_scale):
    R, D = x.shape
    o_a, o_b, o_qc, o_kv, o_qi, o_end = cols
    kv_w = N_KV_HEADS * HEAD_DIM_C
    tw = w_tail.shape[2]
    outs = [(o_b - o_a, F32), (o_qc - o_b, F32), (o_kv - o_qc + o_end - o_qi, BF16), (kv_w, F32), (kv_w, F32),
            (IDX_DIM, F32), (tw, F32), (2 * kv_w + tw, BF16)]
    return pl.pallas_call(
        functools.partial(_in_proj_body, q_scale=q_scale, cols=cols),
        grid=(R // tm,),
        in_specs=[pl.BlockSpec((tm, D), lambda i: (i, 0)), pl.BlockSpec((1, D), lambda i: (0, 0)),
                  pl.BlockSpec((1,) + w_all.shape[1:], lambda i: (layer, 0, 0), pipeline_mode=pl.Buffered(1)),
                  pl.BlockSpec((1,) + w_tail.shape[1:], lambda i: (layer, 0, 0), pipeline_mode=pl.Buffered(1))],
        out_specs=[pl.BlockSpec((tm, wd), lambda i: (i, 0)) for wd, _ in outs],
        out_shape=[jax.ShapeDtypeStruct((R, wd), dt) for wd, dt in outs],
        compiler_params=_cparams(("parallel",)),
        name="in_proj",
    )(x, g.reshape(1, D), w_all, w_tail)


def _oproj_body(x_ref, ya_ref, yb_ref, yc_ref, w_ref, o_ref):
    da, db = ya_ref.shape[1], yb_ref.shape[1]
    o_ref[...] = (x_ref[...] + _dot(ya_ref[...], w_ref[:da]) + _dot(yb_ref[...], w_ref[da:da + db])
                  + _dot(yc_ref[...], w_ref[da + db:]))


def _out_proj(x, ya, yb, yc, w, *, tm):
    R, D = x.shape
    row = lambda a: pl.BlockSpec((tm, a.shape[1]), lambda i: (i, 0))
    return pl.pallas_call(
        _oproj_body,
        grid=(R // tm,),
        in_specs=[row(x), row(ya), row(yb), row(yc),
                  pl.BlockSpec(w.shape, lambda i: (0, 0), pipeline_mode=pl.Buffered(1))],
        out_specs=row(x),
        out_shape=jax.ShapeDtypeStruct((R, D), F32),
        compiler_params=_cparams(("parallel",)),
        name="out_proj",
    )(x, ya, yb, yc, w)


def _lru_body(xa_ref, ga_ref, conv0_ref, h0_ref, cw_ref, cb_ref, wa_ref, ba_ref, wx_ref, bx_ref, lam_ref,
              ya_ref, hl_ref, cn_ref, xs_ref, a_ref, u_ref, hs_ref, hc_ref, *, tc, t_valid_last):
    t = pl.program_id(1)
    da = xa_ref.shape[-1]
    blk = da // N_A_BLOCKS
    tail = 8

    @pl.when(t == 0)
    def _():
        xs_ref[0:tail, :] = conv0_ref[0]
        hc_ref[0:1, :] = h0_ref[0]

    xs_ref[tail:tail + tc, :] = xa_ref[0]
    xc = cb_ref[...]
    for j in range(CONV_WIDTH):
        off = tail - (CONV_WIDTH - 1) + j
        xc = xc + xs_ref[off:off + tc, :] * cw_ref[j:j + 1, :]
    xcb = xc.astype(BF16)
    pre_r = jnp.concatenate([_dot(xcb[:, n * blk:(n + 1) * blk], wa_ref[n]) for n in range(N_A_BLOCKS)], axis=1)
    pre_i = jnp.concatenate([_dot(xcb[:, n * blk:(n + 1) * blk], wx_ref[n]) for n in range(N_A_BLOCKS)], axis=1)
    r = jax.nn.sigmoid(pre_r + ba_ref[...])
    gi = jax.nn.sigmoid(pre_i + bx_ref[...])
    z = -lam_ref[...]
    softplus = jnp.maximum(z, 0.0) + jnp.log1p(jnp.exp(-jnp.abs(z)))
    log_a = (-LRU_C) * r * softplus
    a = jnp.exp(log_a)
    u = jnp.sqrt(-jnp.tanh(log_a) * (a * a + 1.0)) * gi * xc

    row_in_group = lax.broadcasted_iota(I32, (tc, da), 0) % 8
    for s in (1, 2, 4):
        first = row_in_group < s
        a_lo = jnp.where(first, 1.0, pltpu.roll(a, s, 0))
        u_lo = jnp.where(first, 0.0, pltpu.roll(u, s, 0))
        u = a * u_lo + u
        a = a * a_lo
    a_ref[...] = a
    u_ref[...] = u

    def step(k, h):
        base = pl.multiple_of(k * 8, 8)
        hg = a_ref[pl.ds(base, 8), :] * h + u_ref[pl.ds(base, 8), :]
        hs_ref[pl.ds(base, 8), :] = hg
        return hg[7:8, :]

    h = lax.fori_loop(0, tc // 8, step, hc_ref[0:1, :])
    hc_ref[0:1, :] = h
    ya_ref[0] = (hs_ref[...] * jax.nn.gelu(ga_ref[0])).astype(ya_ref.dtype)

    @pl.when(t == pl.num_programs(1) - 1)
    def _():
        hl_ref[0] = hs_ref[t_valid_last - 1:t_valid_last, :]
        lo = tail - (CONV_WIDTH - 1) + t_valid_last
        cn_ref[0] = xs_ref[lo:lo + CONV_WIDTH - 1, :]

    xs_ref[0:tail, :] = xs_ref[tc:tc + tail, :]


def _lru(ua, conv0, h0, cw, cb, wa, ba, wx, bx, lam, *, tc, t_valid_last):
    B, T, da2 = ua.shape
    da = da2 // 2
    vec = lambda: pl.BlockSpec((1, da), lambda b, t: (0, 0))
    blk = da // N_A_BLOCKS
    return pl.pallas_call(
        functools.partial(_lru_body, tc=tc, t_valid_last=t_valid_last),
        grid=(B, T // tc),
        in_specs=[
            pl.BlockSpec((1, tc, da), lambda b, t: (b, t, 0)),
            pl.BlockSpec((1, tc, da), lambda b, t: (b, t, 1)),
            pl.BlockSpec((1, 8, da), lambda b, t: (b, 0, 0)),
            pl.BlockSpec((1, 1, da), lambda b, t: (b, 0, 0)),
            pl.BlockSpec((CONV_WIDTH, da), lambda b, t: (0, 0)),
            vec(),
            pl.BlockSpec((N_A_BLOCKS, blk, blk), lambda b, t: (0, 0, 0)),
            vec(),
            pl.BlockSpec((N_A_BLOCKS, blk, blk), lambda b, t: (0, 0, 0)),
            vec(),
            vec(),
        ],
        out_specs=[
            pl.BlockSpec((1, tc, da), lambda b, t: (b, t, 0)),
            pl.BlockSpec((1, 1, da), lambda b, t: (b, 0, 0)),
            pl.BlockSpec((1, CONV_WIDTH - 1, da), lambda b, t: (b, 0, 0)),
        ],
        out_shape=[
            jax.ShapeDtypeStruct((B, T, da), BF16),
            jax.ShapeDtypeStruct((B, 1, da), F32),
            jax.ShapeDtypeStruct((B, CONV_WIDTH - 1, da), F32),
        ],
        scratch_shapes=[
            pltpu.VMEM((tc + 8, da), F32),
            pltpu.VMEM((tc, da), F32),
            pltpu.VMEM((tc, da), F32),
            pltpu.VMEM((tc, da), F32),
            pltpu.VMEM((8, da), F32),
        ],
        compiler_params=_cparams(("arbitrary", "arbitrary")),
        name="rglru",
    )(ua, ua, conv0, h0, cw, cb.reshape(1, da), wa, ba.reshape(1, da), wx, bx.reshape(1, da), lam.reshape(1, da))


def _ret_body(q_ref, k_ref, v_ref, gb_ref, cos_ref, sin_ref, s0_ref, yb_ref, sl_ref, s_ref, *, cp, c_valid, mm_dtype):
    t = pl.program_id(1)
    dk = q_ref.shape[-1] // N_RET_HEADS
    dv = v_ref.shape[-1] // N_RET_HEADS

    @pl.when(t == 0)
    def _():
        s_ref[...] = s0_ref[0]

    cos = cos_ref[...]
    sin = sin_ref[...]
    first_half = (lax.broadcasted_iota(I32, cos.shape, 1) % dk) < (dk // 2)
    width = cos.shape[1]

    def rot(x):
        partner = jnp.where(first_half, pltpu.roll(x, width - dk // 2, 1), pltpu.roll(x, dk // 2, 1))
        return x * cos + partner * sin

    row = lax.broadcasted_iota(I32, (cp, 1), 0)
    q_all = rot(q_ref[0])
    k_all = rot(k_ref[0]) * (dk ** -0.5)
    ri = lax.broadcasted_iota(I32, (cp, cp), 0)
    ci = lax.broadcasted_iota(I32, (cp, cp), 1)
    diff = (ri - ci).astype(F32)
    rowf = row.astype(F32)
    for h in range(N_RET_HEADS):
        lg = math.log1p(-(2.0 ** (-5.0 - h)))
        dmask = jnp.where(diff >= 0, jnp.exp(jnp.maximum(diff, 0.0) * lg), 0.0)
        cross_dec = jnp.exp((rowf + 1.0) * lg)
        state_dec = jnp.exp((c_valid - 1.0 - rowf) * lg)
        chunk_dec = math.exp(c_valid * lg)
        state = s_ref[h]
        for n in range(q_all.shape[0] // cp):
            rows = slice(n * cp, (n + 1) * cp)
            qh = q_all[rows, h * dk:(h + 1) * dk].astype(mm_dtype)
            kh = k_all[rows, h * dk:(h + 1) * dk]
            if c_valid < cp:
                kh = jnp.where(row < c_valid, kh, 0.0)
            vh = v_ref[0, rows, h * dv:(h + 1) * dv].astype(mm_dtype)
            att = _dot_nt(qh, kh.astype(mm_dtype)) * dmask
            inner = _dot(att.astype(mm_dtype), vh)
            cross = _dot(qh, state.astype(mm_dtype)) * cross_dec
            state = state * chunk_dec + _dot_tn((kh * state_dec).astype(mm_dtype), vh)
            o = inner + cross
            o = o * lax.rsqrt(jnp.mean(o * o, axis=-1, keepdims=True) + EPS)
            gh = gb_ref[0, rows, h * dv:(h + 1) * dv]
            yb_ref[0, rows, h * dv:(h + 1) * dv] = (o * (gh * jax.nn.sigmoid(gh))).astype(yb_ref.dtype)
        s_ref[h] = state

    @pl.when(t == pl.num_programs(1) - 1)
    def _():
        sl_ref[0] = s_ref[...]


def _retention(ub, cos, sin, s0, *, cp, c_valid, mm_dtype, chunks_per_step=1):
    B, T, wtot = ub.shape
    w = wtot // 6
    H, dk, dv = s0.shape[1:]
    blk = cp * chunks_per_step
    return pl.pallas_call(
        functools.partial(_ret_body, cp=cp, c_valid=c_valid, mm_dtype=mm_dtype),
        grid=(B, T // blk),
        in_specs=[
            pl.BlockSpec((1, blk, w), lambda b, t: (b, t, 0)),
            pl.BlockSpec((1, blk, w), lambda b, t: (b, t, 1)),
            pl.BlockSpec((1, blk, 2 * w), lambda b, t: (b, t, 1)),
            pl.BlockSpec((1, blk, 2 * w), lambda b, t: (b, t, 2)),
            pl.BlockSpec((blk, w), lambda b, t: (t, 0)),
            pl.BlockSpec((blk, w), lambda b, t: (t, 0)),
            pl.BlockSpec((1, H, dk, dv), lambda b, t: (b, 0, 0, 0)),
        ],
        out_specs=[
            pl.BlockSpec((1, blk, 2 * w), lambda b, t: (b, t, 0)),
            pl.BlockSpec((1, H, dk, dv), lambda b, t: (b, 0, 0, 0)),
        ],
        out_shape=[
            jax.ShapeDtypeStruct((B, T, 2 * w), BF16),
            jax.ShapeDtypeStruct((B, H, dk, dv), F32),
        ],
        scratch_shapes=[pltpu.VMEM((H, dk, dv), F32)],
        compiler_params=_cparams(("arbitrary", "arbitrary")),
        name="retention",
    )(ub, ub, ub, ub, cos, sin, s0)


def _rope_tables(pos, dk, heads):
    half = dk // 2
    freqs = ROPE_BASE ** (-jnp.arange(half, dtype=F32) / half)
    ang = pos.astype(F32)[:, None] * freqs[None, :]
    cos, sin = jnp.cos(ang), jnp.sin(ang)
    cos_t = jnp.tile(jnp.concatenate([cos, cos], axis=1), (1, heads))
    sin_t = jnp.tile(jnp.concatenate([-sin, sin], axis=1), (1, heads))
    return cos_t, sin_t


def _t5_bucket_np(dist):
    n = np.maximum(dist, 0)
    max_exact = NUM_BUCKETS // 2
    ratio = np.log(np.maximum(n, 1).astype(np.float32) / np.float32(max_exact)) / np.float32(math.log(MAX_DISTANCE / max_exact))
    large = max_exact + (ratio * np.float32(NUM_BUCKETS - max_exact)).astype(np.int32)
    large = np.minimum(large, NUM_BUCKETS - 1)
    return np.where(n < max_exact, n, large).astype(np.int32)


def _bias_from_buckets(bucket, rb_ref, head):
    def step(b, out):
        return jnp.where(bucket == b, rb_ref[b, head], out)

    return lax.fori_loop(0, NUM_BUCKETS, step, jnp.zeros(bucket.shape, F32))


def _dsa_prompt_body(rb_ref, qc_ref, qi_ref, wi_ref, kvb_ref, btab_ref, o_ref,
                     keys_ref, bt_ref, wf_ref, qis_ref, qs_ref, mx_ref, mrep_ref, acc_ref, cut_ref, kmax_ref, sel_ref,
                     *, tq, n_keep, idx_bits):
    i = pl.program_id(0)
    G = qc_ref.shape[1] // (N_KV_HEADS * HEAD_DIM_C)
    n_heads = N_KV_HEADS * G
    kcol, vcol, icol = 0, N_KV_HEADS * HEAD_DIM_C, 2 * N_KV_HEADS * HEAD_DIM_C
    lanes = HEAD_DIM_C
    halves = tq // lanes
    n_tiles = i + 1

    @pl.when(i == 0)
    def _():
        for h in range(n_heads):
            far = rb_ref[NUM_BUCKETS - 1, h]
            for r in range(2):
                bt_ref[h // G, r, h % G] = _bias_from_buckets(btab_ref[r], rb_ref, h) - far

    for h in range(N_IDX_HEADS):
        qis_ref[h] = qi_ref[:, h * IDX_DIM:(h + 1) * IDX_DIM]
        wf_ref[h] = jnp.broadcast_to(wi_ref[:, IDX_DIM + h:IDX_DIM + h + 1], (tq, lanes))
    for c in range(N_KV_HEADS):
        for g in range(G):
            h = c * G + g
            qs_ref[c, g * tq:(g + 1) * tq, :] = qc_ref[:, h * HEAD_DIM_C:(h + 1) * HEAD_DIM_C]

    rowi = lax.broadcasted_iota(I32, (tq, tq), 0)
    coli = lax.broadcasted_iota(I32, (tq, tq), 1)

    def key_index(j):
        return j * tq + coli

    def visible(j):
        return key_index(j) <= (i * tq + rowi)

    def score_tile(j, carry):
        ki = kvb_ref[pl.ds(pl.multiple_of(j * tq, tq), tq), icol:icol + IDX_DIM]
        for rh in range(halves):
            rows = slice(rh * lanes, (rh + 1) * lanes)
            acc = jnp.zeros((lanes, tq), F32)
            for h in range(N_IDX_HEADS):
                w = wf_ref[h, rows, :]
                acc = acc + jnp.maximum(_dot_nt(qis_ref[h, rows, :], ki), 0.0) * jnp.concatenate([w] * halves, axis=1)
            score = acc * (N_IDX_HEADS ** -0.5 * IDX_DIM ** -0.5)
            score = jnp.where(visible(j)[rows], score, -jnp.inf)
            keys_ref[j, :, rows] = _sort_key(score).T
        return carry

    lax.fori_loop(0, n_tiles, score_tile, 0)

    sub = 8
    grp = lax.broadcasted_iota(I32, (tq // sub, sub, tq), 0)
    srow = lax.broadcasted_iota(I32, (tq // sub, sub, tq), 1)

    def count(pred, *row_args):
        args = [a[None] for a in row_args]

        def one(j, cnt):
            k = keys_ref[j].reshape(tq // sub, sub, tq)
            idx = j * tq + grp * sub + srow
            return cnt + jnp.sum(jnp.where(pred(k, idx, *args), 1, 0), axis=0)

        n_pairs = n_tiles // 2
        cnt = lax.fori_loop(0, n_pairs, lambda p, c: one(2 * p + 1, one(2 * p, c)), jnp.zeros((sub, tq), I32))
        cnt = lax.fori_loop(2 * n_pairs, n_tiles, one, cnt)
        return jnp.broadcast_to(jnp.sum(cnt, axis=0, keepdims=True), (sub, tq))

    def bit_step(b, st):
        tau, n_ge = st
        cand = tau + lax.shift_left(jnp.int32(1), 31 - b)
        c = count(lambda k, idx, cnd: k >= cnd, cand)
        ge = c >= n_keep
        return jnp.where(ge, cand, tau), jnp.where(ge, c, n_ge)

    sel_ref[0] = jnp.full((sub, tq), INT_MIN, I32)
    sel_ref[1] = jnp.broadcast_to(n_tiles * tq, (sub, tq)).astype(I32)
    for b0, b1 in zip(SELECT_BIT_GROUPS[:-1], SELECT_BIT_GROUPS[1:]):
        @pl.when(jnp.max(sel_ref[1]) > n_keep)
        def _():
            tau_g, n_ge_g = lax.fori_loop(b0, b1, bit_step, (sel_ref[0], sel_ref[1]))
            sel_ref[0] = tau_g
            sel_ref[1] = n_ge_g

    tau, n_ge = sel_ref[0], sel_ref[1]

    cut_ref[...] = jnp.full((sub, tq), 2 ** 31 - 1, I32)

    @pl.when(jnp.max(n_ge) > n_keep)
    def _():
        need = n_keep - count(lambda k, idx, t: k > t, tau)

        def idx_step(b, cut):
            cand = cut + lax.shift_left(jnp.int32(1), idx_bits - 1 - b)
            below = count(lambda k, idx, t, c: (k == t) & (idx < c), tau, cand)
            return jnp.where(below < need, cand, cut)

        cut_ref[...] = lax.fori_loop(0, idx_bits, idx_step, jnp.zeros((sub, tq), I32))

    tau_t = jnp.broadcast_to(tau[0:1], (tq, tq)).T
    cut_t = jnp.broadcast_to(cut_ref[0:1, :], (tq, tq)).T

    def mask_bias(j, r):
        k = keys_ref[j].T
        sel = (k > tau_t) | ((k == tau_t) & (key_index(j) <= cut_t))
        if r is not None:
            sel = sel & visible(j)
        return jnp.where(sel, 0.0, NEG_BIG)

    def logits(j, c, r, mb):
        base = pl.multiple_of(j * tq, tq)
        kc = kvb_ref[pl.ds(base, tq), kcol + c * HEAD_DIM_C:kcol + (c + 1) * HEAD_DIM_C]
        s = _dot_nt(qs_ref[c], kc).reshape(G, tq, tq) + mb[None]
        if r is not None:
            s = s + bt_ref[c, r]
        return s

    def max_tile(j, r):
        mb = mask_bias(j, r)
        for c in range(N_KV_HEADS):
            s = logits(j, c, r, mb)
            m = s[..., 0:lanes]
            for a in range(1, halves):
                m = jnp.maximum(m, s[..., a * lanes:(a + 1) * lanes])
            mx_ref[c] = jnp.maximum(mx_ref[c], m)

    def acc_tile(j, r):
        base = pl.multiple_of(j * tq, tq)
        mb = mask_bias(j, r)
        for c in range(N_KV_HEADS):
            vc = kvb_ref[pl.ds(base, tq), vcol + c * HEAD_DIM_C:vcol + (c + 1) * HEAD_DIM_C]
            vext = jnp.concatenate([vc, jnp.ones((tq, lanes), BF16)], axis=1)
            m = mrep_ref[c]
            p = jnp.exp(logits(j, c, r, mb) - jnp.concatenate([m] * halves, axis=-1))
            acc_ref[c] += _dot(p.reshape(G * tq, tq).astype(BF16), vext)

    def sweep(tile_fn):
        def far(j, carry):
            tile_fn(j, None)
            return carry

        lax.fori_loop(0, jnp.maximum(i - 1, 0), far, 0)

        @pl.when(i >= 1)
        def _():
            tile_fn(i - 1, 1)

        tile_fn(i, 0)

    @pl.when(i == 0)
    def _():
        kmax_ref[...] = jnp.zeros(kmax_ref.shape, F32)

    for c in range(N_KV_HEADS):
        kt = kvb_ref[pl.ds(pl.multiple_of(i * tq, tq), tq), kcol + c * HEAD_DIM_C:kcol + (c + 1) * HEAD_DIM_C].astype(F32)
        knorm = jnp.sqrt(jnp.max(jnp.sum(kt * kt, axis=1, keepdims=True)))
        kmax_ref[c] = jnp.maximum(kmax_ref[c], knorm)
        qf = qs_ref[c].astype(F32)
        qnorm = jnp.sqrt(jnp.sum(qf * qf, axis=1, keepdims=True)).reshape(G, tq, 1)
        for g in range(G):
            h = c * G + g
            far = rb_ref[NUM_BUCKETS - 1, h]
            bias_max = lax.fori_loop(0, NUM_BUCKETS, lambda b, m: jnp.maximum(m, rb_ref[b, h] - far), jnp.float32(0.0))
            mrep_ref[c, g] = qnorm[g] * kmax_ref[c, 0:1, :] + bias_max

    acc_ref[...] = jnp.zeros(acc_ref.shape, F32)
    sweep(acc_tile)

    @pl.when(jnp.logical_not(jnp.min(acc_ref[:, :, lanes:]) > 1e-30))
    def _():
        mx_ref[...] = jnp.full(mx_ref.shape, NEG_BIG, F32)
        sweep(max_tile)
        mrep_ref[...] = jnp.broadcast_to(jnp.max(mx_ref[...], axis=-1, keepdims=True), mrep_ref.shape)
        acc_ref[...] = jnp.zeros(acc_ref.shape, F32)
        sweep(acc_tile)

    for c in range(N_KV_HEADS):
        for g in range(G):
            h = c * G + g
            a = acc_ref[c, g * tq:(g + 1) * tq, :]
            o_ref[:, h * HEAD_DIM_C:(h + 1) * HEAD_DIM_C] = (a[:, :lanes] / a[:, lanes:]).astype(o_ref.dtype)


def _dsa_prompt(qcqi, tail, kvi_bf16, rel_bias, *, tq, n_keep):
    S = qcqi.shape[0]
    dc = qcqi.shape[1] - N_IDX_HEADS * IDX_DIM
    n_heads = dc // HEAD_DIM_C
    G = n_heads // N_KV_HEADS
    wkv = kvi_bf16.shape[1]
    d = np.arange(tq)[:, None] - np.arange(tq)[None, :]
    btab = jnp.asarray(np.stack([_t5_bucket_np(d + r * tq) for r in range(2)]))
    assert _t5_bucket_np(np.array([tq + 1]))[0] == NUM_BUCKETS - 1
    lanes = HEAD_DIM_C
    grid_spec = pltpu.PrefetchScalarGridSpec(
        num_scalar_prefetch=0,
        grid=(S // tq,),
        in_specs=[
            pl.BlockSpec(memory_space=pltpu.SMEM),
            pl.BlockSpec((tq, dc), lambda i: (i, 0)),
            pl.BlockSpec((tq, N_IDX_HEADS * IDX_DIM), lambda i: (i, dc // (N_IDX_HEADS * IDX_DIM))),
            pl.BlockSpec((tq, tail.shape[1]), lambda i: (i, 0)),
            pl.BlockSpec((S, wkv), lambda i: (0, 0), pipeline_mode=pl.Buffered(1)),
            pl.BlockSpec((2, tq, tq), lambda i: (0, 0, 0)),
        ],
        out_specs=pl.BlockSpec((tq, dc), lambda i: (i, 0)),
        scratch_shapes=[
            pltpu.VMEM((S // tq, tq, tq), I32),
            pltpu.VMEM((N_KV_HEADS, 2, G, tq, tq), F32),
            pltpu.VMEM((N_IDX_HEADS, tq, lanes), F32),
            pltpu.VMEM((N_IDX_HEADS, tq, IDX_DIM), BF16),
            pltpu.VMEM((N_KV_HEADS, G * tq, HEAD_DIM_C), BF16),
            pltpu.VMEM((N_KV_HEADS, G, tq, lanes), F32),
            pltpu.VMEM((N_KV_HEADS, G, tq, lanes), F32),
            pltpu.VMEM((N_KV_HEADS, G * tq, 2 * lanes), F32),
            pltpu.VMEM((8, tq), I32),
            pltpu.VMEM((N_KV_HEADS, 8, lanes), F32),
            pltpu.VMEM((2, 8, tq), I32),
        ],
    )
    return pl.pallas_call(
        functools.partial(_dsa_prompt_body, tq=tq, n_keep=n_keep, idx_bits=(S - 1).bit_length()),
        grid_spec=grid_spec,
        out_shape=jax.ShapeDtypeStruct((S, dc), BF16),
        compiler_params=_cparams(("arbitrary",)),
        name="dsa_prompt",
    )(rel_bias, qcqi, qcqi, tail, kvi_bf16, btab)


def _dsa_sel_body(pt_ref, qi_ref, w_ref, kin_ref, *rest, t_new, n_keep):
    pages = rest[:PAGES_PER_STEP]
    mb_ref, keys_ref, arg_ref, cnt_ref = rest[PAGES_PER_STEP:]
    b = pl.program_id(0)
    p = pl.program_id(1)
    B, n_chunks, tp, P = keys_ref.shape
    last = p == pl.num_programs(1) - 1
    q = qi_ref[0]
    w = w_ref[0]
    rowi = lax.broadcasted_iota(I32, (tp, P), 0)
    coli = lax.broadcasted_iota(I32, (tp, P), 1)
    vis_new = (coli <= rowi) & (coli < t_new)

    def chunk_scores(s):
        s = jnp.maximum(s, 0.0) * w
        s = s.reshape(tp, N_IDX_HEADS, s.shape[-1]).sum(axis=1)
        return s * (N_IDX_HEADS ** -0.5 * IDX_DIM ** -0.5)

    for r in range(PAGES_PER_STEP):
        keys_ref[b, p * PAGES_PER_STEP + r] = _sort_key(chunk_scores(_dot(q, pages[r][0].astype(BF16))))

    @pl.when(last)
    def _():
        s_new = jnp.where(vis_new, chunk_scores(_dot_nt(q, kin_ref[0])), -jnp.inf)
        keys_ref[b, n_chunks - 1] = _sort_key(s_new)

    @pl.when(last & (b == B - 1))
    def _():
        key_index = (lax.broadcasted_iota(I32, (n_chunks, tp, P), 0) * P
                     + lax.broadcasted_iota(I32, (n_chunks, tp, P), 2))

        def count(pred, *row_args):
            for n, a in enumerate(row_args):
                arg_ref[n] = a

            def per_seq(s, carry):
                args = [arg_ref[n, s][None] for n in range(len(row_args))]
                cnt_ref[s] = jnp.sum(jnp.where(pred(keys_ref[s], key_index, *args), 1, 0), axis=0)
                return carry

            lax.fori_loop(0, B, per_seq, 0)
            return jnp.broadcast_to(jnp.sum(cnt_ref[...], axis=-1, keepdims=True), (B, tp, P))

        def bit_step(bit, tau):
            cand = tau + lax.shift_left(jnp.int32(1), 31 - bit)
            return jnp.where(count(lambda k, idx, c: k >= c, cand) >= n_keep, cand, tau)

        tau = lax.fori_loop(0, 32, bit_step, jnp.full((B, tp, P), INT_MIN, I32))

        arg_ref[2] = jnp.full((B, tp, P), 2 ** 31 - 1, I32)

        @pl.when(jnp.max(count(lambda k, idx, t: k >= t, tau)) > n_keep)
        def _():
            need = n_keep - count(lambda k, idx, t: k > t, tau)
            idx_bits = (n_chunks * P - 1).bit_length()

            def idx_step(bit, cut):
                cand = cut + lax.shift_left(jnp.int32(1), idx_bits - 1 - bit)
                below = count(lambda k, idx, t, c: (k == t) & (idx < c), tau, cand)
                return jnp.where(below < need, cand, cut)

            arg_ref[2] = lax.fori_loop(0, idx_bits, idx_step, jnp.zeros((B, tp, P), I32))

        arg_ref[0] = tau

        def write_mask(s, carry):
            k = keys_ref[s]
            t = arg_ref[0, s][None]
            sel = (k > t) | ((k == t) & (key_index <= arg_ref[2, s][None]))
            mb_ref[s] = jnp.where(sel, 0.0, NEG_BIG)
            mb_ref[s, n_chunks - 1] = jnp.where(sel[n_chunks - 1] & vis_new, 0.0, NEG_BIG)
            return carry

        lax.fori_loop(0, B, write_mask, 0)


def _dsa_sample_select(pt_flat, qi, w, ki_new, pool_ki, *, n_pages, t_new, n_keep):
    B = qi.shape[0]
    P = pool_ki.shape[2]
    steps = n_pages // PAGES_PER_STEP

    def page_spec(r):
        return pl.BlockSpec((1, IDX_DIM, P), lambda b, p, pt: (pt[b * n_pages + p * PAGES_PER_STEP + r], 0, 0))

    grid_spec = pltpu.PrefetchScalarGridSpec(
        num_scalar_prefetch=1,
        grid=(B, steps),
        in_specs=[
            pl.BlockSpec((1,) + qi.shape[1:], lambda b, p, pt: (b, 0, 0)),
            pl.BlockSpec((1,) + w.shape[1:], lambda b, p, pt: (b, 0, 0)),
            pl.BlockSpec((1,) + ki_new.shape[1:], lambda b, p, pt: (b, 0, 0)),
        ] + [page_spec(r) for r in range(PAGES_PER_STEP)],
        out_specs=pl.BlockSpec((B, n_pages + 1, SAMPLE_PAD_T, P), lambda b, p, pt: (0, 0, 0, 0)),
        scratch_shapes=[
            pltpu.VMEM((B, n_pages + 1, SAMPLE_PAD_T, P), I32),
            pltpu.VMEM((3, B, SAMPLE_PAD_T, P), I32),
            pltpu.VMEM((B, SAMPLE_PAD_T, P), I32),
        ],
    )
    return pl.pallas_call(
        functools.partial(_dsa_sel_body, t_new=t_new, n_keep=n_keep),
        grid_spec=grid_spec,
        out_shape=jax.ShapeDtypeStruct((B, n_pages + 1, SAMPLE_PAD_T, P), F32),
        compiler_params=_cparams(("arbitrary", "arbitrary")),
        name="dsa_sample_select",
    )(pt_flat, qi, w, ki_new, *([pool_ki] * PAGES_PER_STEP))


def _dsa_att_body(pt_ref, rb_ref, q_ref, mb_ref, kn_ref, vn_ref, btab_ref, *rest):
    kp = rest[:PAGES_PER_STEP]
    vp = rest[PAGES_PER_STEP:2 * PAGES_PER_STEP]
    o_ref, bt_ref, m_ref, acc_ref = rest[2 * PAGES_PER_STEP:]
    b = pl.program_id(0)
    p = pl.program_id(1)
    n_chunks = mb_ref.shape[1]
    tp = SAMPLE_PAD_T
    G = q_ref.shape[2] // tp
    rows = G * tp
    P = kn_ref.shape[1]
    lanes = HEAD_DIM_C
    last = p == pl.num_programs(1) - 1

    @pl.when((b == 0) & (p == 0))
    def _():
        for c in range(N_KV_HEADS):
            for kind in range(2):
                for g in range(G):
                    h = c * G + g
                    bt_ref[c, kind, g * tp:(g + 1) * tp, :] = (_bias_from_buckets(btab_ref[kind], rb_ref, h)
                                                              - rb_ref[NUM_BUCKETS - 1, h])

    @pl.when(p == 0)
    def _():
        m_ref[...] = jnp.full(m_ref.shape, NEG_BIG, F32)
        acc_ref[...] = jnp.zeros(acc_ref.shape, F32)

    def attend(c, s, vext):
        m_old = m_ref[c]
        m_new = jnp.maximum(m_old, jnp.max(s, axis=1, keepdims=True))
        alpha = jnp.exp(m_old - m_new)
        pr = jnp.exp(s - jnp.concatenate([m_new] * (s.shape[1] // lanes), axis=1))
        acc_ref[c] = jnp.concatenate([alpha, alpha], axis=1) * acc_ref[c] + _dot(pr.astype(BF16), vext)
        m_ref[c] = m_new

    ones = jnp.ones((P, lanes), BF16)
    mb_step = jnp.concatenate([mb_ref[0, p * PAGES_PER_STEP + r] for r in range(PAGES_PER_STEP)], axis=1)
    mb_step = jnp.concatenate([mb_step] * G, axis=0)
    for c in range(N_KV_HEADS):
        head_rows = pl.ds(c, P, stride=N_KV_HEADS)
        k_all = jnp.concatenate([kp[r][0, head_rows, :].astype(BF16) for r in range(PAGES_PER_STEP)], axis=0)
        v_all = jnp.concatenate(
            [jnp.concatenate([vp[r][0, head_rows, :].astype(BF16), ones], axis=1) for r in range(PAGES_PER_STEP)], axis=0)
        s = _dot_nt(q_ref[0, c], k_all) + mb_step
        s = jnp.concatenate([s[:, :-P], s[:, -P:] + jnp.where(last, bt_ref[c, 0], 0.0)], axis=1)
        attend(c, s, v_all)

    @pl.when(last)
    def _():
        mb_new = jnp.concatenate([mb_ref[0, n_chunks - 1]] * G, axis=0)
        for c in range(N_KV_HEADS):
            cols = slice(c * HEAD_DIM_C, (c + 1) * HEAD_DIM_C)
            s = _dot_nt(q_ref[0, c], kn_ref[0, :, cols]) + mb_new + bt_ref[c, 1]
            attend(c, s, jnp.concatenate([vn_ref[0, :, cols], ones], axis=1))
            a = acc_ref[c]
            o_ref[0, c] = a[:, :lanes] / a[:, lanes:]


def _dsa_sample_attend(pt_flat, rel_bias, q, mb, k_new, v_new, pool_k, pool_v, *, n_pages, past):
    B, kvh, rows, dh = q.shape
    P = pool_k.shape[1] // kvh
    steps = n_pages // PAGES_PER_STEP
    tp = SAMPLE_PAD_T
    t = np.arange(tp)[:, None]
    col = np.arange(P)[None, :]
    assert past - (n_pages - 1) * P >= MAX_DISTANCE
    last_page = _t5_bucket_np(past + t - ((n_pages - 1) * P + col))
    new = _t5_bucket_np(t - col)
    btab = jnp.asarray(np.stack([last_page, new]))

    def page_spec(r):
        return pl.BlockSpec((1, P * kvh, dh), lambda b, p, pt: (pt[b * n_pages + p * PAGES_PER_STEP + r], 0, 0))

    grid_spec = pltpu.PrefetchScalarGridSpec(
        num_scalar_prefetch=1,
        grid=(B, steps),
        in_specs=[
            pl.BlockSpec(memory_space=pltpu.SMEM),
            pl.BlockSpec((1, kvh, rows, dh), lambda b, p, pt: (b, 0, 0, 0)),
            pl.BlockSpec((1,) + mb.shape[1:], lambda b, p, pt: (b, 0, 0, 0)),
            pl.BlockSpec((1,) + k_new.shape[1:], lambda b, p, pt: (b, 0, 0)),
            pl.BlockSpec((1,) + v_new.shape[1:], lambda b, p, pt: (b, 0, 0)),
            pl.BlockSpec((2, tp, P), lambda b, p, pt: (0, 0, 0)),
        ] + [page_spec(r) for r in range(PAGES_PER_STEP)] * 2,
        out_specs=pl.BlockSpec((1, kvh, rows, dh), lambda b, p, pt: (b, 0, 0, 0)),
        scratch_shapes=[
            pltpu.VMEM((kvh, 2, rows, P), F32),
            pltpu.VMEM((kvh, rows, dh), F32),
            pltpu.VMEM((kvh, rows, 2 * dh), F32),
        ],
    )
    return pl.pallas_call(
        _dsa_att_body,
        grid_spec=grid_spec,
        out_shape=jax.ShapeDtypeStruct((B, kvh, rows, dh), F32),
        compiler_params=_cparams(("arbitrary", "arbitrary")),
        name="dsa_sample_attend",
    )(pt_flat, rel_bias, q, mb, k_new, v_new, btab, *([pool_k] * PAGES_PER_STEP), *([pool_v] * PAGES_PER_STEP))


def _dsa_sample(c1, tail, c2b, pt_flat, rel_bias, pool_k, pool_v, pool_ki, *, Bs, Ts, n_pages):
    tp = SAMPLE_PAD_T
    kv_w = N_KV_HEADS * HEAD_DIM_C
    d_c = c1.shape[1] - N_IDX_HEADS * IDX_DIM
    G = d_c // kv_w
    P = pool_ki.shape[2]
    past = n_pages * P
    qi_s = c1[:, d_c:].reshape(Bs, tp * N_IDX_HEADS, IDX_DIM)
    w_s = tail[:, IDX_DIM:IDX_DIM + N_IDX_HEADS].reshape(Bs, tp * N_IDX_HEADS, 1)
    new_rows = jnp.pad(c2b.reshape(Bs, tp, -1), ((0, 0), (0, P - tp), (0, 0)))
    mb = _dsa_sample_select(pt_flat, qi_s, w_s, new_rows[:, :, 2 * kv_w:2 * kv_w + IDX_DIM], pool_ki,
                            n_pages=n_pages, t_new=Ts, n_keep=min(TOPK_MAX, (past + Ts) // 4))
    q_s = c1[:, :d_c].reshape(Bs, tp, N_KV_HEADS, G, HEAD_DIM_C).transpose(0, 2, 3, 1, 4)
    q_s = q_s.reshape(Bs, N_KV_HEADS, G * tp, HEAD_DIM_C)
    o_s = _dsa_sample_attend(pt_flat, rel_bias, q_s, mb, new_rows[:, :, :kv_w], new_rows[:, :, kv_w:2 * kv_w],
                             pool_k, pool_v, n_pages=n_pages, past=past)
    return o_s.reshape(Bs, N_KV_HEADS, G, tp, HEAD_DIM_C).transpose(0, 3, 1, 2, 4).reshape(Bs * tp, d_c).astype(BF16)


def kernel(x_prompt, x_sample, cache_k, cache_v, cache_kidx, page_table, state_lru_h, state_conv, state_ret,
           norm_ffn1, ffn1_gate, ffn1_up, ffn1_down, norm_mix, w_in, w_out, conv_w, conv_b,
           lru_wa, lru_ba, lru_wx, lru_bx, lru_lambda, rel_bias,
           norm_ffn2, ffn2_gate, ffn2_up, ffn2_down, norm_final):
    depth = norm_ffn1.shape[0]
    _, S, D = x_prompt.shape
    Bs, Ts, _ = x_sample.shape
    n_pool, P = cache_k.shape[1], cache_k.shape[2]
    n_pages = page_table.shape[1]
    past = n_pages * P
    d_a = state_lru_h.shape[-1]
    H_r, rdk, rdv = state_ret.shape[2:]
    d_b = H_r * rdv
    d_c = D - d_a - d_b
    kv_w = N_KV_HEADS * HEAD_DIM_C
    qi_w = N_IDX_HEADS * IDX_DIM
    G = d_c // HEAD_DIM_C // N_KV_HEADS
    tp = SAMPLE_PAD_T
    Rs = Bs * tp

    xp = x_prompt.reshape(S, D)
    xs = jnp.pad(x_sample, ((0, 0), (0, tp - Ts), (0, 0))).reshape(Rs, D)

    cos_p, sin_p = _rope_tables(jnp.arange(S, dtype=I32), rdk, H_r)
    cos_s, sin_s = _rope_tables(past + jnp.arange(tp, dtype=I32), rdk, H_r)

    pool_k = cache_k.reshape(depth * n_pool, P * N_KV_HEADS, HEAD_DIM_C)
    pool_v = cache_v.reshape(depth * n_pool, P * N_KV_HEADS, HEAD_DIM_C)
    pool_ki = jnp.swapaxes(cache_kidx, 2, 3).reshape(depth * n_pool, IDX_DIM, P)

    o = np.cumsum([0, 2 * d_a, 2 * H_r * rdk + 2 * d_b, d_c, 2 * kv_w, qi_w, IDX_DIM + N_IDX_HEADS])
    proj_cols = tuple(int(c) for c in o[:6])
    w_in_b = w_in.astype(BF16)
    w_in_tail = jnp.pad(w_in_b[:, :, o[5]:o[6]], ((0, 0), (0, 0), (0, 128 - int(o[6] - o[5]))))

    zeros_conv = jnp.zeros((1, 8, d_a), F32)
    zeros_h = jnp.zeros((1, 1, d_a), F32)
    zeros_s = jnp.zeros((1, H_r, rdk, rdv), F32)

    outs_p, outs_s = [], []
    y_prompt = y_sample = None
    for l in range(depth):
        in_proj = functools.partial(_in_proj, g=norm_mix[l], w_all=w_in_b, w_tail=w_in_tail, layer=l,
                                    cols=proj_cols, q_scale=HEAD_DIM_C ** -0.5)
        wo = w_out[l].astype(BF16)
        lwa, lwx = lru_wa[l].astype(BF16), lru_wx[l].astype(BF16)
        last = l == depth - 1
        pt_flat = (page_table + l * n_pool).reshape(-1).astype(I32)

        xs, *f1 = _ffn_cast(xs, norm_ffn1[l], ffn1_gate, ffn1_up, ffn1_down, l, tf=512)
        ua, ub, c1, k_new, v_new, ki_new, tail, c2b = in_proj(xs, tm=Rs)
        conv0 = jnp.pad(state_conv[l], ((0, 0), (8 - (CONV_WIDTH - 1), 0), (0, 0)))
        ya, h_last, conv_new = _lru(ua.reshape(Bs, tp, 2 * d_a), conv0, state_lru_h[l].reshape(Bs, 1, d_a),
                                    conv_w[l], conv_b[l], lwa, lru_ba[l], lwx, lru_bx[l], lru_lambda[l],
                                    tc=tp, t_valid_last=Ts)
        yb, s_last = _retention(ub.reshape(Bs, tp, -1), cos_s, sin_s, state_ret[l], cp=tp, c_valid=Ts, mm_dtype=F32)
        yc = _dsa_sample(c1, tail, c2b, pt_flat, rel_bias, pool_k, pool_v, pool_ki, Bs=Bs, Ts=Ts, n_pages=n_pages)
        xs = _out_proj(xs, ya.reshape(Rs, d_a), yb.reshape(Rs, d_b), yc, wo, tm=Rs)
        if last:
            xs, y_sample, *f2 = _ffn_cast(xs, norm_ffn2[l], ffn2_gate, ffn2_up, ffn2_down, l, norm_final, tf=512)
        else:
            xs, *f2 = _ffn_cast(xs, norm_ffn2[l], ffn2_gate, ffn2_up, ffn2_down, l, tf=512)
        valid = lambda a: a.reshape(Bs, tp, -1)[:, :Ts]
        outs_s.append((valid(k_new).reshape(Bs, Ts, N_KV_HEADS, HEAD_DIM_C),
                       valid(v_new).reshape(Bs, Ts, N_KV_HEADS, HEAD_DIM_C),
                       valid(ki_new),
                       h_last.reshape(Bs, d_a), conv_new, s_last))

        xp = _ffn(xp, norm_ffn1[l], *f1, tm=1024, tf=256)
        ua, ub, c1, k_new, v_new, ki_new, tail, c2b = in_proj(xp, tm=256)
        ya, h_last, conv_new = _lru(ua.reshape(1, S, 2 * d_a), zeros_conv, zeros_h, conv_w[l], conv_b[l],
                                    lwa, lru_ba[l], lwx, lru_bx[l], lru_lambda[l], tc=1024, t_valid_last=1024)
        yb, s_last = _retention(ub.reshape(1, S, -1), cos_p, sin_p, zeros_s, cp=128, c_valid=128, mm_dtype=BF16,
                                chunks_per_step=4)
        yc = _dsa_prompt(c1, tail, c2b, rel_bias, tq=256, n_keep=min(TOPK_MAX, S // 4))
        xp = _out_proj(xp, ya.reshape(S, d_a), yb.reshape(S, d_b), yc, wo, tm=512)
        if last:
            xp, y_prompt = _ffn(xp, norm_ffn2[l], *f2, norm_final, tm=512, tf=512)
        else:
            xp = _ffn(xp, norm_ffn2[l], *f2, tm=1024, tf=256)
        outs_p.append((k_new.reshape(1, S, N_KV_HEADS, HEAD_DIM_C),
                       v_new.reshape(1, S, N_KV_HEADS, HEAD_DIM_C),
                       ki_new.reshape(1, S, IDX_DIM),
                       h_last.reshape(1, d_a), conv_new, s_last))

    stack = lambda outs, k: jnp.stack([o[k] for o in outs])
    return ((y_prompt.reshape(1, S, D), y_sample.reshape(Bs, tp, D)[:, :Ts])
            + tuple(stack(outs_p, k) for k in range(6))
            + tuple(stack(outs_s, k) for k in range(6)))
```

```python
import functools
import math

import numpy as np
import jax
import jax.numpy as jnp
from jax import lax
from jax.experimental import pallas as pl
from jax.experimental.pallas import tpu as pltpu

F32 = jnp.float32
BF16 = jnp.bfloat16
I32 = jnp.int32

EPS = 1e-6
N_A_BLOCKS = 4
CONV_WIDTH = 4
LRU_C = 8.0
N_RET_HEADS = 4
ROPE_BASE = 10000.0
HEAD_DIM_C = 128
N_KV_HEADS = 2
N_IDX_HEADS = 16
IDX_DIM = 64
TOPK_MAX = 256
NUM_BUCKETS = 32
MAX_DISTANCE = 128
SAMPLE_PAD_T = 8
PAGES_PER_STEP = 16
NEG_BIG = -1e30
INT_MIN = -2 ** 31
SELECT_BIT_GROUPS = (0, 21, 23, 25, 27, 29, 32)

VMEM_LIMIT = 56 * 1024 * 1024


def _cparams(sem):
    return pltpu.CompilerParams(dimension_semantics=sem, vmem_limit_bytes=VMEM_LIMIT)


def _rms(x, g):
    return x * lax.rsqrt(jnp.mean(x * x, axis=-1, keepdims=True) + EPS) * g


def _dot(a, b):
    return jnp.dot(a, b, preferred_element_type=F32)


def _dot_nt(a, b):
    return lax.dot_general(a, b, (((1,), (1,)), ((), ())), preferred_element_type=F32)


def _dot_tn(a, b):
    return lax.dot_general(a, b, (((0,), (0,)), ((), ())), preferred_element_type=F32)


def _sort_key(score):
    bits = lax.bitcast_convert_type(score, I32)
    return jnp.where(bits < 0, bits ^ jnp.int32(0x7FFFFFFF), bits)


def _ffn_body(x_ref, g_ref, wg_ref, wu_ref, wd_ref, *rest, final_norm):
    if final_norm:
        gf_ref, o_ref, on_ref, h_ref = rest
    else:
        o_ref, h_ref = rest
    j = pl.program_id(1)

    @pl.when(j == 0)
    def _():
        h_ref[...] = _rms(x_ref[...], g_ref[...]).astype(BF16)
        o_ref[...] = jnp.zeros_like(o_ref)

    h = h_ref[...]
    g = _dot(h, wg_ref[...])
    u = _dot(h, wu_ref[...])
    a = (g * jax.nn.sigmoid(g) * u).astype(BF16)
    o_ref[...] += _dot(a, wd_ref[...])

    @pl.when(j == pl.num_programs(1) - 1)
    def _():
        y = x_ref[...] + 0.5 * o_ref[...]
        o_ref[...] = y
        if final_norm:
            on_ref[...] = _rms(y, gf_ref[...])


def _ffn(x, g, wg, wu, wd, gf=None, *, tm, tf):
    R, D = x.shape
    FF = wg.shape[1]
    final_norm = gf is not None
    in_specs = [
        pl.BlockSpec((tm, D), lambda i, j: (i, 0)),
        pl.BlockSpec((1, D), lambda i, j: (0, 0)),
        pl.BlockSpec((D, tf), lambda i, j: (0, j)),
        pl.BlockSpec((D, tf), lambda i, j: (0, j)),
        pl.BlockSpec((tf, D), lambda i, j: (j, 0)),
    ]
    args = [x, g.reshape(1, D), wg, wu, wd]
    out_shape = [jax.ShapeDtypeStruct((R, D), F32)]
    out_specs = [pl.BlockSpec((tm, D), lambda i, j: (i, 0))]
    if final_norm:
        in_specs.append(pl.BlockSpec((1, D), lambda i, j: (0, 0)))
        args.append(gf.reshape(1, D))
        out_shape.append(jax.ShapeDtypeStruct((R, D), F32))
        out_specs.append(pl.BlockSpec((tm, D), lambda i, j: (i, 0)))
    res = pl.pallas_call(
        functools.partial(_ffn_body, final_norm=final_norm),
        grid=(R // tm, FF // tf),
        in_specs=in_specs,
        out_specs=out_specs,
        out_shape=out_shape,
        scratch_shapes=[pltpu.VMEM((tm, D), BF16)],
        compiler_params=_cparams(("parallel", "arbitrary")),
        name="ffn",
    )(*args)
    return res if final_norm else res[0]


def _ffn_cast_body(x_ref, g_ref, wg_ref, wu_ref, wd_ref, *rest, final_norm):
    if final_norm:
        gf_ref, o_ref, on_ref, wgb_ref, wub_ref, wdb_ref, h_ref = rest
    else:
        o_ref, wgb_ref, wub_ref, wdb_ref, h_ref = rest
    j = pl.program_id(0)

    @pl.when(j == 0)
    def _():
        h_ref[...] = _rms(x_ref[...], g_ref[...]).astype(BF16)
        o_ref[...] = jnp.zeros_like(o_ref)

    wg, wu, wd = wg_ref[0].astype(BF16), wu_ref[0].astype(BF16), wd_ref[0].astype(BF16)
    wgb_ref[...] = wg
    wub_ref[...] = wu
    wdb_ref[...] = wd
    h = h_ref[...]
    g = _dot(h, wg)
    a = (g * jax.nn.sigmoid(g) * _dot(h, wu)).astype(BF16)
    o_ref[...] += _dot(a, wd)

    @pl.when(j == pl.num_programs(0) - 1)
    def _():
        y = x_ref[...] + 0.5 * o_ref[...]
        o_ref[...] = y
        if final_norm:
            on_ref[...] = _rms(y, gf_ref[...])


def _ffn_cast(x, g, wg, wu, wd, layer, gf=None, *, tf):
    R, D = x.shape
    FF = wg.shape[2]
    final_norm = gf is not None
    in_specs = [
        pl.BlockSpec((R, D), lambda j: (0, 0)),
        pl.BlockSpec((1, D), lambda j: (0, 0)),
        pl.BlockSpec((1, D, tf), lambda j: (layer, 0, j)),
        pl.BlockSpec((1, D, tf), lambda j: (layer, 0, j)),
        pl.BlockSpec((1, tf, D), lambda j: (layer, j, 0)),
    ]
    args = [x, g.reshape(1, D), wg, wu, wd]
    out_shape = [jax.ShapeDtypeStruct((R, D), F32)]
    out_specs = [pl.BlockSpec((R, D), lambda j: (0, 0))]
    if final_norm:
        in_specs.append(pl.BlockSpec((1, D), lambda j: (0, 0)))
        args.append(gf.reshape(1, D))
        out_shape.append(jax.ShapeDtypeStruct((R, D), F32))
        out_specs.append(pl.BlockSpec((R, D), lambda j: (0, 0)))
    out_shape += [jax.ShapeDtypeStruct((D, FF), BF16), jax.ShapeDtypeStruct((D, FF), BF16),
                  jax.ShapeDtypeStruct((FF, D), BF16)]
    out_specs += [pl.BlockSpec((D, tf), lambda j: (0, j)), pl.BlockSpec((D, tf), lambda j: (0, j)),
                  pl.BlockSpec((tf, D), lambda j: (j, 0))]
    return pl.pallas_call(
        functools.partial(_ffn_cast_body, final_norm=final_norm),
        grid=(FF // tf,),
        in_specs=in_specs,
        out_specs=out_specs,
        out_shape=out_shape,
        scratch_shapes=[pltpu.VMEM((R, D), BF16)],
        compiler_params=_cparams(("arbitrary",)),
        name="ffn_cast",
    )(*args)


def _in_proj_body(x_ref, g_ref, w_ref, wt_ref, ua_ref, ub_ref, c1_ref, k_ref, v_ref, ki_ref, tail_ref, c2b_ref,
                  *, q_scale, cols):
    h = _rms(x_ref[...], g_ref[...]).astype(BF16)
    o_a, o_b, o_qc, o_kv, o_qi, o_end = cols
    kv_w = k_ref.shape[1]
    d_c = o_kv - o_qc
    ua_ref[...] = _dot(h, w_ref[0, :, o_a:o_b])
    ub_ref[...] = _dot(h, w_ref[0, :, o_b:o_qc])
    c1_ref[:, :d_c] = (_dot(h, w_ref[0, :, o_qc:o_kv]) * q_scale).astype(BF16)
    c1_ref[:, d_c:] = _dot(h, w_ref[0, :, o_qi:o_end]).astype(BF16)
    kv = _dot(h, w_ref[0, :, o_kv:o_qi])
    t = _dot(h, wt_ref[0])
    k_ref[...] = kv[:, :kv_w]
    v_ref[...] = kv[:, kv_w:]
    ki_ref[...] = t[:, :IDX_DIM]
    tail_ref[...] = t
    c2b_ref[:, :2 * kv_w] = kv.astype(BF16)
    c2b_ref[:, 2 * kv_w:] = t.astype(BF16)


def _in_proj(x, g, w_all, w_tail, layer, *, tm, cols, q_scale):
    R, D = x.shape
    o_a, o_b, o_qc, o_kv, o_qi, o_end = cols
    kv_w = N_KV_HEADS * HEAD_DIM_C
    tw = w_tail.shape[2]
    outs = [(o_b - o_a, F32), (o_qc - o_b, F32), (o_kv - o_qc + o_end - o_qi, BF16), (kv_w, F32), (kv_w, F32),
            (IDX_DIM, F32), (tw, F32), (2 * kv_w + tw, BF16)]
    return pl.pallas_call(
        functools.partial(_in_proj_body, q_scale=q_scale, cols=cols),
        grid=(R // tm,),
        in_specs=[pl.BlockSpec((tm, D), lambda i: (i, 0)), pl.BlockSpec((1, D), lambda i: (0, 0)),
                  pl.BlockSpec((1,) + w_all.shape[1:], lambda i: (layer, 0, 0), pipeline_mode=pl.Buffered(1)),
                  pl.BlockSpec((1,) + w_tail.shape[1:], lambda i: (layer, 0, 0), pipeline_mode=pl.Buffered(1))],
        out_specs=[pl.BlockSpec((tm, wd), lambda i: (i, 0)) for wd, _ in outs],
        out_shape=[jax.ShapeDtypeStruct((R, wd), dt) for wd, dt in outs],
        compiler_params=_cparams(("parallel",)),
        name="in_proj",
    )(x, g.reshape(1, D), w_all, w_tail)


def _oproj_body(x_ref, ya_ref, yb_ref, yc_ref, w_ref, o_ref):
    da, db = ya_ref.shape[1], yb_ref.shape[1]
    o_ref[...] = (x_ref[...] + _dot(ya_ref[...], w_ref[:da]) + _dot(yb_ref[...], w_ref[da:da + db])
                  + _dot(yc_ref[...], w_ref[da + db:]))


def _out_proj(x, ya, yb, yc, w, *, tm):
    R, D = x.shape
    row = lambda a: pl.BlockSpec((tm, a.shape[1]), lambda i: (i, 0))
    return pl.pallas_call(
        _oproj_body,
        grid=(R // tm,),
        in_specs=[row(x), row(ya), row(yb), row(yc),
                  pl.BlockSpec(w.shape, lambda i: (0, 0), pipeline_mode=pl.Buffered(1))],
        out_specs=row(x),
        out_shape=jax.ShapeDtypeStruct((R, D), F32),
        compiler_params=_cparams(("parallel",)),
        name="out_proj",
    )(x, ya, yb, yc, w)


def _lru_body(xa_ref, ga_ref, conv0_ref, h0_ref, cw_ref, cb_ref, wa_ref, ba_ref, wx_ref, bx_ref, lam_ref,
              ya_ref, hl_ref, cn_ref, xs_ref, a_ref, u_ref, hs_ref, hc_ref, *, tc, t_valid_last):
    t = pl.program_id(1)
    da = xa_ref.shape[-1]
    blk = da // N_A_BLOCKS
    tail = 8

    @pl.when(t == 0)
    def _():
        xs_ref[0:tail, :] = conv0_ref[0]
        hc_ref[0:1, :] = h0_ref[0]

    xs_ref[tail:tail + tc, :] = xa_ref[0]
    xc = cb_ref[...]
    for j in range(CONV_WIDTH):
        off = tail - (CONV_WIDTH - 1) + j
        xc = xc + xs_ref[off:off + tc, :] * cw_ref[j:j + 1, :]
    xcb = xc.astype(BF16)
    pre_r = jnp.concatenate([_dot(xcb[:, n * blk:(n + 1) * blk], wa_ref[n]) for n in range(N_A_BLOCKS)], axis=1)
    pre_i = jnp.concatenate([_dot(xcb[:, n * blk:(n + 1) * blk], wx_ref[n]) for n in range(N_A_BLOCKS)], axis=1)
    r = jax.nn.sigmoid(pre_r + ba_ref[...])
    gi = jax.nn.sigmoid(pre_i + bx_ref[...])
    z = -lam_ref[...]
    softplus = jnp.maximum(z, 0.0) + jnp.log1p(jnp.exp(-jnp.abs(z)))
    log_a = (-LRU_C) * r * softplus
    a = jnp.exp(log_a)
    u = jnp.sqrt(-jnp.tanh(log_a) * (a * a + 1.0)) * gi * xc

    row_in_group = lax.broadcasted_iota(I32, (tc, da), 0) % 8
    for s in (1, 2, 4):
        first = row_in_group < s
        a_lo = jnp.where(first, 1.0, pltpu.roll(a, s, 0))
        u_lo = jnp.where(first, 0.0, pltpu.roll(u, s, 0))
        u = a * u_lo + u
        a = a * a_lo
    a_ref[...] = a
    u_ref[...] = u

    def step(k, h):
        base = pl.multiple_of(k * 8, 8)
        hg = a_ref[pl.ds(base, 8), :] * h + u_ref[pl.ds(base, 8), :]
        hs_ref[pl.ds(base, 8), :] = hg
        return hg[7:8, :]

    h = lax.fori_loop(0, tc // 8, step, hc_ref[0:1, :])
    hc_ref[0:1, :] = h
    ya_ref[0] = (hs_ref[...] * jax.nn.gelu(ga_ref[0])).astype(ya_ref.dtype)

    @pl.when(t == pl.num_programs(1) - 1)
    def _():
        hl_ref[0] = hs_ref[t_valid_last - 1:t_valid_last, :]
        lo = tail - (CONV_WIDTH - 1) + t_valid_last
        cn_ref[0] = xs_ref[lo:lo + CONV_WIDTH - 1, :]

    xs_ref[0:tail, :] = xs_ref[tc:tc + tail, :]


def _lru(ua, conv0, h0, cw, cb, wa, ba, wx, bx, lam, *, tc, t_valid_last):
    B, T, da2 = ua.shape
    da = da2 // 2
    vec = lambda: pl.BlockSpec((1, da), lambda b, t: (0, 0))
    blk = da // N_A_BLOCKS
    return pl.pallas_call(
        functools.partial(_lru_body, tc=tc, t_valid_last=t_valid_last),
        grid=(B, T // tc),
        in_specs=[
            pl.BlockSpec((1, tc, da), lambda b, t: (b, t, 0)),
            pl.BlockSpec((1, tc, da), lambda b, t: (b, t, 1)),
            pl.BlockSpec((1, 8, da), lambda b, t: (b, 0, 0)),
            pl.BlockSpec((1, 1, da), lambda b, t: (b, 0, 0)),
            pl.BlockSpec((CONV_WIDTH, da), lambda b, t: (0, 0)),
            vec(),
            pl.BlockSpec((N_A_BLOCKS, blk, blk), lambda b, t: (0, 0, 0)),
            vec(),
            pl.BlockSpec((N_A_BLOCKS, blk, blk), lambda b, t: (0, 0, 0)),
            vec(),
            vec(),
        ],
        out_specs=[
            pl.BlockSpec((1, tc, da), lambda b, t: (b, t, 0)),
            pl.BlockSpec((1, 1, da), lambda b, t: (b, 0, 0)),
            pl.BlockSpec((1, CONV_WIDTH - 1, da), lambda b, t: (b, 0, 0)),
        ],
        out_shape=[
            jax.ShapeDtypeStruct((B, T, da), BF16),
            jax.ShapeDtypeStruct((B, 1, da), F32),
            jax.ShapeDtypeStruct((B, CONV_WIDTH - 1, da), F32),
        ],
        scratch_shapes=[
            pltpu.VMEM((tc + 8, da), F32),
            pltpu.VMEM((tc, da), F32),
            pltpu.VMEM((tc, da), F32),
            pltpu.VMEM((tc, da), F32),
            pltpu.VMEM((8, da), F32),
        ],
        compiler_params=_cparams(("arbitrary", "arbitrary")),
        name="rglru",
    )(ua, ua, conv0, h0, cw, cb.reshape(1, da), wa, ba.reshape(1, da), wx, bx.reshape(1, da), lam.reshape(1, da))


def _ret_body(q_ref, k_ref, v_ref, gb_ref, cos_ref, sin_ref, s0_ref, yb_ref, sl_ref, s_ref, *, cp, c_valid, mm_dtype):
    t = pl.program_id(1)
    dk = q_ref.shape[-1] // N_RET_HEADS
    dv = v_ref.shape[-1] // N_RET_HEADS

    @pl.when(t == 0)
    def _():
        s_ref[...] = s0_ref[0]

    cos = cos_ref[...]
    sin = sin_ref[...]
    first_half = (lax.broadcasted_iota(I32, cos.shape, 1) % dk) < (dk // 2)
    width = cos.shape[1]

    def rot(x):
        partner = jnp.where(first_half, pltpu.roll(x, width - dk // 2, 1), pltpu.roll(x, dk // 2, 1))
        return x * cos + partner * sin

    row = lax.broadcasted_iota(I32, (cp, 1), 0)
    q_all = rot(q_ref[0])
    k_all = rot(k_ref[0]) * (dk ** -0.5)
    ri = lax.broadcasted_iota(I32, (cp, cp), 0)
    ci = lax.broadcasted_iota(I32, (cp, cp), 1)
    diff = (ri - ci).astype(F32)
    rowf = row.astype(F32)
    for h in range(N_RET_HEADS):
        lg = math.log1p(-(2.0 ** (-5.0 - h)))
        dmask = jnp.where(diff >= 0, jnp.exp(jnp.maximum(diff, 0.0) * lg), 0.0)
        cross_dec = jnp.exp((rowf + 1.0) * lg)
        state_dec = jnp.exp((c_valid - 1.0 - rowf) * lg)
        chunk_dec = math.exp(c_valid * lg)
        state = s_ref[h]
        for n in range(q_all.shape[0] // cp):
            rows = slice(n * cp, (n + 1) * cp)
            qh = q_all[rows, h * dk:(h + 1) * dk].astype(mm_dtype)
            kh = k_all[rows, h * dk:(h + 1) * dk]
            if c_valid < cp:
                kh = jnp.where(row < c_valid, kh, 0.0)
            vh = v_ref[0, rows, h * dv:(h + 1) * dv].astype(mm_dtype)
            att = _dot_nt(qh, kh.astype(mm_dtype)) * dmask
            inner = _dot(att.astype(mm_dtype), vh)
            cross = _dot(qh, state.astype(mm_dtype)) * cross_dec
            state = state * chunk_dec + _dot_tn((kh * state_dec).astype(mm_dtype), vh)
            o = inner + cross
            o = o * lax.rsqrt(jnp.mean(o * o, axis=-1, keepdims=True) + EPS)
            gh = gb_ref[0, rows, h * dv:(h + 1) * dv]
            yb_ref[0, rows, h * dv:(h + 1) * dv] = (o * (gh * jax.nn.sigmoid(gh))).astype(yb_ref.dtype)
        s_ref[h] = state

    @pl.when(t == pl.num_programs(1) - 1)
    def _():
        sl_ref[0] = s_ref[...]


def _retention(ub, cos, sin, s0, *, cp, c_valid, mm_dtype, chunks_per_step=1):
    B, T, wtot = ub.shape
    w = wtot // 6
    H, dk, dv = s0.shape[1:]
    blk = cp * chunks_per_step
    return pl.pallas_call(
        functools.partial(_ret_body, cp=cp, c_valid=c_valid, mm_dtype=mm_dtype),
        grid=(B, T // blk),
        in_specs=[
            pl.BlockSpec((1, blk, w), lambda b, t: (b, t, 0)),
            pl.BlockSpec((1, blk, w), lambda b, t: (b, t, 1)),
            pl.BlockSpec((1, blk, 2 * w), lambda b, t: (b, t, 1)),
            pl.BlockSpec((1, blk, 2 * w), lambda b, t: (b, t, 2)),
            pl.BlockSpec((blk, w), lambda b, t: (t, 0)),
            pl.BlockSpec((blk, w), lambda b, t: (t, 0)),
            pl.BlockSpec((1, H, dk, dv), lambda b, t: (b, 0, 0, 0)),
        ],
        out_specs=[
            pl.BlockSpec((1, blk, 2 * w), lambda b, t: (b, t, 0)),
            pl.BlockSpec((1, H, dk, dv), lambda b, t: (b, 0, 0, 0)),
        ],
        out_shape=[
            jax.ShapeDtypeStruct((B, T, 2 * w), BF16),
            jax.ShapeDtypeStruct((B, H, dk, dv), F32),
        ],
        scratch_shapes=[pltpu.VMEM((H, dk, dv), F32)],
        compiler_params=_cparams(("arbitrary", "arbitrary")),
        name="retention",
    )(ub, ub, ub, ub, cos, sin, s0)


def _rope_tables(pos, dk, heads):
    half = dk // 2
    freqs = ROPE_BASE ** (-jnp.arange(half, dtype=F32) / half)
    ang = pos.astype(F32)[:, None] * freqs[None, :]
    cos, sin = jnp.cos(ang), jnp.sin(ang)
    cos_t = jnp.tile(jnp.concatenate([cos, cos], axis=1), (1, heads))
    sin_t = jnp.tile(jnp.concatenate([-sin, sin], axis=1), (1, heads))
    return cos_t, sin_t


def _t5_bucket_np(dist):
    n = np.maximum(dist, 0)
    max_exact = NUM_BUCKETS // 2
    ratio = np.log(np.maximum(n, 1).astype(np.float32) / np.float32(max_exact)) / np.float32(math.log(MAX_DISTANCE / max_exact))
    large = max_exact + (ratio * np.float32(NUM_BUCKETS - max_exact)).astype(np.int32)
    large = np.minimum(large, NUM_BUCKETS - 1)
    return np.where(n < max_exact, n, large).astype(np.int32)


def _bias_from_buckets(bucket, rb_ref, head):
    def step(b, out):
        return jnp.where(bucket == b, rb_ref[b, head], out)

    return lax.fori_loop(0, NUM_BUCKETS, step, jnp.zeros(bucket.shape, F32))


def _dsa_prompt_body(rb_ref, qc_ref, qi_ref, wi_ref, kvb_ref, btab_ref, o_ref,
                     keys_ref, bt_ref, wf_ref, qis_ref, qs_ref, mx_ref, mrep_ref, acc_ref, cut_ref, kmax_ref, sel_ref,
                     *, tq, n_keep, idx_bits):
    i = pl.program_id(0)
    G = qc_ref.shape[1] // (N_KV_HEADS * HEAD_DIM_C)
    n_heads = N_KV_HEADS * G
    kcol, vcol, icol = 0, N_KV_HEADS * HEAD_DIM_C, 2 * N_KV_HEADS * HEAD_DIM_C
    lanes = HEAD_DIM_C
    halves = tq // lanes
    n_tiles = i + 1

    @pl.when(i == 0)
    def _():
        for h in range(n_heads):
            far = rb_ref[NUM_BUCKETS - 1, h]
            for r in range(2):
                bt_ref[h // G, r, h % G] = _bias_from_buckets(btab_ref[r], rb_ref, h) - far

    for h in range(N_IDX_HEADS):
        qis_ref[h] = qi_ref[:, h * IDX_DIM:(h + 1) * IDX_DIM]
        wf_ref[h] = jnp.broadcast_to(wi_ref[:, IDX_DIM + h:IDX_DIM + h + 1], (tq, lanes))
    for c in range(N_KV_HEADS):
        for g in range(G):
            h = c * G + g
            qs_ref[c, g * tq:(g + 1) * tq, :] = qc_ref[:, h * HEAD_DIM_C:(h + 1) * HEAD_DIM_C]

    rowi = lax.broadcasted_iota(I32, (tq, tq), 0)
    coli = lax.broadcasted_iota(I32, (tq, tq), 1)

    def key_index(j):
        return j * tq + coli

    def visible(j):
        return key_index(j) <= (i * tq + rowi)

    def score_tile(j, carry):
        ki = kvb_ref[pl.ds(pl.multiple_of(j * tq, tq), tq), icol:icol + IDX_DIM]
        for rh in range(halves):
            rows = slice(rh * lanes, (rh + 1) * lanes)
            acc = jnp.zeros((lanes, tq), F32)
            for h in range(N_IDX_HEADS):
                w = wf_ref[h, rows, :]
                acc = acc + jnp.maximum(_dot_nt(qis_ref[h, rows, :], ki), 0.0) * jnp.concatenate([w] * halves, axis=1)
            score = acc * (N_IDX_HEADS ** -0.5 * IDX_DIM ** -0.5)
            score = jnp.where(visible(j)[rows], score, -jnp.inf)
            keys_ref[j, :, rows] = _sort_key(score).T
        return carry

    lax.fori_loop(0, n_tiles // 2, lambda p, carry: score_tile(2 * p + 1, score_tile(2 * p, carry)), 0)
    lax.fori_loop(2 * (n_tiles // 2), n_tiles, score_tile, 0)

    sub = 8
    grp = lax.broadcasted_iota(I32, (tq // sub, sub, tq), 0)
    srow = lax.broadcasted_iota(I32, (tq // sub, sub, tq), 1)

    def count(pred, *row_args):
        args = [a[None] for a in row_args]

        def one(j, cnt):
            k = keys_ref[j].reshape(tq // sub, sub, tq)
            idx = j * tq + grp * sub + srow
            return cnt + jnp.sum(jnp.where(pred(k, idx, *args), 1, 0), axis=0)

        n_pairs = n_tiles // 2
        cnt = lax.fori_loop(0, n_pairs, lambda p, c: one(2 * p + 1, one(2 * p, c)), jnp.zeros((sub, tq), I32))
        cnt = lax.fori_loop(2 * n_pairs, n_tiles, one, cnt)
        return jnp.broadcast_to(jnp.sum(cnt, axis=0, keepdims=True), (sub, tq))

    def bit_step(b, st):
        tau, n_ge = st
        cand = tau + lax.shift_left(jnp.int32(1), 31 - b)
        c = count(lambda k, idx, cnd: k >= cnd, cand)
        ge = c >= n_keep
        return jnp.where(ge, cand, tau), jnp.where(ge, c, n_ge)

    sel_ref[0] = jnp.full((sub, tq), INT_MIN, I32)
    sel_ref[1] = jnp.broadcast_to(n_tiles * tq, (sub, tq)).astype(I32)
    for b0, b1 in zip(SELECT_BIT_GROUPS[:-1], SELECT_BIT_GROUPS[1:]):
        @pl.when(jnp.max(sel_ref[1]) > n_keep)
        def _():
            tau_g, n_ge_g = lax.fori_loop(b0, b1, bit_step, (sel_ref[0], sel_ref[1]))
            sel_ref[0] = tau_g
            sel_ref[1] = n_ge_g

    tau, n_ge = sel_ref[0], sel_ref[1]

    cut_ref[...] = jnp.full((sub, tq), 2 ** 31 - 1, I32)

    @pl.when(jnp.max(n_ge) > n_keep)
    def _():
        need = n_keep - count(lambda k, idx, t: k > t, tau)

        def idx_step(b, cut):
            cand = cut + lax.shift_left(jnp.int32(1), idx_bits - 1 - b)
            below = count(lambda k, idx, t, c: (k == t) & (idx < c), tau, cand)
            return jnp.where(below < need, cand, cut)

        cut_ref[...] = lax.fori_loop(0, idx_bits, idx_step, jnp.zeros((sub, tq), I32))

    tau_t = jnp.broadcast_to(tau[0:1], (tq, tq)).T
    cut_t = jnp.broadcast_to(cut_ref[0:1, :], (tq, tq)).T

    def mask_bias(j, r):
        k = keys_ref[j].T
        sel = (k > tau_t) | ((k == tau_t) & (key_index(j) <= cut_t))
        if r is not None:
            sel = sel & visible(j)
        return jnp.where(sel, 0.0, NEG_BIG)

    def logits(j, c, r, mb):
        base = pl.multiple_of(j * tq, tq)
        kc = kvb_ref[pl.ds(base, tq), kcol + c * HEAD_DIM_C:kcol + (c + 1) * HEAD_DIM_C]
        s = _dot_nt(qs_ref[c], kc).reshape(G, tq, tq) + mb[None]
        if r is not None:
            s = s + bt_ref[c, r]
        return s

    def max_tile(j, r):
        mb = mask_bias(j, r)
        for c in range(N_KV_HEADS):
            s = logits(j, c, r, mb)
            m = s[..., 0:lanes]
            for a in range(1, halves):
                m = jnp.maximum(m, s[..., a * lanes:(a + 1) * lanes])
            mx_ref[c] = jnp.maximum(mx_ref[c], m)

    def acc_tile(j, r):
        base = pl.multiple_of(j * tq, tq)
        mb = mask_bias(j, r)
        for c in range(N_KV_HEADS):
            vc = kvb_ref[pl.ds(base, tq), vcol + c * HEAD_DIM_C:vcol + (c + 1) * HEAD_DIM_C]
            vext = jnp.concatenate([vc, jnp.ones((tq, lanes), BF16)], axis=1)
            m = mrep_ref[c]
            p = jnp.exp(logits(j, c, r, mb) - jnp.concatenate([m] * halves, axis=-1))
            acc_ref[c] += _dot(p.reshape(G * tq, tq).astype(BF16), vext)

    def sweep(tile_fn):
        def far(j, carry):
            tile_fn(j, None)
            return carry

        n_far = jnp.maximum(i - 1, 0)
        lax.fori_loop(0, n_far // 2, lambda p, carry: far(2 * p + 1, far(2 * p, carry)), 0)
        lax.fori_loop(2 * (n_far // 2), n_far, far, 0)

        @pl.when(i >= 1)
        def _():
            tile_fn(i - 1, 1)

        tile_fn(i, 0)

    @pl.when(i == 0)
    def _():
        kmax_ref[...] = jnp.zeros(kmax_ref.shape, F32)

    for c in range(N_KV_HEADS):
        kt = kvb_ref[pl.ds(pl.multiple_of(i * tq, tq), tq), kcol + c * HEAD_DIM_C:kcol + (c + 1) * HEAD_DIM_C].astype(F32)
        knorm = jnp.sqrt(jnp.max(jnp.sum(kt * kt, axis=1, keepdims=True)))
        kmax_ref[c] = jnp.maximum(kmax_ref[c], knorm)
        qf = qs_ref[c].astype(F32)
        qnorm = jnp.sqrt(jnp.sum(qf * qf, axis=1, keepdims=True)).reshape(G, tq, 1)
        for g in range(G):
            h = c * G + g
            far = rb_ref[NUM_BUCKETS - 1, h]
            bias_max = lax.fori_loop(0, NUM_BUCKETS, lambda b, m: jnp.maximum(m, rb_ref[b, h] - far), jnp.float32(0.0))
            mrep_ref[c, g] = qnorm[g] * kmax_ref[c, 0:1, :] + bias_max

    acc_ref[...] = jnp.zeros(acc_ref.shape, F32)
    sweep(acc_tile)

    @pl.when(jnp.logical_not(jnp.min(acc_ref[:, :, lanes:]) > 1e-30))
    def _():
        mx_ref[...] = jnp.full(mx_ref.shape, NEG_BIG, F32)
        sweep(max_tile)
        mrep_ref[...] = jnp.broadcast_to(jnp.max(mx_ref[...], axis=-1, keepdims=True), mrep_ref.shape)
        acc_ref[...] = jnp.zeros(acc_ref.shape, F32)
        sweep(acc_tile)

    for c in range(N_KV_HEADS):
        for g in range(G):
            h = c * G + g
            a = acc_ref[c, g * tq:(g + 1) * tq, :]
            o_ref[:, h * HEAD_DIM_C:(h + 1) * HEAD_DIM_C] = (a[:, :lanes] / a[:, lanes:]).astype(o_ref.dtype)


def _dsa_prompt(qcqi, tail, kvi_bf16, rel_bias, *, tq, n_keep):
    S = qcqi.shape[0]
    dc = qcqi.shape[1] - N_IDX_HEADS * IDX_DIM
    n_heads = dc // HEAD_DIM_C
    G = n_heads // N_KV_HEADS
    wkv = kvi_bf16.shape[1]
    d = np.arange(tq)[:, None] - np.arange(tq)[None, :]
    btab = jnp.asarray(np.stack([_t5_bucket_np(d + r * tq) for r in range(2)]))
    assert _t5_bucket_np(np.array([tq + 1]))[0] == NUM_BUCKETS - 1
    lanes = HEAD_DIM_C
    grid_spec = pltpu.PrefetchScalarGridSpec(
        num_scalar_prefetch=0,
        grid=(S // tq,),
        in_specs=[
            pl.BlockSpec(memory_space=pltpu.SMEM),
            pl.BlockSpec((tq, dc), lambda i: (i, 0)),
            pl.BlockSpec((tq, N_IDX_HEADS * IDX_DIM), lambda i: (i, dc // (N_IDX_HEADS * IDX_DIM))),
            pl.BlockSpec((tq, tail.shape[1]), lambda i: (i, 0)),
            pl.BlockSpec((S, wkv), lambda i: (0, 0), pipeline_mode=pl.Buffered(1)),
            pl.BlockSpec((2, tq, tq), lambda i: (0, 0, 0)),
        ],
        out_specs=pl.BlockSpec((tq, dc), lambda i: (i, 0)),
        scratch_shapes=[
            pltpu.VMEM((S // tq, tq, tq), I32),
            pltpu.VMEM((N_KV_HEADS, 2, G, tq, tq), F32),
            pltpu.VMEM((N_IDX_HEADS, tq, lanes), F32),
            pltpu.VMEM((N_IDX_HEADS, tq, IDX_DIM), BF16),
            pltpu.VMEM((N_KV_HEADS, G * tq, HEAD_DIM_C), BF16),
            pltpu.VMEM((N_KV_HEADS, G, tq, lanes), F32),
            pltpu.VMEM((N_KV_HEADS, G, tq, lanes), F32),
            pltpu.VMEM((N_KV_HEADS, G * tq, 2 * lanes), F32),
            pltpu.VMEM((8, tq), I32),
            pltpu.VMEM((N_KV_HEADS, 8, lanes), F32),
            pltpu.VMEM((2, 8, tq), I32),
        ],
    )
    return pl.pallas_call(
        functools.partial(_dsa_prompt_body, tq=tq, n_keep=n_keep, idx_bits=(S - 1).bit_length()),
        grid_spec=grid_spec,
        out_shape=jax.ShapeDtypeStruct((S, dc), BF16),
        compiler_params=_cparams(("arbitrary",)),
        name="dsa_prompt",
    )(rel_bias, qcqi, qcqi, tail, kvi_bf16, btab)


def _dsa_sel_body(pt_ref, qi_ref, w_ref, kin_ref, *rest, t_new, n_keep):
    pages = rest[:PAGES_PER_STEP]
    mb_ref, keys_ref, arg_ref, cnt_ref = rest[PAGES_PER_STEP:]
    b = pl.program_id(0)
    p = pl.program_id(1)
    B, n_chunks, tp, P = keys_ref.shape
    last = p == pl.num_programs(1) - 1
    q = qi_ref[0]
    w = w_ref[0]
    rowi = lax.broadcasted_iota(I32, (tp, P), 0)
    coli = lax.broadcasted_iota(I32, (tp, P), 1)
    vis_new = (coli <= rowi) & (coli < t_new)

    def chunk_scores(s):
        s = jnp.maximum(s, 0.0) * w
        s = s.reshape(tp, N_IDX_HEADS, s.shape[-1]).sum(axis=1)
        return s * (N_IDX_HEADS ** -0.5 * IDX_DIM ** -0.5)

    for r in range(PAGES_PER_STEP):
        keys_ref[b, p * PAGES_PER_STEP + r] = _sort_key(chunk_scores(_dot(q, pages[r][0].astype(BF16))))

    @pl.when(last)
    def _():
        s_new = jnp.where(vis_new, chunk_scores(_dot_nt(q, kin_ref[0])), -jnp.inf)
        keys_ref[b, n_chunks - 1] = _sort_key(s_new)

    @pl.when(last & (b == B - 1))
    def _():
        key_index = (lax.broadcasted_iota(I32, (n_chunks, tp, P), 0) * P
                     + lax.broadcasted_iota(I32, (n_chunks, tp, P), 2))

        def count(pred, *row_args):
            for n, a in enumerate(row_args):
                arg_ref[n] = a

            def per_seq(s, carry):
                args = [arg_ref[n, s][None] for n in range(len(row_args))]
                cnt_ref[s] = jnp.sum(jnp.where(pred(keys_ref[s], key_index, *args), 1, 0), axis=0)
                return carry

            lax.fori_loop(0, B, per_seq, 0)
            return jnp.broadcast_to(jnp.sum(cnt_ref[...], axis=-1, keepdims=True), (B, tp, P))

        def bit_step(bit, tau):
            cand = tau + lax.shift_left(jnp.int32(1), 31 - bit)
            return jnp.where(count(lambda k, idx, c: k >= c, cand) >= n_keep, cand, tau)

        tau = lax.fori_loop(0, 32, bit_step, jnp.full((B, tp, P), INT_MIN, I32))

        arg_ref[2] = jnp.full((B, tp, P), 2 ** 31 - 1, I32)

        @pl.when(jnp.max(count(lambda k, idx, t: k >= t, tau)) > n_keep)
        def _():
            need = n_keep - count(lambda k, idx, t: k > t, tau)
            idx_bits = (n_chunks * P - 1).bit_length()

            def idx_step(bit, cut):
                cand = cut + lax.shift_left(jnp.int32(1), idx_bits - 1 - bit)
                below = count(lambda k, idx, t, c: (k == t) & (idx < c), tau, cand)
                return jnp.where(below < need, cand, cut)

            arg_ref[2] = lax.fori_loop(0, idx_bits, idx_step, jnp.zeros((B, tp, P), I32))

        arg_ref[0] = tau

        def write_mask(s, carry):
            k = keys_ref[s]
            t = arg_ref[0, s][None]
            sel = (k > t) | ((k == t) & (key_index <= arg_ref[2, s][None]))
            mb_ref[s] = jnp.where(sel, 0.0, NEG_BIG)
            mb_ref[s, n_chunks - 1] = jnp.where(sel[n_chunks - 1] & vis_new, 0.0, NEG_BIG)
            return carry

        lax.fori_loop(0, B, write_mask, 0)


def _dsa_sample_select(pt_flat, qi, w, ki_new, pool_ki, *, n_pages, t_new, n_keep):
    B = qi.shape[0]
    P = pool_ki.shape[2]
    steps = n_pages // PAGES_PER_STEP

    def page_spec(r):
        return pl.BlockSpec((1, IDX_DIM, P), lambda b, p, pt: (pt[b * n_pages + p * PAGES_PER_STEP + r], 0, 0))

    grid_spec = pltpu.PrefetchScalarGridSpec(
        num_scalar_prefetch=1,
        grid=(B, steps),
        in_specs=[
            pl.BlockSpec((1,) + qi.shape[1:], lambda b, p, pt: (b, 0, 0)),
            pl.BlockSpec((1,) + w.shape[1:], lambda b, p, pt: (b, 0, 0)),
            pl.BlockSpec((1,) + ki_new.shape[1:], lambda b, p, pt: (b, 0, 0)),
        ] + [page_spec(r) for r in range(PAGES_PER_STEP)],
        out_specs=pl.BlockSpec((B, n_pages + 1, SAMPLE_PAD_T, P), lambda b, p, pt: (0, 0, 0, 0)),
        scratch_shapes=[
            pltpu.VMEM((B, n_pages + 1, SAMPLE_PAD_T, P), I32),
            pltpu.VMEM((3, B, SAMPLE_PAD_T, P), I32),
            pltpu.VMEM((B, SAMPLE_PAD_T, P), I32),
        ],
    )
    return pl.pallas_call(
        functools.partial(_dsa_sel_body, t_new=t_new, n_keep=n_keep),
        grid_spec=grid_spec,
        out_shape=jax.ShapeDtypeStruct((B, n_pages + 1, SAMPLE_PAD_T, P), F32),
        compiler_params=_cparams(("arbitrary", "arbitrary")),
        name="dsa_sample_select",
    )(pt_flat, qi, w, ki_new, *([pool_ki] * PAGES_PER_STEP))


def _dsa_att_body(pt_ref, rb_ref, q_ref, mb_ref, kn_ref, vn_ref, btab_ref, *rest):
    kp = rest[:PAGES_PER_STEP]
    vp = rest[PAGES_PER_STEP:2 * PAGES_PER_STEP]
    o_ref, bt_ref, m_ref, acc_ref = rest[2 * PAGES_PER_STEP:]
    b = pl.program_id(0)
    p = pl.program_id(1)
    n_chunks = mb_ref.shape[1]
    tp = SAMPLE_PAD_T
    G = q_ref.shape[2] // tp
    rows = G * tp
    P = kn_ref.shape[1]
    lanes = HEAD_DIM_C
    last = p == pl.num_programs(1) - 1

    @pl.when((b == 0) & (p == 0))
    def _():
        for c in range(N_KV_HEADS):
            for kind in range(2):
                for g in range(G):
                    h = c * G + g
                    bt_ref[c, kind, g * tp:(g + 1) * tp, :] = (_bias_from_buckets(btab_ref[kind], rb_ref, h)
                                                              - rb_ref[NUM_BUCKETS - 1, h])

    @pl.when(p == 0)
    def _():
        m_ref[...] = jnp.full(m_ref.shape, NEG_BIG, F32)
        acc_ref[...] = jnp.zeros(acc_ref.shape, F32)

    def attend(c, s, vext):
        m_old = m_ref[c]
        m_new = jnp.maximum(m_old, jnp.max(s, axis=1, keepdims=True))
        alpha = jnp.exp(m_old - m_new)
        pr = jnp.exp(s - jnp.concatenate([m_new] * (s.shape[1] // lanes), axis=1))
        acc_ref[c] = jnp.concatenate([alpha, alpha], axis=1) * acc_ref[c] + _dot(pr.astype(BF16), vext)
        m_ref[c] = m_new

    ones = jnp.ones((P, lanes), BF16)
    mb_step = jnp.concatenate([mb_ref[0, p * PAGES_PER_STEP + r] for r in range(PAGES_PER_STEP)], axis=1)
    mb_step = jnp.concatenate([mb_step] * G, axis=0)
    for c in range(N_KV_HEADS):
        head_rows = pl.ds(c, P, stride=N_KV_HEADS)
        k_all = jnp.concatenate([kp[r][0, head_rows, :].astype(BF16) for r in range(PAGES_PER_STEP)], axis=0)
        v_all = jnp.concatenate(
            [jnp.concatenate([vp[r][0, head_rows, :].astype(BF16), ones], axis=1) for r in range(PAGES_PER_STEP)], axis=0)
        s = _dot_nt(q_ref[0, c], k_all) + mb_step
        s = jnp.concatenate([s[:, :-P], s[:, -P:] + jnp.where(last, bt_ref[c, 0], 0.0)], axis=1)
        attend(c, s, v_all)

    @pl.when(last)
    def _():
        mb_new = jnp.concatenate([mb_ref[0, n_chunks - 1]] * G, axis=0)
        for c in range(N_KV_HEADS):
            cols = slice(c * HEAD_DIM_C, (c + 1) * HEAD_DIM_C)
            s = _dot_nt(q_ref[0, c], kn_ref[0, :, cols]) + mb_new + bt_ref[c, 1]
            attend(c, s, jnp.concatenate([vn_ref[0, :, cols], ones], axis=1))
            a = acc_ref[c]
            o_ref[0, c] = a[:, :lanes] / a[:, lanes:]


def _dsa_sample_attend(pt_flat, rel_bias, q, mb, k_new, v_new, pool_k, pool_v, *, n_pages, past):
    B, kvh, rows, dh = q.shape
    P = pool_k.shape[1] // kvh
    steps = n_pages // PAGES_PER_STEP
    tp = SAMPLE_PAD_T
    t = np.arange(tp)[:, None]
    col = np.arange(P)[None, :]
    assert past - (n_pages - 1) * P >= MAX_DISTANCE
    last_page = _t5_bucket_np(past + t - ((n_pages - 1) * P + col))
    new = _t5_bucket_np(t - col)
    btab = jnp.asarray(np.stack([last_page, new]))

    def page_spec(r):
        return pl.BlockSpec((1, P * kvh, dh), lambda b, p, pt: (pt[b * n_pages + p * PAGES_PER_STEP + r], 0, 0))

    grid_spec = pltpu.PrefetchScalarGridSpec(
        num_scalar_prefetch=1,
        grid=(B, steps),
        in_specs=[
            pl.BlockSpec(memory_space=pltpu.SMEM),
            pl.BlockSpec((1, kvh, rows, dh), lambda b, p, pt: (b, 0, 0, 0)),
            pl.BlockSpec((1,) + mb.shape[1:], lambda b, p, pt: (b, 0, 0, 0)),
            pl.BlockSpec((1,) + k_new.shape[1:], lambda b, p, pt: (b, 0, 0)),
            pl.BlockSpec((1,) + v_new.shape[1:], lambda b, p, pt: (b, 0, 0)),
            pl.BlockSpec((2, tp, P), lambda b, p, pt: (0, 0, 0)),
        ] + [page_spec(r) for r in range(PAGES_PER_STEP)] * 2,
        out_specs=pl.BlockSpec((1, kvh, rows, dh), lambda b, p, pt: (b, 0, 0, 0)),
        scratch_shapes=[
            pltpu.VMEM((kvh, 2, rows, P), F32),
            pltpu.VMEM((kvh, rows, dh), F32),
            pltpu.VMEM((kvh, rows, 2 * dh), F32),
        ],
    )
    return pl.pallas_call(
        _dsa_att_body,
        grid_spec=grid_spec,
        out_shape=jax.ShapeDtypeStruct((B, kvh, rows, dh), F32),
        compiler_params=_cparams(("arbitrary", "arbitrary")),
        name="dsa_sample_attend",
    )(pt_flat, rel_bias, q, mb, k_new, v_new, btab, *([pool_k] * PAGES_PER_STEP), *([pool_v] * PAGES_PER_STEP))


def _dsa_sample(c1, tail, c2b, pt_flat, rel_bias, pool_k, pool_v, pool_ki, *, Bs, Ts, n_pages):
    tp = SAMPLE_PAD_T
    kv_w = N_KV_HEADS * HEAD_DIM_C
    d_c = c1.shape[1] - N_IDX_HEADS * IDX_DIM
    G = d_c // kv_w
    P = pool_ki.shape[2]
    past = n_pages * P
    qi_s = c1[:, d_c:].reshape(Bs, tp * N_IDX_HEADS, IDX_DIM)
    w_s = tail[:, IDX_DIM:IDX_DIM + N_IDX_HEADS].reshape(Bs, tp * N_IDX_HEADS, 1)
    new_rows = jnp.pad(c2b.reshape(Bs, tp, -1), ((0, 0), (0, P - tp), (0, 0)))
    mb = _dsa_sample_select(pt_flat, qi_s, w_s, new_rows[:, :, 2 * kv_w:2 * kv_w + IDX_DIM], pool_ki,
                            n_pages=n_pages, t_new=Ts, n_keep=min(TOPK_MAX, (past + Ts) // 4))
    q_s = c1[:, :d_c].reshape(Bs, tp, N_KV_HEADS, G, HEAD_DIM_C).transpose(0, 2, 3, 1, 4)
    q_s = q_s.reshape(Bs, N_KV_HEADS, G * tp, HEAD_DIM_C)
    o_s = _dsa_sample_attend(pt_flat, rel_bias, q_s, mb, new_rows[:, :, :kv_w], new_rows[:, :, kv_w:2 * kv_w],
                             pool_k, pool_v, n_pages=n_pages, past=past)
    return o_s.reshape(Bs, N_KV_HEADS, G, tp, HEAD_DIM_C).transpose(0, 3, 1, 2, 4).reshape(Bs * tp, d_c).astype(BF16)


def kernel(x_prompt, x_sample, cache_k, cache_v, cache_kidx, page_table, state_lru_h, state_conv, state_ret,
           norm_ffn1, ffn1_gate, ffn1_up, ffn1_down, norm_mix, w_in, w_out, conv_w, conv_b,
           lru_wa, lru_ba, lru_wx, lru_bx, lru_lambda, rel_bias,
           norm_ffn2, ffn2_gate, ffn2_up, ffn2_down, norm_final):
    depth = norm_ffn1.shape[0]
    _, S, D = x_prompt.shape
    Bs, Ts, _ = x_sample.shape
    n_pool, P = cache_k.shape[1], cache_k.shape[2]
    n_pages = page_table.shape[1]
    past = n_pages * P
    d_a = state_lru_h.shape[-1]
    H_r, rdk, rdv = state_ret.shape[2:]
    d_b = H_r * rdv
    d_c = D - d_a - d_b
    kv_w = N_KV_HEADS * HEAD_DIM_C
    qi_w = N_IDX_HEADS * IDX_DIM
    G = d_c // HEAD_DIM_C // N_KV_HEADS
    tp = SAMPLE_PAD_T
    Rs = Bs * tp

    xp = x_prompt.reshape(S, D)
    xs = jnp.pad(x_sample, ((0, 0), (0, tp - Ts), (0, 0))).reshape(Rs, D)

    cos_p, sin_p = _rope_tables(jnp.arange(S, dtype=I32), rdk, H_r)
    cos_s, sin_s = _rope_tables(past + jnp.arange(tp, dtype=I32), rdk, H_r)

    pool_k = cache_k.reshape(depth * n_pool, P * N_KV_HEADS, HEAD_DIM_C)
    pool_v = cache_v.reshape(depth * n_pool, P * N_KV_HEADS, HEAD_DIM_C)
    pool_ki = jnp.swapaxes(cache_kidx, 2, 3).reshape(depth * n_pool, IDX_DIM, P)

    o = np.cumsum([0, 2 * d_a, 2 * H_r * rdk + 2 * d_b, d_c, 2 * kv_w, qi_w, IDX_DIM + N_IDX_HEADS])
    proj_cols = tuple(int(c) for c in o[:6])
    w_in_b = w_in.astype(BF16)
    w_in_tail = jnp.pad(w_in_b[:, :, o[5]:o[6]], ((0, 0), (0, 0), (0, 128 - int(o[6] - o[5]))))

    zeros_conv = jnp.zeros((1, 8, d_a), F32)
    zeros_h = jnp.zeros((1, 1, d_a), F32)
    zeros_s = jnp.zeros((1, H_r, rdk, rdv), F32)

    outs_p, outs_s = [], []
    y_prompt = y_sample = None
    for l in range(depth):
        in_proj = functools.partial(_in_proj, g=norm_mix[l], w_all=w_in_b, w_tail=w_in_tail, layer=l,
                                    cols=proj_cols, q_scale=HEAD_DIM_C ** -0.5)
        wo = w_out[l].astype(BF16)
        lwa, lwx = lru_wa[l].astype(BF16), lru_wx[l].astype(BF16)
        last = l == depth - 1
        pt_flat = (page_table + l * n_pool).reshape(-1).astype(I32)

        xs, *f1 = _ffn_cast(xs, norm_ffn1[l], ffn1_gate, ffn1_up, ffn1_down, l, tf=512)
        ua, ub, c1, k_new, v_new, ki_new, tail, c2b = in_proj(xs, tm=Rs)
        conv0 = jnp.pad(state_conv[l], ((0, 0), (8 - (CONV_WIDTH - 1), 0), (0, 0)))
        ya, h_last, conv_new = _lru(ua.reshape(Bs, tp, 2 * d_a), conv0, state_lru_h[l].reshape(Bs, 1, d_a),
                                    conv_w[l], conv_b[l], lwa, lru_ba[l], lwx, lru_bx[l], lru_lambda[l],
                                    tc=tp, t_valid_last=Ts)
        yb, s_last = _retention(ub.reshape(Bs, tp, -1), cos_s, sin_s, state_ret[l], cp=tp, c_valid=Ts, mm_dtype=F32)
        yc = _dsa_sample(c1, tail, c2b, pt_flat, rel_bias, pool_k, pool_v, pool_ki, Bs=Bs, Ts=Ts, n_pages=n_pages)
        xs = _out_proj(xs, ya.reshape(Rs, d_a), yb.reshape(Rs, d_b), yc, wo, tm=Rs)
        if last:
            xs, y_sample, *f2 = _ffn_cast(xs, norm_ffn2[l], ffn2_gate, ffn2_up, ffn2_down, l, norm_final, tf=512)
        else:
            xs, *f2 = _ffn_cast(xs, norm_ffn2[l], ffn2_gate, ffn2_up, ffn2_down, l, tf=512)
        valid = lambda a: a.reshape(Bs, tp, -1)[:, :Ts]
        outs_s.append((valid(k_new).reshape(Bs, Ts, N_KV_HEADS, HEAD_DIM_C),
                       valid(v_new).reshape(Bs, Ts, N_KV_HEADS, HEAD_DIM_C),
                       valid(ki_new),
                       h_last.reshape(Bs, d_a), conv_new, s_last))

        xp = _ffn(xp, norm_ffn1[l], *f1, tm=1024, tf=256)
        ua, ub, c1, k_new, v_new, ki_new, tail, c2b = in_proj(xp, tm=256)
        ya, h_last, conv_new = _lru(ua.reshape(1, S, 2 * d_a), zeros_conv, zeros_h, conv_w[l], conv_b[l],
                                    lwa, lru_ba[l], lwx, lru_bx[l], lru_lambda[l], tc=1024, t_valid_last=1024)
        yb, s_last = _retention(ub.reshape(1, S, -1), cos_p, sin_p, zeros_s, cp=128, c_valid=128, mm_dtype=BF16,
                                chunks_per_step=4)
        yc = _dsa_prompt(c1, tail, c2b, rel_bias, tq=256, n_keep=min(TOPK_MAX, S // 4))
        xp = _out_proj(xp, ya.reshape(S, d_a), yb.reshape(S, d_b), yc, wo, tm=512)
        if last:
            xp, y_prompt = _ffn(xp, norm_ffn2[l], *f2, norm_final, tm=512, tf=512)
        else:
            xp = _ffn(xp, norm_ffn2[l], *f2, tm=1024, tf=256)
        outs_p.append((k_new.reshape(1, S, N_KV_HEADS, HEAD_DIM_C),
                       v_new.reshape(1, S, N_KV_HEADS, HEAD_DIM_C),
                       ki_new.reshape(1, S, IDX_DIM),
                       h_last.reshape(1, d_a), conv_new, s_last))

    stack = lambda outs, k: jnp.stack([o[k] for o in outs])
    return ((y_prompt.reshape(1, S, D), y_sample.reshape(Bs, tp, D)[:, :Ts])
            + tuple(stack(outs_p, k) for k in range(6))
            + tuple(stack(outs_s, k) for k in range(6)))
```

```python
import functools
import math

import numpy as np
import jax
import jax.numpy as jnp
from jax import lax
from jax.experimental import pallas as pl
from jax.experimental.pallas import tpu as pltpu

F32 = jnp.float32
BF16 = jnp.bfloat16
I32 = jnp.int32

EPS = 1e-6
N_A_BLOCKS = 4
CONV_WIDTH = 4
LRU_C = 8.0
N_RET_HEADS = 4
ROPE_BASE = 10000.0
HEAD_DIM_C = 128
N_KV_HEADS = 2
N_IDX_HEADS = 16
IDX_DIM = 64
TOPK_MAX = 256
NUM_BUCKETS = 32
MAX_DISTANCE = 128
SAMPLE_PAD_T = 8
PAGES_PER_STEP = 32
NEG_BIG = -1e30
INT_MIN = -2 ** 31
SELECT_BIT_GROUPS = (0, 21, 23, 25, 27, 29, 32)

VMEM_LIMIT = 56 * 1024 * 1024


def _cparams(sem):
    return pltpu.CompilerParams(dimension_semantics=sem, vmem_limit_bytes=VMEM_LIMIT)


def _rms(x, g):
    return x * lax.rsqrt(jnp.mean(x * x, axis=-1, keepdims=True) + EPS) * g


def _dot(a, b):
    return jnp.dot(a, b, preferred_element_type=F32)


def _dot_nt(a, b):
    return lax.dot_general(a, b, (((1,), (1,)), ((), ())), preferred_element_type=F32)


def _dot_tn(a, b):
    return lax.dot_general(a, b, (((0,), (0,)), ((), ())), preferred_element_type=F32)


def _sort_key(score):
    bits = lax.bitcast_convert_type(score, I32)
    return jnp.where(bits < 0, bits ^ jnp.int32(0x7FFFFFFF), bits)


def _ffn_body(x_ref, g_ref, wg_ref, wu_ref, wd_ref, *rest, final_norm):
    if final_norm:
        gf_ref, o_ref, on_ref, h_ref = rest
    else:
        o_ref, h_ref = rest
    j = pl.program_id(1)

    @pl.when(j == 0)
    def _():
        h_ref[...] = _rms(x_ref[...], g_ref[...]).astype(BF16)
        o_ref[...] = jnp.zeros_like(o_ref)

    h = h_ref[...]
    g = _dot(h, wg_ref[...])
    u = _dot(h, wu_ref[...])
    a = (g * jax.nn.sigmoid(g) * u).astype(BF16)
    o_ref[...] += _dot(a, wd_ref[...])

    @pl.when(j == pl.num_programs(1) - 1)
    def _():
        y = x_ref[...] + 0.5 * o_ref[...]
        o_ref[...] = y
        if final_norm:
            on_ref[...] = _rms(y, gf_ref[...])


def _ffn(x, g, wg, wu, wd, gf=None, *, tm, tf):
    R, D = x.shape
    FF = wg.shape[1]
    final_norm = gf is not None
    in_specs = [
        pl.BlockSpec((tm, D), lambda i, j: (i, 0)),
        pl.BlockSpec((1, D), lambda i, j: (0, 0)),
        pl.BlockSpec((D, tf), lambda i, j: (0, j)),
        pl.BlockSpec((D, tf), lambda i, j: (0, j)),
        pl.BlockSpec((tf, D), lambda i, j: (j, 0)),
    ]
    args = [x, g.reshape(1, D), wg, wu, wd]
    out_shape = [jax.ShapeDtypeStruct((R, D), F32)]
    out_specs = [pl.BlockSpec((tm, D), lambda i, j: (i, 0))]
    if final_norm:
        in_specs.append(pl.BlockSpec((1, D), lambda i, j: (0, 0)))
        args.append(gf.reshape(1, D))
        out_shape.append(jax.ShapeDtypeStruct((R, D), F32))
        out_specs.append(pl.BlockSpec((tm, D), lambda i, j: (i, 0)))
    res = pl.pallas_call(
        functools.partial(_ffn_body, final_norm=final_norm),
        grid=(R // tm, FF // tf),
        in_specs=in_specs,
        out_specs=out_specs,
        out_shape=out_shape,
        scratch_shapes=[pltpu.VMEM((tm, D), BF16)],
        compiler_params=_cparams(("parallel", "arbitrary")),
        name="ffn",
    )(*args)
    return res if final_norm else res[0]


def _ffn_cast_body(x_ref, g_ref, wg_ref, wu_ref, wd_ref, *rest, final_norm):
    if final_norm:
        gf_ref, o_ref, on_ref, wgb_ref, wub_ref, wdb_ref, h_ref = rest
    else:
        o_ref, wgb_ref, wub_ref, wdb_ref, h_ref = rest
    j = pl.program_id(0)

    @pl.when(j == 0)
    def _():
        h_ref[...] = _rms(x_ref[...], g_ref[...]).astype(BF16)
        o_ref[...] = jnp.zeros_like(o_ref)

    wg, wu, wd = wg_ref[0].astype(BF16), wu_ref[0].astype(BF16), wd_ref[0].astype(BF16)
    wgb_ref[...] = wg
    wub_ref[...] = wu
    wdb_ref[...] = wd
    h = h_ref[...]
    g = _dot(h, wg)
    a = (g * jax.nn.sigmoid(g) * _dot(h, wu)).astype(BF16)
    o_ref[...] += _dot(a, wd)

    @pl.when(j == pl.num_programs(0) - 1)
    def _():
        y = x_ref[...] + 0.5 * o_ref[...]
        o_ref[...] = y
        if final_norm:
            on_ref[...] = _rms(y, gf_ref[...])


def _ffn_cast(x, g, wg, wu, wd, layer, gf=None, *, tf):
    R, D = x.shape
    FF = wg.shape[2]
    final_norm = gf is not None
    in_specs = [
        pl.BlockSpec((R, D), lambda j: (0, 0)),
        pl.BlockSpec((1, D), lambda j: (0, 0)),
        pl.BlockSpec((1, D, tf), lambda j: (layer, 0, j)),
        pl.BlockSpec((1, D, tf), lambda j: (layer, 0, j)),
        pl.BlockSpec((1, tf, D), lambda j: (layer, j, 0)),
    ]
    args = [x, g.reshape(1, D), wg, wu, wd]
    out_shape = [jax.ShapeDtypeStruct((R, D), F32)]
    out_specs = [pl.BlockSpec((R, D), lambda j: (0, 0))]
    if final_norm:
        in_specs.append(pl.BlockSpec((1, D), lambda j: (0, 0)))
        args.append(gf.reshape(1, D))
        out_shape.append(jax.ShapeDtypeStruct((R, D), F32))
        out_specs.append(pl.BlockSpec((R, D), lambda j: (0, 0)))
    out_shape += [jax.ShapeDtypeStruct((D, FF), BF16), jax.ShapeDtypeStruct((D, FF), BF16),
                  jax.ShapeDtypeStruct((FF, D), BF16)]
    out_specs += [pl.BlockSpec((D, tf), lambda j: (0, j)), pl.BlockSpec((D, tf), lambda j: (0, j)),
                  pl.BlockSpec((tf, D), lambda j: (j, 0))]
    return pl.pallas_call(
        functools.partial(_ffn_cast_body, final_norm=final_norm),
        grid=(FF // tf,),
        in_specs=in_specs,
        out_specs=out_specs,
        out_shape=out_shape,
        scratch_shapes=[pltpu.VMEM((R, D), BF16)],
        compiler_params=_cparams(("arbitrary",)),
        name="ffn_cast",
    )(*args)


def _in_proj_body(x_ref, g_ref, w_ref, wt_ref, ua_ref, ub_ref, c1_ref, k_ref, v_ref, ki_ref, tail_ref, c2b_ref,
                  *, q_scale, cols):
    h = _rms(x_ref[...], g_ref[...]).astype(BF16)
    o_a, o_b, o_qc, o_kv, o_qi, o_end = cols
    kv_w = k_ref.shape[1]
    d_c = o_kv - o_qc
    ua_ref[...] = _dot(h, w_ref[0, :, o_a:o_b])
    ub_ref[...] = _dot(h, w_ref[0, :, o_b:o_qc])
    c1_ref[:, :d_c] = (_dot(h, w_ref[0, :, o_qc:o_kv]) * q_scale).astype(BF16)
    c1_ref[:, d_c:] = _dot(h, w_ref[0, :, o_qi:o_end]).astype(BF16)
    kv = _dot(h, w_ref[0, :, o_kv:o_qi])
    t = _dot(h, wt_ref[0])
    k_ref[...] = kv[:, :kv_w]
    v_ref[...] = kv[:, kv_w:]
    ki_ref[...] = t[:, :IDX_DIM]
    tail_ref[...] = t
    c2b_ref[:, :2 * kv_w] = kv.astype(BF16)
    c2b_ref[:, 2 * kv_w:] = t.astype(BF16)


def _in_proj(x, g, w_all, w_tail, layer, *, tm, cols, q_scale):
    R, D = x.shape
    o_a, o_b, o_qc, o_kv, o_qi, o_end = cols
    kv_w = N_KV_HEADS * HEAD_DIM_C
    tw = w_tail.shape[2]
    outs = [(o_b - o_a, F32), (o_qc - o_b, F32), (o_kv - o_qc + o_end - o_qi, BF16), (kv_w, F32), (kv_w, F32),
            (IDX_DIM, F32), (tw, F32), (2 * kv_w + tw, BF16)]
    return pl.pallas_call(
        functools.partial(_in_proj_body, q_scale=q_scale, cols=cols),
        grid=(R // tm,),
        in_specs=[pl.BlockSpec((tm, D), lambda i: (i, 0)), pl.BlockSpec((1, D), lambda i: (0, 0)),
                  pl.BlockSpec((1,) + w_all.shape[1:], lambda i: (layer, 0, 0), pipeline_mode=pl.Buffered(1)),
                  pl.BlockSpec((1,) + w_tail.shape[1:], lambda i: (layer, 0, 0), pipeline_mode=pl.Buffered(1))],
        out_specs=[pl.BlockSpec((tm, wd), lambda i: (i, 0)) for wd, _ in outs],
        out_shape=[jax.ShapeDtypeStruct((R, wd), dt) for wd, dt in outs],
        compiler_params=_cparams(("parallel",)),
        name="in_proj",
    )(x, g.reshape(1, D), w_all, w_tail)


def _oproj_body(x_ref, ya_ref, yb_ref, yc_ref, w_ref, o_ref):
    da, db = ya_ref.shape[1], yb_ref.shape[1]
    o_ref[...] = (x_ref[...] + _dot(ya_ref[...], w_ref[:da]) + _dot(yb_ref[...], w_ref[da:da + db])
                  + _dot(yc_ref[...], w_ref[da + db:]))


def _out_proj(x, ya, yb, yc, w, *, tm):
    R, D = x.shape
    row = lambda a: pl.BlockSpec((tm, a.shape[1]), lambda i: (i, 0))
    return pl.pallas_call(
        _oproj_body,
        grid=(R // tm,),
        in_specs=[row(x), row(ya), row(yb), row(yc),
                  pl.BlockSpec(w.shape, lambda i: (0, 0), pipeline_mode=pl.Buffered(1))],
        out_specs=row(x),
        out_shape=jax.ShapeDtypeStruct((R, D), F32),
        compiler_params=_cparams(("parallel",)),
        name="out_proj",
    )(x, ya, yb, yc, w)


def _lru_body(xa_ref, ga_ref, conv0_ref, h0_ref, cw_ref, cb_ref, wa_ref, ba_ref, wx_ref, bx_ref, lam_ref,
              ya_ref, hl_ref, cn_ref, xs_ref, a_ref, u_ref, hs_ref, hc_ref, *, tc, t_valid_last):
    t = pl.program_id(1)
    da = xa_ref.shape[-1]
    blk = da // N_A_BLOCKS
    tail = 8

    @pl.when(t == 0)
    def _():
        xs_ref[0:tail, :] = conv0_ref[0]
        hc_ref[0:1, :] = h0_ref[0]

    xs_ref[tail:tail + tc, :] = xa_ref[0]
    xc = cb_ref[...]
    for j in range(CONV_WIDTH):
        off = tail - (CONV_WIDTH - 1) + j
        xc = xc + xs_ref[off:off + tc, :] * cw_ref[j:j + 1, :]
    xcb = xc.astype(BF16)
    pre_r = jnp.concatenate([_dot(xcb[:, n * blk:(n + 1) * blk], wa_ref[n]) for n in range(N_A_BLOCKS)], axis=1)
    pre_i = jnp.concatenate([_dot(xcb[:, n * blk:(n + 1) * blk], wx_ref[n]) for n in range(N_A_BLOCKS)], axis=1)
    r = jax.nn.sigmoid(pre_r + ba_ref[...])
    gi = jax.nn.sigmoid(pre_i + bx_ref[...])
    z = -lam_ref[...]
    softplus = jnp.maximum(z, 0.0) + jnp.log1p(jnp.exp(-jnp.abs(z)))
    log_a = (-LRU_C) * r * softplus
    a = jnp.exp(log_a)
    u = jnp.sqrt(-jnp.tanh(log_a) * (a * a + 1.0)) * gi * xc

    row_in_group = lax.broadcasted_iota(I32, (tc, da), 0) % 8
    for s in (1, 2, 4):
        first = row_in_group < s
        a_lo = jnp.where(first, 1.0, pltpu.roll(a, s, 0))
        u_lo = jnp.where(first, 0.0, pltpu.roll(u, s, 0))
        u = a * u_lo + u
        a = a * a_lo
    a_ref[...] = a
    u_ref[...] = u

    def step(k, h):
        base = pl.multiple_of(k * 8, 8)
        hg = a_ref[pl.ds(base, 8), :] * h + u_ref[pl.ds(base, 8), :]
        hs_ref[pl.ds(base, 8), :] = hg
        return hg[7:8, :]

    h = lax.fori_loop(0, tc // 8, step, hc_ref[0:1, :])
    hc_ref[0:1, :] = h
    ya_ref[0] = (hs_ref[...] * jax.nn.gelu(ga_ref[0])).astype(ya_ref.dtype)

    @pl.when(t == pl.num_programs(1) - 1)
    def _():
        hl_ref[0] = hs_ref[t_valid_last - 1:t_valid_last, :]
        lo = tail - (CONV_WIDTH - 1) + t_valid_last
        cn_ref[0] = xs_ref[lo:lo + CONV_WIDTH - 1, :]

    xs_ref[0:tail, :] = xs_ref[tc:tc + tail, :]


def _lru(ua, conv0, h0, cw, cb, wa, ba, wx, bx, lam, *, tc, t_valid_last):
    B, T, da2 = ua.shape
    da = da2 // 2
    vec = lambda: pl.BlockSpec((1, da), lambda b, t: (0, 0))
    blk = da // N_A_BLOCKS
    return pl.pallas_call(
        functools.partial(_lru_body, tc=tc, t_valid_last=t_valid_last),
        grid=(B, T // tc),
        in_specs=[
            pl.BlockSpec((1, tc, da), lambda b, t: (b, t, 0)),
            pl.BlockSpec((1, tc, da), lambda b, t: (b, t, 1)),
            pl.BlockSpec((1, 8, da), lambda b, t: (b, 0, 0)),
            pl.BlockSpec((1, 1, da), lambda b, t: (b, 0, 0)),
            pl.BlockSpec((CONV_WIDTH, da), lambda b, t: (0, 0)),
            vec(),
            pl.BlockSpec((N_A_BLOCKS, blk, blk), lambda b, t: (0, 0, 0)),
            vec(),
            pl.BlockSpec((N_A_BLOCKS, blk, blk), lambda b, t: (0, 0, 0)),
            vec(),
            vec(),
        ],
        out_specs=[
            pl.BlockSpec((1, tc, da), lambda b, t: (b, t, 0)),
            pl.BlockSpec((1, 1, da), lambda b, t: (b, 0, 0)),
            pl.BlockSpec((1, CONV_WIDTH - 1, da), lambda b, t: (b, 0, 0)),
        ],
        out_shape=[
            jax.ShapeDtypeStruct((B, T, da), BF16),
            jax.ShapeDtypeStruct((B, 1, da), F32),
            jax.ShapeDtypeStruct((B, CONV_WIDTH - 1, da), F32),
        ],
        scratch_shapes=[
            pltpu.VMEM((tc + 8, da), F32),
            pltpu.VMEM((tc, da), F32),
            pltpu.VMEM((tc, da), F32),
            pltpu.VMEM((tc, da), F32),
            pltpu.VMEM((8, da), F32),
        ],
        compiler_params=_cparams(("arbitrary", "arbitrary")),
        name="rglru",
    )(ua, ua, conv0, h0, cw, cb.reshape(1, da), wa, ba.reshape(1, da), wx, bx.reshape(1, da), lam.reshape(1, da))


def _ret_body(q_ref, k_ref, v_ref, gb_ref, cos_ref, sin_ref, s0_ref, yb_ref, sl_ref, s_ref, *, cp, c_valid, mm_dtype):
    t = pl.program_id(1)
    dk = q_ref.shape[-1] // N_RET_HEADS
    dv = v_ref.shape[-1] // N_RET_HEADS

    @pl.when(t == 0)
    def _():
        s_ref[...] = s0_ref[0]

    cos = cos_ref[...]
    sin = sin_ref[...]
    first_half = (lax.broadcasted_iota(I32, cos.shape, 1) % dk) < (dk // 2)
    width = cos.shape[1]

    def rot(x):
        partner = jnp.where(first_half, pltpu.roll(x, width - dk // 2, 1), pltpu.roll(x, dk // 2, 1))
        return x * cos + partner * sin

    row = lax.broadcasted_iota(I32, (cp, 1), 0)
    q_all = rot(q_ref[0])
    k_all = rot(k_ref[0]) * (dk ** -0.5)
    ri = lax.broadcasted_iota(I32, (cp, cp), 0)
    ci = lax.broadcasted_iota(I32, (cp, cp), 1)
    diff = (ri - ci).astype(F32)
    rowf = row.astype(F32)
    for h in range(N_RET_HEADS):
        lg = math.log1p(-(2.0 ** (-5.0 - h)))
        dmask = jnp.where(diff >= 0, jnp.exp(jnp.maximum(diff, 0.0) * lg), 0.0)
        cross_dec = jnp.exp((rowf + 1.0) * lg)
        state_dec = jnp.exp((c_valid - 1.0 - rowf) * lg)
        chunk_dec = math.exp(c_valid * lg)
        state = s_ref[h]
        for n in range(q_all.shape[0] // cp):
            rows = slice(n * cp, (n + 1) * cp)
            qh = q_all[rows, h * dk:(h + 1) * dk].astype(mm_dtype)
            kh = k_all[rows, h * dk:(h + 1) * dk]
            if c_valid < cp:
                kh = jnp.where(row < c_valid, kh, 0.0)
            vh = v_ref[0, rows, h * dv:(h + 1) * dv].astype(mm_dtype)
            att = _dot_nt(qh, kh.astype(mm_dtype)) * dmask
            inner = _dot(att.astype(mm_dtype), vh)
            cross = _dot(qh, state.astype(mm_dtype)) * cross_dec
            state = state * chunk_dec + _dot_tn((kh * state_dec).astype(mm_dtype), vh)
            o = inner + cross
            o = o * lax.rsqrt(jnp.mean(o * o, axis=-1, keepdims=True) + EPS)
            gh = gb_ref[0, rows, h * dv:(h + 1) * dv]
            yb_ref[0, rows, h * dv:(h + 1) * dv] = (o * (gh * jax.nn.sigmoid(gh))).astype(yb_ref.dtype)
        s_ref[h] = state

    @pl.when(t == pl.num_programs(1) - 1)
    def _():
        sl_ref[0] = s_ref[...]


def _retention(ub, cos, sin, s0, *, cp, c_valid, mm_dtype, chunks_per_step=1):
    B, T, wtot = ub.shape
    w = wtot // 6
    H, dk, dv = s0.shape[1:]
    blk = cp * chunks_per_step
    return pl.pallas_call(
        functools.partial(_ret_body, cp=cp, c_valid=c_valid, mm_dtype=mm_dtype),
        grid=(B, T // blk),
        in_specs=[
            pl.BlockSpec((1, blk, w), lambda b, t: (b, t, 0)),
            pl.BlockSpec((1, blk, w), lambda b, t: (b, t, 1)),
            pl.BlockSpec((1, blk, 2 * w), lambda b, t: (b, t, 1)),
            pl.BlockSpec((1, blk, 2 * w), lambda b, t: (b, t, 2)),
            pl.BlockSpec((blk, w), lambda b, t: (t, 0)),
            pl.BlockSpec((blk, w), lambda b, t: (t, 0)),
            pl.BlockSpec((1, H, dk, dv), lambda b, t: (b, 0, 0, 0)),
        ],
        out_specs=[
            pl.BlockSpec((1, blk, 2 * w), lambda b, t: (b, t, 0)),
            pl.BlockSpec((1, H, dk, dv), lambda b, t: (b, 0, 0, 0)),
        ],
        out_shape=[
            jax.ShapeDtypeStruct((B, T, 2 * w), BF16),
            jax.ShapeDtypeStruct((B, H, dk, dv), F32),
        ],
        scratch_shapes=[pltpu.VMEM((H, dk, dv), F32)],
        compiler_params=_cparams(("arbitrary", "arbitrary")),
        name="retention",
    )(ub, ub, ub, ub, cos, sin, s0)


def _rope_tables(pos, dk, heads):
    half = dk // 2
    freqs = ROPE_BASE ** (-jnp.arange(half, dtype=F32) / half)
    ang = pos.astype(F32)[:, None] * freqs[None, :]
    cos, sin = jnp.cos(ang), jnp.sin(ang)
    cos_t = jnp.tile(jnp.concatenate([cos, cos], axis=1), (1, heads))
    sin_t = jnp.tile(jnp.concatenate([-sin, sin], axis=1), (1, heads))
    return cos_t, sin_t


def _t5_bucket_np(dist):
    n = np.maximum(dist, 0)
    max_exact = NUM_BUCKETS // 2
    ratio = np.log(np.maximum(n, 1).astype(np.float32) / np.float32(max_exact)) / np.float32(math.log(MAX_DISTANCE / max_exact))
    large = max_exact + (ratio * np.float32(NUM_BUCKETS - max_exact)).astype(np.int32)
    large = np.minimum(large, NUM_BUCKETS - 1)
    return np.where(n < max_exact, n, large).astype(np.int32)


def _bias_from_buckets(bucket, rb_ref, head):
    def step(b, out):
        return jnp.where(bucket == b, rb_ref[b, head], out)

    return lax.fori_loop(0, NUM_BUCKETS, step, jnp.zeros(bucket.shape, F32))


def _dsa_prompt_body(rb_ref, qc_ref, qi_ref, wi_ref, kvb_ref, btab_ref, o_ref,
                     keys_ref, bt_ref, wf_ref, qis_ref, qs_ref, mx_ref, mrep_ref, acc_ref, cut_ref, kmax_ref, sel_ref,
                     *, tq, n_keep, idx_bits):
    i = pl.program_id(0)
    G = qc_ref.shape[1] // (N_KV_HEADS * HEAD_DIM_C)
    n_heads = N_KV_HEADS * G
    kcol, vcol, icol = 0, N_KV_HEADS * HEAD_DIM_C, 2 * N_KV_HEADS * HEAD_DIM_C
    lanes = HEAD_DIM_C
    halves = tq // lanes
    n_tiles = i + 1

    @pl.when(i == 0)
    def _():
        for h in range(n_heads):
            far = rb_ref[NUM_BUCKETS - 1, h]
            for r in range(2):
                bt_ref[h // G, r, h % G] = _bias_from_buckets(btab_ref[r], rb_ref, h) - far

    for h in range(N_IDX_HEADS):
        qis_ref[h] = qi_ref[:, h * IDX_DIM:(h + 1) * IDX_DIM]
        wf_ref[h] = jnp.broadcast_to(wi_ref[:, IDX_DIM + h:IDX_DIM + h + 1], (tq, lanes))
    for c in range(N_KV_HEADS):
        for g in range(G):
            h = c * G + g
            qs_ref[c, g * tq:(g + 1) * tq, :] = qc_ref[:, h * HEAD_DIM_C:(h + 1) * HEAD_DIM_C]

    rowi = lax.broadcasted_iota(I32, (tq, tq), 0)
    coli = lax.broadcasted_iota(I32, (tq, tq), 1)

    def key_index(j):
        return j * tq + coli

    def visible(j):
        return key_index(j) <= (i * tq + rowi)

    def score_tile(j, carry):
        ki = kvb_ref[pl.ds(pl.multiple_of(j * tq, tq), tq), icol:icol + IDX_DIM]
        for rh in range(halves):
            rows = slice(rh * lanes, (rh + 1) * lanes)
            acc = jnp.zeros((lanes, tq), F32)
            for h in range(N_IDX_HEADS):
                w = wf_ref[h, rows, :]
                acc = acc + jnp.maximum(_dot_nt(qis_ref[h, rows, :], ki), 0.0) * jnp.concatenate([w] * halves, axis=1)
            score = acc * (N_IDX_HEADS ** -0.5 * IDX_DIM ** -0.5)
            score = jnp.where(visible(j)[rows], score, -jnp.inf)
            keys_ref[j, :, rows] = _sort_key(score).T
        return carry

    lax.fori_loop(0, n_tiles // 2, lambda p, carry: score_tile(2 * p + 1, score_tile(2 * p, carry)), 0)
    lax.fori_loop(2 * (n_tiles // 2), n_tiles, score_tile, 0)

    sub = 8
    grp = lax.broadcasted_iota(I32, (tq // sub, sub, tq), 0)
    srow = lax.broadcasted_iota(I32, (tq // sub, sub, tq), 1)

    def count(pred, *row_args):
        args = [a[None] for a in row_args]

        def one(j, cnt):
            k = keys_ref[j].reshape(tq // sub, sub, tq)
            idx = j * tq + grp * sub + srow
            return cnt + jnp.sum(jnp.where(pred(k, idx, *args), 1, 0), axis=0)

        n_pairs = n_tiles // 2
        cnt = lax.fori_loop(0, n_pairs, lambda p, c: one(2 * p + 1, one(2 * p, c)), jnp.zeros((sub, tq), I32))
        cnt = lax.fori_loop(2 * n_pairs, n_tiles, one, cnt)
        return jnp.broadcast_to(jnp.sum(cnt, axis=0, keepdims=True), (sub, tq))

    def bit_step(b, st):
        tau, n_ge = st
        cand = tau + lax.shift_left(jnp.int32(1), 31 - b)
        c = count(lambda k, idx, cnd: k >= cnd, cand)
        ge = c >= n_keep
        return jnp.where(ge, cand, tau), jnp.where(ge, c, n_ge)

    sel_ref[0] = jnp.full((sub, tq), INT_MIN, I32)
    sel_ref[1] = jnp.broadcast_to(n_tiles * tq, (sub, tq)).astype(I32)
    for b0, b1 in zip(SELECT_BIT_GROUPS[:-1], SELECT_BIT_GROUPS[1:]):
        @pl.when(jnp.max(sel_ref[1]) > n_keep)
        def _():
            tau_g, n_ge_g = lax.fori_loop(b0, b1, bit_step, (sel_ref[0], sel_ref[1]))
            sel_ref[0] = tau_g
            sel_ref[1] = n_ge_g

    tau, n_ge = sel_ref[0], sel_ref[1]

    cut_ref[...] = jnp.full((sub, tq), 2 ** 31 - 1, I32)

    @pl.when(jnp.max(n_ge) > n_keep)
    def _():
        need = n_keep - count(lambda k, idx, t: k > t, tau)

        def idx_step(b, cut):
            cand = cut + lax.shift_left(jnp.int32(1), idx_bits - 1 - b)
            below = count(lambda k, idx, t, c: (k == t) & (idx < c), tau, cand)
            return jnp.where(below < need, cand, cut)

        cut_ref[...] = lax.fori_loop(0, idx_bits, idx_step, jnp.zeros((sub, tq), I32))

    tau_t = jnp.broadcast_to(tau[0:1], (tq, tq)).T
    cut_t = jnp.broadcast_to(cut_ref[0:1, :], (tq, tq)).T

    def mask_bias(j, r):
        k = keys_ref[j].T
        sel = (k > tau_t) | ((k == tau_t) & (key_index(j) <= cut_t))
        if r is not None:
            sel = sel & visible(j)
        return jnp.where(sel, 0.0, NEG_BIG)

    def logits(j, c, r, mb):
        base = pl.multiple_of(j * tq, tq)
        kc = kvb_ref[pl.ds(base, tq), kcol + c * HEAD_DIM_C:kcol + (c + 1) * HEAD_DIM_C]
        s = _dot_nt(qs_ref[c], kc).reshape(G, tq, tq) + mb[None]
        if r is not None:
            s = s + bt_ref[c, r]
        return s

    def max_tile(j, r):
        mb = mask_bias(j, r)
        for c in range(N_KV_HEADS):
            s = logits(j, c, r, mb)
            m = s[..., 0:lanes]
            for a in range(1, halves):
                m = jnp.maximum(m, s[..., a * lanes:(a + 1) * lanes])
            mx_ref[c] = jnp.maximum(mx_ref[c], m)

    def acc_tile(j, r):
        base = pl.multiple_of(j * tq, tq)
        mb = mask_bias(j, r)
        for c in range(N_KV_HEADS):
            vc = kvb_ref[pl.ds(base, tq), vcol + c * HEAD_DIM_C:vcol + (c + 1) * HEAD_DIM_C]
            vext = jnp.concatenate([vc, jnp.ones((tq, lanes), BF16)], axis=1)
            m = mrep_ref[c]
            p = jnp.exp(logits(j, c, r, mb) - jnp.concatenate([m] * halves, axis=-1))
            acc_ref[c] += _dot(p.reshape(G * tq, tq).astype(BF16), vext)

    def sweep(tile_fn):
        def far(j, carry):
            tile_fn(j, None)
            return carry

        n_far = jnp.maximum(i - 1, 0)
        lax.fori_loop(0, n_far // 2, lambda p, carry: far(2 * p + 1, far(2 * p, carry)), 0)
        lax.fori_loop(2 * (n_far // 2), n_far, far, 0)

        @pl.when(i >= 1)
        def _():
            tile_fn(i - 1, 1)
            tile_fn(i, 0)

        @pl.when(i == 0)
        def _():
            tile_fn(0, 0)

    @pl.when(i == 0)
    def _():
        kmax_ref[...] = jnp.zeros(kmax_ref.shape, F32)

    for c in range(N_KV_HEADS):
        kt = kvb_ref[pl.ds(pl.multiple_of(i * tq, tq), tq), kcol + c * HEAD_DIM_C:kcol + (c + 1) * HEAD_DIM_C].astype(F32)
        knorm = jnp.sqrt(jnp.max(jnp.sum(kt * kt, axis=1, keepdims=True)))
        kmax_ref[c] = jnp.maximum(kmax_ref[c], knorm)
        qf = qs_ref[c].astype(F32)
        qnorm = jnp.sqrt(jnp.sum(qf * qf, axis=1, keepdims=True)).reshape(G, tq, 1)
        for g in range(G):
            h = c * G + g
            far = rb_ref[NUM_BUCKETS - 1, h]
            bias_max = lax.fori_loop(0, NUM_BUCKETS, lambda b, m: jnp.maximum(m, rb_ref[b, h] - far), jnp.float32(0.0))
            mrep_ref[c, g] = qnorm[g] * kmax_ref[c, 0:1, :] + bias_max

    acc_ref[...] = jnp.zeros(acc_ref.shape, F32)
    sweep(acc_tile)

    @pl.when(jnp.logical_not(jnp.min(acc_ref[:, :, lanes:]) > 1e-30))
    def _():
        mx_ref[...] = jnp.full(mx_ref.shape, NEG_BIG, F32)
        sweep(max_tile)
        mrep_ref[...] = jnp.broadcast_to(jnp.max(mx_ref[...], axis=-1, keepdims=True), mrep_ref.shape)
        acc_ref[...] = jnp.zeros(acc_ref.shape, F32)
        sweep(acc_tile)

    for c in range(N_KV_HEADS):
        for g in range(G):
            h = c * G + g
            a = acc_ref[c, g * tq:(g + 1) * tq, :]
            o_ref[:, h * HEAD_DIM_C:(h + 1) * HEAD_DIM_C] = (a[:, :lanes] / a[:, lanes:]).astype(o_ref.dtype)


def _dsa_prompt(qcqi, tail, kvi_bf16, rel_bias, *, tq, n_keep):
    S = qcqi.shape[0]
    dc = qcqi.shape[1] - N_IDX_HEADS * IDX_DIM
    n_heads = dc // HEAD_DIM_C
    G = n_heads // N_KV_HEADS
    wkv = kvi_bf16.shape[1]
    d = np.arange(tq)[:, None] - np.arange(tq)[None, :]
    btab = jnp.asarray(np.stack([_t5_bucket_np(d + r * tq) for r in range(2)]))
    assert _t5_bucket_np(np.array([tq + 1]))[0] == NUM_BUCKETS - 1
    lanes = HEAD_DIM_C
    grid_spec = pltpu.PrefetchScalarGridSpec(
        num_scalar_prefetch=0,
        grid=(S // tq,),
        in_specs=[
            pl.BlockSpec(memory_space=pltpu.SMEM),
            pl.BlockSpec((tq, dc), lambda i: (i, 0)),
            pl.BlockSpec((tq, N_IDX_HEADS * IDX_DIM), lambda i: (i, dc // (N_IDX_HEADS * IDX_DIM))),
            pl.BlockSpec((tq, tail.shape[1]), lambda i: (i, 0)),
            pl.BlockSpec((S, wkv), lambda i: (0, 0), pipeline_mode=pl.Buffered(1)),
            pl.BlockSpec((2, tq, tq), lambda i: (0, 0, 0)),
        ],
        out_specs=pl.BlockSpec((tq, dc), lambda i: (i, 0)),
        scratch_shapes=[
            pltpu.VMEM((S // tq, tq, tq), I32),
            pltpu.VMEM((N_KV_HEADS, 2, G, tq, tq), F32),
            pltpu.VMEM((N_IDX_HEADS, tq, lanes), F32),
            pltpu.VMEM((N_IDX_HEADS, tq, IDX_DIM), BF16),
            pltpu.VMEM((N_KV_HEADS, G * tq, HEAD_DIM_C), BF16),
            pltpu.VMEM((N_KV_HEADS, G, tq, lanes), F32),
            pltpu.VMEM((N_KV_HEADS, G, tq, lanes), F32),
            pltpu.VMEM((N_KV_HEADS, G * tq, 2 * lanes), F32),
            pltpu.VMEM((8, tq), I32),
            pltpu.VMEM((N_KV_HEADS, 8, lanes), F32),
            pltpu.VMEM((2, 8, tq), I32),
        ],
    )
    return pl.pallas_call(
        functools.partial(_dsa_prompt_body, tq=tq, n_keep=n_keep, idx_bits=(S - 1).bit_length()),
        grid_spec=grid_spec,
        out_shape=jax.ShapeDtypeStruct((S, dc), BF16),
        compiler_params=_cparams(("arbitrary",)),
        name="dsa_prompt",
    )(rel_bias, qcqi, qcqi, tail, kvi_bf16, btab)


def _dsa_sel_body(pt_ref, qi_ref, w_ref, kin_ref, *rest, t_new, n_keep):
    pages = rest[:PAGES_PER_STEP]
    mb_ref, keys_ref, arg_ref, cnt_ref = rest[PAGES_PER_STEP:]
    b = pl.program_id(0)
    p = pl.program_id(1)
    B, n_chunks, tp, P = keys_ref.shape
    last = p == pl.num_programs(1) - 1
    q = qi_ref[0]
    w = w_ref[0]
    rowi = lax.broadcasted_iota(I32, (tp, P), 0)
    coli = lax.broadcasted_iota(I32, (tp, P), 1)
    vis_new = (coli <= rowi) & (coli < t_new)

    def chunk_scores(s):
        s = jnp.maximum(s, 0.0) * w
        s = s.reshape(tp, N_IDX_HEADS, s.shape[-1]).sum(axis=1)
        return s * (N_IDX_HEADS ** -0.5 * IDX_DIM ** -0.5)

    for r in range(PAGES_PER_STEP):
        keys_ref[b, p * PAGES_PER_STEP + r] = _sort_key(chunk_scores(_dot(q, pages[r][0].astype(BF16))))

    @pl.when(last)
    def _():
        s_new = jnp.where(vis_new, chunk_scores(_dot_nt(q, kin_ref[0])), -jnp.inf)
        keys_ref[b, n_chunks - 1] = _sort_key(s_new)

    @pl.when(last & (b == B - 1))
    def _():
        key_index = (lax.broadcasted_iota(I32, (n_chunks, tp, P), 0) * P
                     + lax.broadcasted_iota(I32, (n_chunks, tp, P), 2))

        def count(pred, *row_args):
            for n, a in enumerate(row_args):
                arg_ref[n] = a

            def per_seq(s, carry):
                args = [arg_ref[n, s][None] for n in range(len(row_args))]
                cnt_ref[s] = jnp.sum(jnp.where(pred(keys_ref[s], key_index, *args), 1, 0), axis=0)
                return carry

            lax.fori_loop(0, B, per_seq, 0)
            return jnp.broadcast_to(jnp.sum(cnt_ref[...], axis=-1, keepdims=True), (B, tp, P))

        def bit_step(bit, tau):
            cand = tau + lax.shift_left(jnp.int32(1), 31 - bit)
            return jnp.where(count(lambda k, idx, c: k >= c, cand) >= n_keep, cand, tau)

        tau = lax.fori_loop(0, 32, bit_step, jnp.full((B, tp, P), INT_MIN, I32))

        arg_ref[2] = jnp.full((B, tp, P), 2 ** 31 - 1, I32)

        @pl.when(jnp.max(count(lambda k, idx, t: k >= t, tau)) > n_keep)
        def _():
            need = n_keep - count(lambda k, idx, t: k > t, tau)
            idx_bits = (n_chunks * P - 1).bit_length()

            def idx_step(bit, cut):
                cand = cut + lax.shift_left(jnp.int32(1), idx_bits - 1 - bit)
                below = count(lambda k, idx, t, c: (k == t) & (idx < c), tau, cand)
                return jnp.where(below < need, cand, cut)

            arg_ref[2] = lax.fori_loop(0, idx_bits, idx_step, jnp.zeros((B, tp, P), I32))

        arg_ref[0] = tau

        def write_mask(s, carry):
            k = keys_ref[s]
            t = arg_ref[0, s][None]
            sel = (k > t) | ((k == t) & (key_index <= arg_ref[2, s][None]))
            mb_ref[s] = jnp.where(sel, 0.0, NEG_BIG)
            mb_ref[s, n_chunks - 1] = jnp.where(sel[n_chunks - 1] & vis_new, 0.0, NEG_BIG)
            return carry

        lax.fori_loop(0, B, write_mask, 0)


def _dsa_sample_select(pt_flat, qi, w, ki_new, pool_ki, *, n_pages, t_new, n_keep):
    B = qi.shape[0]
    P = pool_ki.shape[2]
    steps = n_pages // PAGES_PER_STEP

    def page_spec(r):
        return pl.BlockSpec((1, IDX_DIM, P), lambda b, p, pt: (pt[b * n_pages + p * PAGES_PER_STEP + r], 0, 0))

    grid_spec = pltpu.PrefetchScalarGridSpec(
        num_scalar_prefetch=1,
        grid=(B, steps),
        in_specs=[
            pl.BlockSpec((1,) + qi.shape[1:], lambda b, p, pt: (b, 0, 0)),
            pl.BlockSpec((1,) + w.shape[1:], lambda b, p, pt: (b, 0, 0)),
            pl.BlockSpec((1,) + ki_new.shape[1:], lambda b, p, pt: (b, 0, 0)),
        ] + [page_spec(r) for r in range(PAGES_PER_STEP)],
        out_specs=pl.BlockSpec((B, n_pages + 1, SAMPLE_PAD_T, P), lambda b, p, pt: (0, 0, 0, 0)),
        scratch_shapes=[
            pltpu.VMEM((B, n_pages + 1, SAMPLE_PAD_T, P), I32),
            pltpu.VMEM((3, B, SAMPLE_PAD_T, P), I32),
            pltpu.VMEM((B, SAMPLE_PAD_T, P), I32),
        ],
    )
    return pl.pallas_call(
        functools.partial(_dsa_sel_body, t_new=t_new, n_keep=n_keep),
        grid_spec=grid_spec,
        out_shape=jax.ShapeDtypeStruct((B, n_pages + 1, SAMPLE_PAD_T, P), F32),
        compiler_params=_cparams(("arbitrary", "arbitrary")),
        name="dsa_sample_select",
    )(pt_flat, qi, w, ki_new, *([pool_ki] * PAGES_PER_STEP))


def _dsa_att_body(pt_ref, rb_ref, q_ref, mb_ref, kn_ref, vn_ref, btab_ref, *rest):
    kp = rest[:PAGES_PER_STEP]
    vp = rest[PAGES_PER_STEP:2 * PAGES_PER_STEP]
    o_ref, bt_ref, m_ref, acc_ref = rest[2 * PAGES_PER_STEP:]
    b = pl.program_id(0)
    p = pl.program_id(1)
    n_chunks = mb_ref.shape[1]
    tp = SAMPLE_PAD_T
    G = q_ref.shape[2] // tp
    rows = G * tp
    P = kn_ref.shape[1]
    lanes = HEAD_DIM_C
    last = p == pl.num_programs(1) - 1

    @pl.when((b == 0) & (p == 0))
    def _():
        for c in range(N_KV_HEADS):
            for kind in range(2):
                for g in range(G):
                    h = c * G + g
                    bt_ref[c, kind, g * tp:(g + 1) * tp, :] = (_bias_from_buckets(btab_ref[kind], rb_ref, h)
                                                              - rb_ref[NUM_BUCKETS - 1, h])

    @pl.when(p == 0)
    def _():
        m_ref[...] = jnp.full(m_ref.shape, NEG_BIG, F32)
        acc_ref[...] = jnp.zeros(acc_ref.shape, F32)

    def attend(c, s, vext):
        m_old = m_ref[c]
        m_new = jnp.maximum(m_old, jnp.max(s, axis=1, keepdims=True))
        alpha = jnp.exp(m_old - m_new)
        pr = jnp.exp(s - jnp.concatenate([m_new] * (s.shape[1] // lanes), axis=1))
        acc_ref[c] = jnp.concatenate([alpha, alpha], axis=1) * acc_ref[c] + _dot(pr.astype(BF16), vext)
        m_ref[c] = m_new

    ones = jnp.ones((P, lanes), BF16)
    mb_step = jnp.concatenate([mb_ref[0, p * PAGES_PER_STEP + r] for r in range(PAGES_PER_STEP)], axis=1)
    mb_step = jnp.concatenate([mb_step] * G, axis=0)
    for c in range(N_KV_HEADS):
        head_rows = pl.ds(c, P, stride=N_KV_HEADS)
        k_all = jnp.concatenate([kp[r][0, head_rows, :].astype(BF16) for r in range(PAGES_PER_STEP)], axis=0)
        v_all = jnp.concatenate(
            [jnp.concatenate([vp[r][0, head_rows, :].astype(BF16), ones], axis=1) for r in range(PAGES_PER_STEP)], axis=0)
        s = _dot_nt(q_ref[0, c], k_all) + mb_step
        s = jnp.concatenate([s[:, :-P], s[:, -P:] + jnp.where(last, bt_ref[c, 0], 0.0)], axis=1)
        attend(c, s, v_all)

    @pl.when(last)
    def _():
        mb_new = jnp.concatenate([mb_ref[0, n_chunks - 1]] * G, axis=0)
        for c in range(N_KV_HEADS):
            cols = slice(c * HEAD_DIM_C, (c + 1) * HEAD_DIM_C)
            s = _dot_nt(q_ref[0, c], kn_ref[0, :, cols]) + mb_new + bt_ref[c, 1]
            attend(c, s, jnp.concatenate([vn_ref[0, :, cols], ones], axis=1))
            a = acc_ref[c]
            o_ref[0, c] = a[:, :lanes] / a[:, lanes:]


def _dsa_sample_attend(pt_flat, rel_bias, q, mb, k_new, v_new, pool_k, pool_v, *, n_pages, past):
    B, kvh, rows, dh = q.shape
    P = pool_k.shape[1] // kvh
    steps = n_pages // PAGES_PER_STEP
    tp = SAMPLE_PAD_T
    t = np.arange(tp)[:, None]
    col = np.arange(P)[None, :]
    assert past - (n_pages - 1) * P >= MAX_DISTANCE
    last_page = _t5_bucket_np(past + t - ((n_pages - 1) * P + col))
    new = _t5_bucket_np(t - col)
    btab = jnp.asarray(np.stack([last_page, new]))

    def page_spec(r):
        return pl.BlockSpec((1, P * kvh, dh), lambda b, p, pt: (pt[b * n_pages + p * PAGES_PER_STEP + r], 0, 0))

    grid_spec = pltpu.PrefetchScalarGridSpec(
        num_scalar_prefetch=1,
        grid=(B, steps),
        in_specs=[
            pl.BlockSpec(memory_space=pltpu.SMEM),
            pl.BlockSpec((1, kvh, rows, dh), lambda b, p, pt: (b, 0, 0, 0)),
            pl.BlockSpec((1,) + mb.shape[1:], lambda b, p, pt: (b, 0, 0, 0)),
            pl.BlockSpec((1,) + k_new.shape[1:], lambda b, p, pt: (b, 0, 0)),
            pl.BlockSpec((1,) + v_new.shape[1:], lambda b, p, pt: (b, 0, 0)),
            pl.BlockSpec((2, tp, P), lambda b, p, pt: (0, 0, 0)),
        ] + [page_spec(r) for r in range(PAGES_PER_STEP)] * 2,
        out_specs=pl.BlockSpec((1, kvh, rows, dh), lambda b, p, pt: (b, 0, 0, 0)),
        scratch_shapes=[
            pltpu.VMEM((kvh, 2, rows, P), F32),
            pltpu.VMEM((kvh, rows, dh), F32),
            pltpu.VMEM((kvh, rows, 2 * dh), F32),
        ],
    )
    return pl.pallas_call(
        _dsa_att_body,
        grid_spec=grid_spec,
        out_shape=jax.ShapeDtypeStruct((B, kvh, rows, dh), F32),
        compiler_params=_cparams(("arbitrary", "arbitrary")),
        name="dsa_sample_attend",
    )(pt_flat, rel_bias, q, mb, k_new, v_new, btab, *([pool_k] * PAGES_PER_STEP), *([pool_v] * PAGES_PER_STEP))


def _dsa_sample(c1, tail, c2b, pt_flat, rel_bias, pool_k, pool_v, pool_ki, *, Bs, Ts, n_pages):
    tp = SAMPLE_PAD_T
    kv_w = N_KV_HEADS * HEAD_DIM_C
    d_c = c1.shape[1] - N_IDX_HEADS * IDX_DIM
    G = d_c // kv_w
    P = pool_ki.shape[2]
    past = n_pages * P
    qi_s = c1[:, d_c:].reshape(Bs, tp * N_IDX_HEADS, IDX_DIM)
    w_s = tail[:, IDX_DIM:IDX_DIM + N_IDX_HEADS].reshape(Bs, tp * N_IDX_HEADS, 1)
    new_rows = jnp.pad(c2b.reshape(Bs, tp, -1), ((0, 0), (0, P - tp), (0, 0)))
    mb = _dsa_sample_select(pt_flat, qi_s, w_s, new_rows[:, :, 2 * kv_w:2 * kv_w + IDX_DIM], pool_ki,
                            n_pages=n_pages, t_new=Ts, n_keep=min(TOPK_MAX, (past + Ts) // 4))
    q_s = c1[:, :d_c].reshape(Bs, tp, N_KV_HEADS, G, HEAD_DIM_C).transpose(0, 2, 3, 1, 4)
    q_s = q_s.reshape(Bs, N_KV_HEADS, G * tp, HEAD_DIM_C)
    o_s = _dsa_sample_attend(pt_flat, rel_bias, q_s, mb, new_rows[:, :, :kv_w], new_rows[:, :, kv_w:2 * kv_w],
                             pool_k, pool_v, n_pages=n_pages, past=past)
    return o_s.reshape(Bs, N_KV_HEADS, G, tp, HEAD_DIM_C).transpose(0, 3, 1, 2, 4).reshape(Bs * tp, d_c).astype(BF16)


def kernel(x_prompt, x_sample, cache_k, cache_v, cache_kidx, page_table, state_lru_h, state_conv, state_ret,
           norm_ffn1, ffn1_gate, ffn1_up, ffn1_down, norm_mix, w_in, w_out, conv_w, conv_b,
           lru_wa, lru_ba, lru_wx, lru_bx, lru_lambda, rel_bias,
           norm_ffn2, ffn2_gate, ffn2_up, ffn2_down, norm_final):
    depth = norm_ffn1.shape[0]
    _, S, D = x_prompt.shape
    Bs, Ts, _ = x_sample.shape
    n_pool, P = cache_k.shape[1], cache_k.shape[2]
    n_pages = page_table.shape[1]
    past = n_pages * P
    d_a = state_lru_h.shape[-1]
    H_r, rdk, rdv = state_ret.shape[2:]
    d_b = H_r * rdv
    d_c = D - d_a - d_b
    kv_w = N_KV_HEADS * HEAD_DIM_C
    qi_w = N_IDX_HEADS * IDX_DIM
    G = d_c // HEAD_DIM_C // N_KV_HEADS
    tp = SAMPLE_PAD_T
    Rs = Bs * tp

    xp = x_prompt.reshape(S, D)
    xs = jnp.pad(x_sample, ((0, 0), (0, tp - Ts), (0, 0))).reshape(Rs, D)

    cos_p, sin_p = _rope_tables(jnp.arange(S, dtype=I32), rdk, H_r)
    cos_s, sin_s = _rope_tables(past + jnp.arange(tp, dtype=I32), rdk, H_r)

    pool_k = cache_k.reshape(depth * n_pool, P * N_KV_HEADS, HEAD_DIM_C)
    pool_v = cache_v.reshape(depth * n_pool, P * N_KV_HEADS, HEAD_DIM_C)
    pool_ki = jnp.swapaxes(cache_kidx, 2, 3).reshape(depth * n_pool, IDX_DIM, P)

    o = np.cumsum([0, 2 * d_a, 2 * H_r * rdk + 2 * d_b, d_c, 2 * kv_w, qi_w, IDX_DIM + N_IDX_HEADS])
    proj_cols = tuple(int(c) for c in o[:6])
    w_in_b = w_in.astype(BF16)
    w_in_tail = jnp.pad(w_in_b[:, :, o[5]:o[6]], ((0, 0), (0, 0), (0, 128 - int(o[6] - o[5]))))

    zeros_conv = jnp.zeros((1, 8, d_a), F32)
    zeros_h = jnp.zeros((1, 1, d_a), F32)
    zeros_s = jnp.zeros((1, H_r, rdk, rdv), F32)

    outs_p, outs_s = [], []
    y_prompt = y_sample = None
    for l in range(depth):
        in_proj = functools.partial(_in_proj, g=norm_mix[l], w_all=w_in_b, w_tail=w_in_tail, layer=l,
                                    cols=proj_cols, q_scale=HEAD_DIM_C ** -0.5)
        wo = w_out[l].astype(BF16)
        lwa, lwx = lru_wa[l].astype(BF16), lru_wx[l].astype(BF16)
        last = l == depth - 1
        pt_flat = (page_table + l * n_pool).reshape(-1).astype(I32)

        xs, *f1 = _ffn_cast(xs, norm_ffn1[l], ffn1_gate, ffn1_up, ffn1_down, l, tf=512)
        ua, ub, c1, k_new, v_new, ki_new, tail, c2b = in_proj(xs, tm=Rs)
        conv0 = jnp.pad(state_conv[l], ((0, 0), (8 - (CONV_WIDTH - 1), 0), (0, 0)))
        ya, h_last, conv_new = _lru(ua.reshape(Bs, tp, 2 * d_a), conv0, state_lru_h[l].reshape(Bs, 1, d_a),
                                    conv_w[l], conv_b[l], lwa, lru_ba[l], lwx, lru_bx[l], lru_lambda[l],
                                    tc=tp, t_valid_last=Ts)
        yb, s_last = _retention(ub.reshape(Bs, tp, -1), cos_s, sin_s, state_ret[l], cp=tp, c_valid=Ts, mm_dtype=F32)
        yc = _dsa_sample(c1, tail, c2b, pt_flat, rel_bias, pool_k, pool_v, pool_ki, Bs=Bs, Ts=Ts, n_pages=n_pages)
        xs = _out_proj(xs, ya.reshape(Rs, d_a), yb.reshape(Rs, d_b), yc, wo, tm=Rs)
        if last:
            xs, y_sample, *f2 = _ffn_cast(xs, norm_ffn2[l], ffn2_gate, ffn2_up, ffn2_down, l, norm_final, tf=512)
        else:
            xs, *f2 = _ffn_cast(xs, norm_ffn2[l], ffn2_gate, ffn2_up, ffn2_down, l, tf=512)
        valid = lambda a: a.reshape(Bs, tp, -1)[:, :Ts]
        outs_s.append((valid(k_new).reshape(Bs, Ts, N_KV_HEADS, HEAD_DIM_C),
                       valid(v_new).reshape(Bs, Ts, N_KV_HEADS, HEAD_DIM_C),
                       valid(ki_new),
                       h_last.reshape(Bs, d_a), conv_new, s_last))

        xp = _ffn(xp, norm_ffn1[l], *f1, tm=1024, tf=256)
        ua, ub, c1, k_new, v_new, ki_new, tail, c2b = in_proj(xp, tm=256)
        ya, h_last, conv_new = _lru(ua.reshape(1, S, 2 * d_a), zeros_conv, zeros_h, conv_w[l], conv_b[l],
                                    lwa, lru_ba[l], lwx, lru_bx[l], lru_lambda[l], tc=1024, t_valid_last=1024)
        yb, s_last = _retention(ub.reshape(1, S, -1), cos_p, sin_p, zeros_s, cp=128, c_valid=128, mm_dtype=BF16,
                                chunks_per_step=4)
        yc = _dsa_prompt(c1, tail, c2b, rel_bias, tq=256, n_keep=min(TOPK_MAX, S // 4))
        xp = _out_proj(xp, ya.reshape(S, d_a), yb.reshape(S, d_b), yc, wo, tm=512)
        if last:
            xp, y_prompt = _ffn(xp, norm_ffn2[l], *f2, norm_final, tm=512, tf=512)
        else:
            xp = _ffn(xp, norm_ffn2[l], *f2, tm=1024, tf=256)
        outs_p.append((k_new.reshape(1, S, N_KV_HEADS, HEAD_DIM_C),
                       v_new.reshape(1, S, N_KV_HEADS, HEAD_DIM_C),
                       ki_new.reshape(1, S, IDX_DIM),
                       h_last.reshape(1, d_a), conv_new, s_last))

    stack = lambda outs, k: jnp.stack([o[k] for o in outs])
    return ((y_prompt.reshape(1, S, D), y_sample.reshape(Bs, tp, D)[:, :Ts])
            + tuple(stack(outs_p, k) for k in range(6))
            + tuple(stack(outs_s, k) for k in range(6)))
```

```python
import functools
import math

import numpy as np
import jax
import jax.numpy as jnp
from jax import lax
from jax.experimental import pallas as pl
from jax.experimental.pallas import tpu as pltpu

F32 = jnp.float32
BF16 = jnp.bfloat16
I32 = jnp.int32

EPS = 1e-6
N_A_BLOCKS = 4
CONV_WIDTH = 4
LRU_C = 8.0
N_RET_HEADS = 4
ROPE_BASE = 10000.0
HEAD_DIM_C = 128
N_KV_HEADS = 2
N_IDX_HEADS = 16
IDX_DIM = 64
TOPK_MAX = 256
NUM_BUCKETS = 32
MAX_DISTANCE = 128
SAMPLE_PAD_T = 8
PAGES_PER_STEP = 32
NEG_BIG = -1e30
INT_MIN = -2 ** 31
SELECT_BIT_GROUPS = (0, 21, 23, 25, 27, 29, 32)

VMEM_LIMIT = 56 * 1024 * 1024


def _cparams(sem):
    return pltpu.CompilerParams(dimension_semantics=sem, vmem_limit_bytes=VMEM_LIMIT)


def _rms(x, g):
    return x * lax.rsqrt(jnp.mean(x * x, axis=-1, keepdims=True) + EPS) * g


def _dot(a, b):
    return jnp.dot(a, b, preferred_element_type=F32)


def _dot_nt(a, b):
    return lax.dot_general(a, b, (((1,), (1,)), ((), ())), preferred_element_type=F32)


def _dot_tn(a, b):
    return lax.dot_general(a, b, (((0,), (0,)), ((), ())), preferred_element_type=F32)


def _sort_key(score):
    bits = lax.bitcast_convert_type(score, I32)
    return jnp.where(bits < 0, bits ^ jnp.int32(0x7FFFFFFF), bits)


def _ffn_body(x_ref, g_ref, wg_ref, wu_ref, wd_ref, *rest, final_norm):
    if final_norm:
        gf_ref, o_ref, h_ref = rest
    else:
        o_ref, h_ref = rest
    j = pl.program_id(1)

    @pl.when(j == 0)
    def _():
        h_ref[...] = _rms(x_ref[...], g_ref[...]).astype(BF16)
        o_ref[...] = jnp.zeros_like(o_ref)

    h = h_ref[...]
    g = _dot(h, wg_ref[...])
    u = _dot(h, wu_ref[...])
    a = (g * jax.nn.sigmoid(g) * u).astype(BF16)
    o_ref[...] += _dot(a, wd_ref[...])

    @pl.when(j == pl.num_programs(1) - 1)
    def _():
        y = x_ref[...] + 0.5 * o_ref[...]
        o_ref[...] = _rms(y, gf_ref[...]) if final_norm else y


def _ffn(x, g, wg, wu, wd, gf=None, *, tm, tf):
    R, D = x.shape
    FF = wg.shape[1]
    final_norm = gf is not None
    in_specs = [
        pl.BlockSpec((tm, D), lambda i, j: (i, 0)),
        pl.BlockSpec((1, D), lambda i, j: (0, 0)),
        pl.BlockSpec((D, tf), lambda i, j: (0, j)),
        pl.BlockSpec((D, tf), lambda i, j: (0, j)),
        pl.BlockSpec((tf, D), lambda i, j: (j, 0)),
    ]
    args = [x, g.reshape(1, D), wg, wu, wd]
    out_shape = [jax.ShapeDtypeStruct((R, D), F32)]
    out_specs = [pl.BlockSpec((tm, D), lambda i, j: (i, 0))]
    if final_norm:
        in_specs.append(pl.BlockSpec((1, D), lambda i, j: (0, 0)))
        args.append(gf.reshape(1, D))
    res = pl.pallas_call(
        functools.partial(_ffn_body, final_norm=final_norm),
        grid=(R // tm, FF // tf),
        in_specs=in_specs,
        out_specs=out_specs,
        out_shape=out_shape,
        scratch_shapes=[pltpu.VMEM((tm, D), BF16)],
        compiler_params=_cparams(("parallel", "arbitrary")),
        name="ffn",
    )(*args)
    return res[0]


def _ffn_cast_body(x_ref, g_ref, wg_ref, wu_ref, wd_ref, *rest, final_norm):
    if final_norm:
        gf_ref, o_ref, on_ref, wgb_ref, wub_ref, wdb_ref, h_ref = rest
    else:
        o_ref, wgb_ref, wub_ref, wdb_ref, h_ref = rest
    j = pl.program_id(0)

    @pl.when(j == 0)
    def _():
        h_ref[...] = _rms(x_ref[...], g_ref[...]).astype(BF16)
        o_ref[...] = jnp.zeros_like(o_ref)

    wg, wu, wd = wg_ref[0].astype(BF16), wu_ref[0].astype(BF16), wd_ref[0].astype(BF16)
    wgb_ref[...] = wg
    wub_ref[...] = wu
    wdb_ref[...] = wd
    h = h_ref[...]
    g = _dot(h, wg)
    a = (g * jax.nn.sigmoid(g) * _dot(h, wu)).astype(BF16)
    o_ref[...] += _dot(a, wd)

    @pl.when(j == pl.num_programs(0) - 1)
    def _():
        y = x_ref[...] + 0.5 * o_ref[...]
        o_ref[...] = y
        if final_norm:
            on_ref[...] = _rms(y, gf_ref[...])


def _ffn_cast(x, g, wg, wu, wd, layer, gf=None, *, tf):
    R, D = x.shape
    FF = wg.shape[2]
    final_norm = gf is not None
    in_specs = [
        pl.BlockSpec((R, D), lambda j: (0, 0)),
        pl.BlockSpec((1, D), lambda j: (0, 0)),
        pl.BlockSpec((1, D, tf), lambda j: (layer, 0, j)),
        pl.BlockSpec((1, D, tf), lambda j: (layer, 0, j)),
        pl.BlockSpec((1, tf, D), lambda j: (layer, j, 0)),
    ]
    args = [x, g.reshape(1, D), wg, wu, wd]
    out_shape = [jax.ShapeDtypeStruct((R, D), F32)]
    out_specs = [pl.BlockSpec((R, D), lambda j: (0, 0))]
    if final_norm:
        in_specs.append(pl.BlockSpec((1, D), lambda j: (0, 0)))
        args.append(gf.reshape(1, D))
        out_shape.append(jax.ShapeDtypeStruct((R, D), F32))
        out_specs.append(pl.BlockSpec((R, D), lambda j: (0, 0)))
    out_shape += [jax.ShapeDtypeStruct((D, FF), BF16), jax.ShapeDtypeStruct((D, FF), BF16),
                  jax.ShapeDtypeStruct((FF, D), BF16)]
    out_specs += [pl.BlockSpec((D, tf), lambda j: (0, j)), pl.BlockSpec((D, tf), lambda j: (0, j)),
                  pl.BlockSpec((tf, D), lambda j: (j, 0))]
    return pl.pallas_call(
        functools.partial(_ffn_cast_body, final_norm=final_norm),
        grid=(FF // tf,),
        in_specs=in_specs,
        out_specs=out_specs,
        out_shape=out_shape,
        scratch_shapes=[pltpu.VMEM((R, D), BF16)],
        compiler_params=_cparams(("arbitrary",)),
        name="ffn_cast",
    )(*args)


def _in_proj_body(x_ref, g_ref, w_ref, wt_ref, ua_ref, ub_ref, c1_ref, k_ref, v_ref, ki_ref, tail_ref, c2b_ref,
                  *, q_scale, cols):
    h = _rms(x_ref[...], g_ref[...]).astype(BF16)
    o_a, o_b, o_qc, o_kv, o_qi, o_end = cols
    kv_w = k_ref.shape[1]
    d_c = o_kv - o_qc
    ua_ref[...] = _dot(h, w_ref[0, :, o_a:o_b])
    ub_ref[...] = _dot(h, w_ref[0, :, o_b:o_qc])
    c1_ref[:, :d_c] = (_dot(h, w_ref[0, :, o_qc:o_kv]) * q_scale).astype(BF16)
    c1_ref[:, d_c:] = _dot(h, w_ref[0, :, o_qi:o_end]).astype(BF16)
    kv = _dot(h, w_ref[0, :, o_kv:o_qi])
    t = _dot(h, wt_ref[0])
    k_ref[...] = kv[:, :kv_w]
    v_ref[...] = kv[:, kv_w:]
    ki_ref[...] = t[:, :IDX_DIM]
    tail_ref[...] = t
    c2b_ref[:, :2 * kv_w] = kv.astype(BF16)
    c2b_ref[:, 2 * kv_w:] = t.astype(BF16)


def _in_proj(x, g, w_all, w_tail, layer, *, tm, cols, q_scale):
    R, D = x.shape
    o_a, o_b, o_qc, o_kv, o_qi, o_end = cols
    kv_w = N_KV_HEADS * HEAD_DIM_C
    tw = w_tail.shape[2]
    outs = [(o_b - o_a, F32), (o_qc - o_b, F32), (o_kv - o_qc + o_end - o_qi, BF16), (kv_w, F32), (kv_w, F32),
            (IDX_DIM, F32), (tw, F32), (2 * kv_w + tw, BF16)]
    return pl.pallas_call(
        functools.partial(_in_proj_body, q_scale=q_scale, cols=cols),
        grid=(R // tm,),
        in_specs=[pl.BlockSpec((tm, D), lambda i: (i, 0)), pl.BlockSpec((1, D), lambda i: (0, 0)),
                  pl.BlockSpec((1,) + w_all.shape[1:], lambda i: (layer, 0, 0), pipeline_mode=pl.Buffered(1)),
                  pl.BlockSpec((1,) + w_tail.shape[1:], lambda i: (layer, 0, 0), pipeline_mode=pl.Buffered(1))],
        out_specs=[pl.BlockSpec((tm, wd), lambda i: (i, 0)) for wd, _ in outs],
        out_shape=[jax.ShapeDtypeStruct((R, wd), dt) for wd, dt in outs],
        compiler_params=_cparams(("parallel",)),
        name="in_proj",
    )(x, g.reshape(1, D), w_all, w_tail)


def _oproj_body(x_ref, ya_ref, yb_ref, yc_ref, w_ref, o_ref):
    da, db = ya_ref.shape[1], yb_ref.shape[1]
    o_ref[...] = (x_ref[...] + _dot(ya_ref[...], w_ref[:da]) + _dot(yb_ref[...], w_ref[da:da + db])
                  + _dot(yc_ref[...], w_ref[da + db:]))


def _out_proj(x, ya, yb, yc, w, *, tm):
    R, D = x.shape
    row = lambda a: pl.BlockSpec((tm, a.shape[1]), lambda i: (i, 0))
    return pl.pallas_call(
        _oproj_body,
        grid=(R // tm,),
        in_specs=[row(x), row(ya), row(yb), row(yc),
                  pl.BlockSpec(w.shape, lambda i: (0, 0), pipeline_mode=pl.Buffered(1))],
        out_specs=row(x),
        out_shape=jax.ShapeDtypeStruct((R, D), F32),
        compiler_params=_cparams(("parallel",)),
        name="out_proj",
    )(x, ya, yb, yc, w)


def _lru_body(xa_ref, ga_ref, conv0_ref, h0_ref, cw_ref, cb_ref, wa_ref, ba_ref, wx_ref, bx_ref, lam_ref,
              ya_ref, hl_ref, cn_ref, xs_ref, a_ref, u_ref, hs_ref, hc_ref, *, tc, t_valid_last):
    t = pl.program_id(1)
    da = xa_ref.shape[-1]
    blk = da // N_A_BLOCKS
    tail = 8

    @pl.when(t == 0)
    def _():
        xs_ref[0:tail, :] = conv0_ref[0]
        hc_ref[0:1, :] = h0_ref[0]

    xs_ref[tail:tail + tc, :] = xa_ref[0]
    xc = cb_ref[...]
    for j in range(CONV_WIDTH):
        off = tail - (CONV_WIDTH - 1) + j
        xc = xc + xs_ref[off:off + tc, :] * cw_ref[j:j + 1, :]
    xcb = xc.astype(BF16)
    pre_r = jnp.concatenate([_dot(xcb[:, n * blk:(n + 1) * blk], wa_ref[n]) for n in range(N_A_BLOCKS)], axis=1)
    pre_i = jnp.concatenate([_dot(xcb[:, n * blk:(n + 1) * blk], wx_ref[n]) for n in range(N_A_BLOCKS)], axis=1)
    r = jax.nn.sigmoid(pre_r + ba_ref[...])
    gi = jax.nn.sigmoid(pre_i + bx_ref[...])
    z = -lam_ref[...]
    softplus = jnp.maximum(z, 0.0) + jnp.log1p(jnp.exp(-jnp.abs(z)))
    log_a = (-LRU_C) * r * softplus
    a = jnp.exp(log_a)
    u = jnp.sqrt(-jnp.tanh(log_a) * (a * a + 1.0)) * gi * xc

    row_in_group = lax.broadcasted_iota(I32, (tc, da), 0) % 8
    for s in (1, 2, 4):
        first = row_in_group < s
        a_lo = jnp.where(first, 1.0, pltpu.roll(a, s, 0))
        u_lo = jnp.where(first, 0.0, pltpu.roll(u, s, 0))
        u = a * u_lo + u
        a = a * a_lo
    a_ref[...] = a
    u_ref[...] = u

    def step(k, h):
        base = pl.multiple_of(k * 8, 8)
        hg = a_ref[pl.ds(base, 8), :] * h + u_ref[pl.ds(base, 8), :]
        hs_ref[pl.ds(base, 8), :] = hg
        return hg[7:8, :]

    h = lax.fori_loop(0, tc // 8, step, hc_ref[0:1, :])
    hc_ref[0:1, :] = h
    ya_ref[0] = (hs_ref[...] * jax.nn.gelu(ga_ref[0])).astype(ya_ref.dtype)

    @pl.when(t == pl.num_programs(1) - 1)
    def _():
        hl_ref[0] = hs_ref[t_valid_last - 1:t_valid_last, :]
        lo = tail - (CONV_WIDTH - 1) + t_valid_last
        cn_ref[0] = xs_ref[lo:lo + CONV_WIDTH - 1, :]

    xs_ref[0:tail, :] = xs_ref[tc:tc + tail, :]


def _lru(ua, conv0, h0, cw, cb, wa, ba, wx, bx, lam, *, tc, t_valid_last):
    B, T, da2 = ua.shape
    da = da2 // 2
    vec = lambda: pl.BlockSpec((1, da), lambda b, t: (0, 0))
    blk = da // N_A_BLOCKS
    return pl.pallas_call(
        functools.partial(_lru_body, tc=tc, t_valid_last=t_valid_last),
        grid=(B, T // tc),
        in_specs=[
            pl.BlockSpec((1, tc, da), lambda b, t: (b, t, 0)),
            pl.BlockSpec((1, tc, da), lambda b, t: (b, t, 1)),
            pl.BlockSpec((1, 8, da), lambda b, t: (b, 0, 0)),
            pl.BlockSpec((1, 1, da), lambda b, t: (b, 0, 0)),
            pl.BlockSpec((CONV_WIDTH, da), lambda b, t: (0, 0)),
            vec(),
            pl.BlockSpec((N_A_BLOCKS, blk, blk), lambda b, t: (0, 0, 0)),
            vec(),
            pl.BlockSpec((N_A_BLOCKS, blk, blk), lambda b, t: (0, 0, 0)),
            vec(),
            vec(),
        ],
        out_specs=[
            pl.BlockSpec((1, tc, da), lambda b, t: (b, t, 0)),
            pl.BlockSpec((1, 1, da), lambda b, t: (b, 0, 0)),
            pl.BlockSpec((1, CONV_WIDTH - 1, da), lambda b, t: (b, 0, 0)),
        ],
        out_shape=[
            jax.ShapeDtypeStruct((B, T, da), BF16),
            jax.ShapeDtypeStruct((B, 1, da), F32),
            jax.ShapeDtypeStruct((B, CONV_WIDTH - 1, da), F32),
        ],
        scratch_shapes=[
            pltpu.VMEM((tc + 8, da), F32),
            pltpu.VMEM((tc, da), F32),
            pltpu.VMEM((tc, da), F32),
            pltpu.VMEM((tc, da), F32),
            pltpu.VMEM((8, da), F32),
        ],
        compiler_params=_cparams(("arbitrary", "arbitrary")),
        name="rglru",
    )(ua, ua, conv0, h0, cw, cb.reshape(1, da), wa, ba.reshape(1, da), wx, bx.reshape(1, da), lam.reshape(1, da))


def _ret_body(q_ref, k_ref, v_ref, gb_ref, cos_ref, sin_ref, s0_ref, yb_ref, sl_ref, s_ref, *, cp, c_valid, mm_dtype):
    t = pl.program_id(1)
    dk = q_ref.shape[-1] // N_RET_HEADS
    dv = v_ref.shape[-1] // N_RET_HEADS

    @pl.when(t == 0)
    def _():
        s_ref[...] = s0_ref[0]

    cos = cos_ref[...]
    sin = sin_ref[...]
    first_half = (lax.broadcasted_iota(I32, cos.shape, 1) % dk) < (dk // 2)
    width = cos.shape[1]

    def rot(x):
        partner = jnp.where(first_half, pltpu.roll(x, width - dk // 2, 1), pltpu.roll(x, dk // 2, 1))
        return x * cos + partner * sin

    row = lax.broadcasted_iota(I32, (cp, 1), 0)
    q_all = rot(q_ref[0])
    k_all = rot(k_ref[0]) * (dk ** -0.5)
    ri = lax.broadcasted_iota(I32, (cp, cp), 0)
    ci = lax.broadcasted_iota(I32, (cp, cp), 1)
    diff = (ri - ci).astype(F32)
    rowf = row.astype(F32)
    for h in range(N_RET_HEADS):
        lg = math.log1p(-(2.0 ** (-5.0 - h)))
        dmask = jnp.where(diff >= 0, jnp.exp(jnp.maximum(diff, 0.0) * lg), 0.0)
        cross_dec = jnp.exp((rowf + 1.0) * lg)
        state_dec = jnp.exp((c_valid - 1.0 - rowf) * lg)
        chunk_dec = math.exp(c_valid * lg)
        state = s_ref[h]
        for n in range(q_all.shape[0] // cp):
            rows = slice(n * cp, (n + 1) * cp)
            qh = q_all[rows, h * dk:(h + 1) * dk].astype(mm_dtype)
            kh = k_all[rows, h * dk:(h + 1) * dk]
            if c_valid < cp:
                kh = jnp.where(row < c_valid, kh, 0.0)
            vh = v_ref[0, rows, h * dv:(h + 1) * dv].astype(mm_dtype)
            att = _dot_nt(qh, kh.astype(mm_dtype)) * dmask
            inner = _dot(att.astype(mm_dtype), vh)
            cross = _dot(qh, state.astype(mm_dtype)) * cross_dec
            state = state * chunk_dec + _dot_tn((kh * state_dec).astype(mm_dtype), vh)
            o = inner + cross
            o = o * lax.rsqrt(jnp.mean(o * o, axis=-1, keepdims=True) + EPS)
            gh = gb_ref[0, rows, h * dv:(h + 1) * dv]
            yb_ref[0, rows, h * dv:(h + 1) * dv] = (o * (gh * jax.nn.sigmoid(gh))).astype(yb_ref.dtype)
        s_ref[h] = state

    @pl.when(t == pl.num_programs(1) - 1)
    def _():
        sl_ref[0] = s_ref[...]


def _retention(ub, cos, sin, s0, *, cp, c_valid, mm_dtype, chunks_per_step=1):
    B, T, wtot = ub.shape
    w = wtot // 6
    H, dk, dv = s0.shape[1:]
    blk = cp * chunks_per_step
    return pl.pallas_call(
        functools.partial(_ret_body, cp=cp, c_valid=c_valid, mm_dtype=mm_dtype),
        grid=(B, T // blk),
        in_specs=[
            pl.BlockSpec((1, blk, w), lambda b, t: (b, t, 0)),
            pl.BlockSpec((1, blk, w), lambda b, t: (b, t, 1)),
            pl.BlockSpec((1, blk, 2 * w), lambda b, t: (b, t, 1)),
            pl.BlockSpec((1, blk, 2 * w), lambda b, t: (b, t, 2)),
            pl.BlockSpec((blk, w), lambda b, t: (t, 0)),
            pl.BlockSpec((blk, w), lambda b, t: (t, 0)),
            pl.BlockSpec((1, H, dk, dv), lambda b, t: (b, 0, 0, 0)),
        ],
        out_specs=[
            pl.BlockSpec((1, blk, 2 * w), lambda b, t: (b, t, 0)),
            pl.BlockSpec((1, H, dk, dv), lambda b, t: (b, 0, 0, 0)),
        ],
        out_shape=[
            jax.ShapeDtypeStruct((B, T, 2 * w), BF16),
            jax.ShapeDtypeStruct((B, H, dk, dv), F32),
        ],
        scratch_shapes=[pltpu.VMEM((H, dk, dv), F32)],
        compiler_params=_cparams(("arbitrary", "arbitrary")),
        name="retention",
    )(ub, ub, ub, ub, cos, sin, s0)


def _rope_tables(pos, dk, heads):
    half = dk // 2
    freqs = ROPE_BASE ** (-jnp.arange(half, dtype=F32) / half)
    ang = pos.astype(F32)[:, None] * freqs[None, :]
    cos, sin = jnp.cos(ang), jnp.sin(ang)
    cos_t = jnp.tile(jnp.concatenate([cos, cos], axis=1), (1, heads))
    sin_t = jnp.tile(jnp.concatenate([-sin, sin], axis=1), (1, heads))
    return cos_t, sin_t


def _t5_bucket_np(dist):
    n = np.maximum(dist, 0)
    max_exact = NUM_BUCKETS // 2
    ratio = np.log(np.maximum(n, 1).astype(np.float32) / np.float32(max_exact)) / np.float32(math.log(MAX_DISTANCE / max_exact))
    large = max_exact + (ratio * np.float32(NUM_BUCKETS - max_exact)).astype(np.int32)
    large = np.minimum(large, NUM_BUCKETS - 1)
    return np.where(n < max_exact, n, large).astype(np.int32)


def _bias_from_buckets(bucket, rb_ref, head):
    def step(b, out):
        return jnp.where(bucket == b, rb_ref[b, head], out)

    return lax.fori_loop(0, NUM_BUCKETS, step, jnp.zeros(bucket.shape, F32))


def _dsa_prompt_body(rb_ref, qc_ref, qi_ref, wi_ref, kvb_ref, btab_ref, o_ref,
                     keys_ref, bt_ref, wf_ref, qis_ref, qs_ref, mx_ref, mrep_ref, acc_ref, cut_ref, kmax_ref, sel_ref,
                     *, tq, n_keep, idx_bits):
    i = pl.program_id(0)
    G = qc_ref.shape[1] // (N_KV_HEADS * HEAD_DIM_C)
    n_heads = N_KV_HEADS * G
    kcol, vcol, icol = 0, N_KV_HEADS * HEAD_DIM_C, 2 * N_KV_HEADS * HEAD_DIM_C
    lanes = HEAD_DIM_C
    halves = tq // lanes
    n_tiles = i + 1

    @pl.when(i == 0)
    def _():
        for h in range(n_heads):
            far = rb_ref[NUM_BUCKETS - 1, h]
            for r in range(2):
                bt_ref[h // G, r, h % G] = _bias_from_buckets(btab_ref[r], rb_ref, h) - far

    for h in range(N_IDX_HEADS):
        qis_ref[h] = qi_ref[:, h * IDX_DIM:(h + 1) * IDX_DIM]
        wf_ref[h] = jnp.broadcast_to(wi_ref[:, IDX_DIM + h:IDX_DIM + h + 1], (tq, lanes))
    for c in range(N_KV_HEADS):
        for g in range(G):
            h = c * G + g
            qs_ref[c, g * tq:(g + 1) * tq, :] = qc_ref[:, h * HEAD_DIM_C:(h + 1) * HEAD_DIM_C]

    rowi = lax.broadcasted_iota(I32, (tq, tq), 0)
    coli = lax.broadcasted_iota(I32, (tq, tq), 1)

    def key_index(j):
        return j * tq + coli

    def visible(j):
        return key_index(j) <= (i * tq + rowi)

    def score_tile(j, carry):
        ki = kvb_ref[pl.ds(pl.multiple_of(j * tq, tq), tq), icol:icol + IDX_DIM]
        for rh in range(halves):
            rows = slice(rh * lanes, (rh + 1) * lanes)
            acc = jnp.zeros((lanes, tq), F32)
            for h in range(N_IDX_HEADS):
                w = wf_ref[h, rows, :]
                acc = acc + jnp.maximum(_dot_nt(qis_ref[h, rows, :], ki), 0.0) * jnp.concatenate([w] * halves, axis=1)
            score = acc * (N_IDX_HEADS ** -0.5 * IDX_DIM ** -0.5)
            score = jnp.where(visible(j)[rows], score, -jnp.inf)
            keys_ref[j, :, rows] = _sort_key(score).T
        return carry

    lax.fori_loop(0, n_tiles // 2, lambda p, carry: score_tile(2 * p + 1, score_tile(2 * p, carry)), 0)
    lax.fori_loop(2 * (n_tiles // 2), n_tiles, score_tile, 0)

    sub = 8
    grp = lax.broadcasted_iota(I32, (tq // sub, sub, tq), 0)
    srow = lax.broadcasted_iota(I32, (tq // sub, sub, tq), 1)

    def count(pred, *row_args):
        args = [a[None] for a in row_args]

        def one(j, cnt):
            k = keys_ref[j].reshape(tq // sub, sub, tq)
            idx = j * tq + grp * sub + srow
            return cnt + jnp.sum(jnp.where(pred(k, idx, *args), 1, 0), axis=0)

        n_pairs = n_tiles // 2
        cnt = lax.fori_loop(0, n_pairs, lambda p, c: one(2 * p + 1, one(2 * p, c)), jnp.zeros((sub, tq), I32))
        cnt = lax.fori_loop(2 * n_pairs, n_tiles, one, cnt)
        return jnp.broadcast_to(jnp.sum(cnt, axis=0, keepdims=True), (sub, tq))

    def bit_step(b, st):
        tau, n_ge = st
        cand = tau + lax.shift_left(jnp.int32(1), 31 - b)
        c = count(lambda k, idx, cnd: k >= cnd, cand)
        ge = c >= n_keep
        return jnp.where(ge, cand, tau), jnp.where(ge, c, n_ge)

    sel_ref[0] = jnp.full((sub, tq), INT_MIN, I32)
    sel_ref[1] = jnp.broadcast_to(n_tiles * tq, (sub, tq)).astype(I32)
    for b0, b1 in zip(SELECT_BIT_GROUPS[:-1], SELECT_BIT_GROUPS[1:]):
        @pl.when(jnp.max(sel_ref[1]) > n_keep)
        def _():
            tau_g, n_ge_g = lax.fori_loop(b0, b1, bit_step, (sel_ref[0], sel_ref[1]))
            sel_ref[0] = tau_g
            sel_ref[1] = n_ge_g

    tau, n_ge = sel_ref[0], sel_ref[1]

    cut_ref[...] = jnp.full((sub, tq), 2 ** 31 - 1, I32)

    @pl.when(jnp.max(n_ge) > n_keep)
    def _():
        need = n_keep - count(lambda k, idx, t: k > t, tau)

        def idx_step(b, cut):
            cand = cut + lax.shift_left(jnp.int32(1), idx_bits - 1 - b)
            below = count(lambda k, idx, t, c: (k == t) & (idx < c), tau, cand)
            return jnp.where(below < need, cand, cut)

        cut_ref[...] = lax.fori_loop(0, idx_bits, idx_step, jnp.zeros((sub, tq), I32))

    tau_t = jnp.broadcast_to(tau[0:1], (tq, tq)).T
    cut_t = jnp.broadcast_to(cut_ref[0:1, :], (tq, tq)).T

    def mask_bias(j, r):
        k = keys_ref[j].T
        sel = (k > tau_t) | ((k == tau_t) & (key_index(j) <= cut_t))
        if r is not None:
            sel = sel & visible(j)
        return jnp.where(sel, 0.0, NEG_BIG)

    def logits(j, c, r, mb):
        base = pl.multiple_of(j * tq, tq)
        kc = kvb_ref[pl.ds(base, tq), kcol + c * HEAD_DIM_C:kcol + (c + 1) * HEAD_DIM_C]
        s = _dot_nt(qs_ref[c], kc).reshape(G, tq, tq) + mb[None]
        if r is not None:
            s = s + bt_ref[c, r]
        return s

    def max_tile(j, r):
        mb = mask_bias(j, r)
        for c in range(N_KV_HEADS):
            s = logits(j, c, r, mb)
            m = s[..., 0:lanes]
            for a in range(1, halves):
                m = jnp.maximum(m, s[..., a * lanes:(a + 1) * lanes])
            mx_ref[c] = jnp.maximum(mx_ref[c], m)

    def acc_tile(j, r):
        base = pl.multiple_of(j * tq, tq)
        mb = mask_bias(j, r)
        for c in range(N_KV_HEADS):
            vc = kvb_ref[pl.ds(base, tq), vcol + c * HEAD_DIM_C:vcol + (c + 1) * HEAD_DIM_C]
            vext = jnp.concatenate([vc, jnp.ones((tq, lanes), BF16)], axis=1)
            m = mrep_ref[c]
            p = jnp.exp(logits(j, c, r, mb) - jnp.concatenate([m] * halves, axis=-1))
            acc_ref[c] += _dot(p.reshape(G * tq, tq).astype(BF16), vext)

    def sweep(tile_fn):
        def far(j, carry):
            tile_fn(j, None)
            return carry

        n_far = jnp.maximum(i - 1, 0)
        lax.fori_loop(0, n_far // 2, lambda p, carry: far(2 * p + 1, far(2 * p, carry)), 0)
        lax.fori_loop(2 * (n_far // 2), n_far, far, 0)

        @pl.when(i >= 1)
        def _():
            tile_fn(i - 1, 1)
            tile_fn(i, 0)

        @pl.when(i == 0)
        def _():
            tile_fn(0, 0)

    @pl.when(i == 0)
    def _():
        kmax_ref[...] = jnp.zeros(kmax_ref.shape, F32)

    for c in range(N_KV_HEADS):
        kt = kvb_ref[pl.ds(pl.multiple_of(i * tq, tq), tq), kcol + c * HEAD_DIM_C:kcol + (c + 1) * HEAD_DIM_C].astype(F32)
        knorm = jnp.sqrt(jnp.max(jnp.sum(kt * kt, axis=1, keepdims=True)))
        kmax_ref[c] = jnp.maximum(kmax_ref[c], knorm)
        qf = qs_ref[c].astype(F32)
        qnorm = jnp.sqrt(jnp.sum(qf * qf, axis=1, keepdims=True)).reshape(G, tq, 1)
        for g in range(G):
            h = c * G + g
            far = rb_ref[NUM_BUCKETS - 1, h]
            bias_max = lax.fori_loop(0, NUM_BUCKETS, lambda b, m: jnp.maximum(m, rb_ref[b, h] - far), jnp.float32(0.0))
            mrep_ref[c, g] = qnorm[g] * kmax_ref[c, 0:1, :] + bias_max

    acc_ref[...] = jnp.zeros(acc_ref.shape, F32)
    sweep(acc_tile)

    @pl.when(jnp.logical_not(jnp.min(acc_ref[:, :, lanes:]) > 1e-30))
    def _():
        mx_ref[...] = jnp.full(mx_ref.shape, NEG_BIG, F32)
        sweep(max_tile)
        mrep_ref[...] = jnp.broadcast_to(jnp.max(mx_ref[...], axis=-1, keepdims=True), mrep_ref.shape)
        acc_ref[...] = jnp.zeros(acc_ref.shape, F32)
        sweep(acc_tile)

    for c in range(N_KV_HEADS):
        for g in range(G):
            h = c * G + g
            a = acc_ref[c, g * tq:(g + 1) * tq, :]
            o_ref[:, h * HEAD_DIM_C:(h + 1) * HEAD_DIM_C] = (a[:, :lanes] / a[:, lanes:]).astype(o_ref.dtype)


def _dsa_prompt(qcqi, tail, kvi_bf16, rel_bias, *, tq, n_keep):
    S = qcqi.shape[0]
    dc = qcqi.shape[1] - N_IDX_HEADS * IDX_DIM
    n_heads = dc // HEAD_DIM_C
    G = n_heads // N_KV_HEADS
    wkv = kvi_bf16.shape[1]
    d = np.arange(tq)[:, None] - np.arange(tq)[None, :]
    btab = jnp.asarray(np.stack([_t5_bucket_np(d + r * tq) for r in range(2)]))
    assert _t5_bucket_np(np.array([tq + 1]))[0] == NUM_BUCKETS - 1
    lanes = HEAD_DIM_C
    grid_spec = pltpu.PrefetchScalarGridSpec(
        num_scalar_prefetch=0,
        grid=(S // tq,),
        in_specs=[
            pl.BlockSpec(memory_space=pltpu.SMEM),
            pl.BlockSpec((tq, dc), lambda i: (i, 0)),
            pl.BlockSpec((tq, N_IDX_HEADS * IDX_DIM), lambda i: (i, dc // (N_IDX_HEADS * IDX_DIM))),
            pl.BlockSpec((tq, tail.shape[1]), lambda i: (i, 0)),
            pl.BlockSpec((S, wkv), lambda i: (0, 0), pipeline_mode=pl.Buffered(1)),
            pl.BlockSpec((2, tq, tq), lambda i: (0, 0, 0)),
        ],
        out_specs=pl.BlockSpec((tq, dc), lambda i: (i, 0)),
        scratch_shapes=[
            pltpu.VMEM((S // tq, tq, tq), I32),
            pltpu.VMEM((N_KV_HEADS, 2, G, tq, tq), F32),
            pltpu.VMEM((N_IDX_HEADS, tq, lanes), F32),
            pltpu.VMEM((N_IDX_HEADS, tq, IDX_DIM), BF16),
            pltpu.VMEM((N_KV_HEADS, G * tq, HEAD_DIM_C), BF16),
            pltpu.VMEM((N_KV_HEADS, G, tq, lanes), F32),
            pltpu.VMEM((N_KV_HEADS, G, tq, lanes), F32),
            pltpu.VMEM((N_KV_HEADS, G * tq, 2 * lanes), F32),
            pltpu.VMEM((8, tq), I32),
            pltpu.VMEM((N_KV_HEADS, 8, lanes), F32),
            pltpu.VMEM((2, 8, tq), I32),
        ],
    )
    return pl.pallas_call(
        functools.partial(_dsa_prompt_body, tq=tq, n_keep=n_keep, idx_bits=(S - 1).bit_length()),
        grid_spec=grid_spec,
        out_shape=jax.ShapeDtypeStruct((S, dc), BF16),
        compiler_params=_cparams(("arbitrary",)),
        name="dsa_prompt",
    )(rel_bias, qcqi, qcqi, tail, kvi_bf16, btab)


def _dsa_sel_body(pt_ref, qi_ref, w_ref, kin_ref, *rest, t_new, n_keep):
    pages = rest[:PAGES_PER_STEP]
    mb_ref, keys_ref, arg_ref, cnt_ref = rest[PAGES_PER_STEP:]
    b = pl.program_id(0)
    p = pl.program_id(1)
    B, n_chunks, tp, P = keys_ref.shape
    last = p == pl.num_programs(1) - 1
    q = qi_ref[0]
    w = w_ref[0]
    rowi = lax.broadcasted_iota(I32, (tp, P), 0)
    coli = lax.broadcasted_iota(I32, (tp, P), 1)
    vis_new = (coli <= rowi) & (coli < t_new)

    def chunk_scores(s):
        s = jnp.maximum(s, 0.0) * w
        s = s.reshape(tp, N_IDX_HEADS, s.shape[-1]).sum(axis=1)
        return s * (N_IDX_HEADS ** -0.5 * IDX_DIM ** -0.5)

    for r in range(PAGES_PER_STEP):
        keys_ref[b, p * PAGES_PER_STEP + r] = _sort_key(chunk_scores(_dot(q, pages[r][0].astype(BF16))))

    @pl.when(last)
    def _():
        s_new = jnp.where(vis_new, chunk_scores(_dot_nt(q, kin_ref[0])), -jnp.inf)
        keys_ref[b, n_chunks - 1] = _sort_key(s_new)

    @pl.when(last & (b == B - 1))
    def _():
        key_index = (lax.broadcasted_iota(I32, (n_chunks, tp, P), 0) * P
                     + lax.broadcasted_iota(I32, (n_chunks, tp, P), 2))

        def count(pred, *row_args):
            for n, a in enumerate(row_args):
                arg_ref[n] = a

            def per_seq(s, carry):
                args = [arg_ref[n, s][None] for n in range(len(row_args))]
                cnt_ref[s] = jnp.sum(jnp.where(pred(keys_ref[s], key_index, *args), 1, 0), axis=0)
                return carry

            lax.fori_loop(0, B, per_seq, 0)
            return jnp.broadcast_to(jnp.sum(cnt_ref[...], axis=-1, keepdims=True), (B, tp, P))

        def bit_step(bit, tau):
            cand = tau + lax.shift_left(jnp.int32(1), 31 - bit)
            return jnp.where(count(lambda k, idx, c: k >= c, cand) >= n_keep, cand, tau)

        tau = lax.fori_loop(0, 32, bit_step, jnp.full((B, tp, P), INT_MIN, I32))

        arg_ref[2] = jnp.full((B, tp, P), 2 ** 31 - 1, I32)

        @pl.when(jnp.max(count(lambda k, idx, t: k >= t, tau)) > n_keep)
        def _():
            need = n_keep - count(lambda k, idx, t: k > t, tau)
            idx_bits = (n_chunks * P - 1).bit_length()

            def idx_step(bit, cut):
                cand = cut + lax.shift_left(jnp.int32(1), idx_bits - 1 - bit)
                below = count(lambda k, idx, t, c: (k == t) & (idx < c), tau, cand)
                return jnp.where(below < need, cand, cut)

            arg_ref[2] = lax.fori_loop(0, idx_bits, idx_step, jnp.zeros((B, tp, P), I32))

        arg_ref[0] = tau

        def write_mask(s, carry):
            k = keys_ref[s]
            t = arg_ref[0, s][None]
            sel = (k > t) | ((k == t) & (key_index <= arg_ref[2, s][None]))
            mb_ref[s] = jnp.where(sel, 0.0, NEG_BIG)
            mb_ref[s, n_chunks - 1] = jnp.where(sel[n_chunks - 1] & vis_new, 0.0, NEG_BIG)
            return carry

        lax.fori_loop(0, B, write_mask, 0)


def _dsa_sample_select(pt_flat, qi, w, ki_new, pool_ki, *, n_pages, t_new, n_keep):
    B = qi.shape[0]
    P = pool_ki.shape[2]
    steps = n_pages // PAGES_PER_STEP

    def page_spec(r):
        return pl.BlockSpec((1, IDX_DIM, P), lambda b, p, pt: (pt[b * n_pages + p * PAGES_PER_STEP + r], 0, 0))

    grid_spec = pltpu.PrefetchScalarGridSpec(
        num_scalar_prefetch=1,
        grid=(B, steps),
        in_specs=[
            pl.BlockSpec((1,) + qi.shape[1:], lambda b, p, pt: (b, 0, 0)),
            pl.BlockSpec((1,) + w.shape[1:], lambda b, p, pt: (b, 0, 0)),
            pl.BlockSpec((1,) + ki_new.shape[1:], lambda b, p, pt: (b, 0, 0)),
        ] + [page_spec(r) for r in range(PAGES_PER_STEP)],
        out_specs=pl.BlockSpec((B, n_pages + 1, SAMPLE_PAD_T, P), lambda b, p, pt: (0, 0, 0, 0)),
        scratch_shapes=[
            pltpu.VMEM((B, n_pages + 1, SAMPLE_PAD_T, P), I32),
            pltpu.VMEM((3, B, SAMPLE_PAD_T, P), I32),
            pltpu.VMEM((B, SAMPLE_PAD_T, P), I32),
        ],
    )
    return pl.pallas_call(
        functools.partial(_dsa_sel_body, t_new=t_new, n_keep=n_keep),
        grid_spec=grid_spec,
        out_shape=jax.ShapeDtypeStruct((B, n_pages + 1, SAMPLE_PAD_T, P), F32),
        compiler_params=_cparams(("arbitrary", "arbitrary")),
        name="dsa_sample_select",
    )(pt_flat, qi, w, ki_new, *([pool_ki] * PAGES_PER_STEP))


def _dsa_att_body(pt_ref, rb_ref, q_ref, mb_ref, kn_ref, vn_ref, btab_ref, *rest):
    kp = rest[:PAGES_PER_STEP]
    vp = rest[PAGES_PER_STEP:2 * PAGES_PER_STEP]
    o_ref, bt_ref, m_ref, acc_ref = rest[2 * PAGES_PER_STEP:]
    b = pl.program_id(0)
    p = pl.program_id(1)
    n_chunks = mb_ref.shape[1]
    tp = SAMPLE_PAD_T
    G = q_ref.shape[2] // tp
    rows = G * tp
    P = kn_ref.shape[1]
    lanes = HEAD_DIM_C
    last = p == pl.num_programs(1) - 1

    @pl.when((b == 0) & (p == 0))
    def _():
        for c in range(N_KV_HEADS):
            for kind in range(2):
                for g in range(G):
                    h = c * G + g
                    bt_ref[c, kind, g * tp:(g + 1) * tp, :] = (_bias_from_buckets(btab_ref[kind], rb_ref, h)
                                                              - rb_ref[NUM_BUCKETS - 1, h])

    @pl.when(p == 0)
    def _():
        m_ref[...] = jnp.full(m_ref.shape, NEG_BIG, F32)
        acc_ref[...] = jnp.zeros(acc_ref.shape, F32)

    def attend(c, s, vext):
        m_old = m_ref[c]
        m_new = jnp.maximum(m_old, jnp.max(s, axis=1, keepdims=True))
        alpha = jnp.exp(m_old - m_new)
        pr = jnp.exp(s - jnp.concatenate([m_new] * (s.shape[1] // lanes), axis=1))
        acc_ref[c] = jnp.concatenate([alpha, alpha], axis=1) * acc_ref[c] + _dot(pr.astype(BF16), vext)
        m_ref[c] = m_new

    ones = jnp.ones((P, lanes), BF16)
    mb_step = jnp.concatenate([mb_ref[0, p * PAGES_PER_STEP + r] for r in range(PAGES_PER_STEP)], axis=1)
    mb_step = jnp.concatenate([mb_step] * G, axis=0)
    for c in range(N_KV_HEADS):
        head_rows = pl.ds(c, P, stride=N_KV_HEADS)
        k_all = jnp.concatenate([kp[r][0, head_rows, :].astype(BF16) for r in range(PAGES_PER_STEP)], axis=0)
        v_all = jnp.concatenate(
            [jnp.concatenate([vp[r][0, head_rows, :].astype(BF16), ones], axis=1) for r in range(PAGES_PER_STEP)], axis=0)
        s = _dot_nt(q_ref[0, c], k_all) + mb_step
        s = jnp.concatenate([s[:, :-P], s[:, -P:] + jnp.where(last, bt_ref[c, 0], 0.0)], axis=1)
        attend(c, s, v_all)

    @pl.when(last)
    def _():
        mb_new = jnp.concatenate([mb_ref[0, n_chunks - 1]] * G, axis=0)
        for c in range(N_KV_HEADS):
            cols = slice(c * HEAD_DIM_C, (c + 1) * HEAD_DIM_C)
            s = _dot_nt(q_ref[0, c], kn_ref[0, :, cols]) + mb_new + bt_ref[c, 1]
            attend(c, s, jnp.concatenate([vn_ref[0, :, cols], ones], axis=1))
            a = acc_ref[c]
            o_ref[0, c] = a[:, :lanes] / a[:, lanes:]


def _dsa_sample_attend(pt_flat, rel_bias, q, mb, k_new, v_new, pool_k, pool_v, *, n_pages, past):
    B, kvh, rows, dh = q.shape
    P = pool_k.shape[1] // kvh
    steps = n_pages // PAGES_PER_STEP
    tp = SAMPLE_PAD_T
    t = np.arange(tp)[:, None]
    col = np.arange(P)[None, :]
    assert past - (n_pages - 1) * P >= MAX_DISTANCE
    last_page = _t5_bucket_np(past + t - ((n_pages - 1) * P + col))
    new = _t5_bucket_np(t - col)
    btab = jnp.asarray(np.stack([last_page, new]))

    def page_spec(r):
        return pl.BlockSpec((1, P * kvh, dh), lambda b, p, pt: (pt[b * n_pages + p * PAGES_PER_STEP + r], 0, 0))

    grid_spec = pltpu.PrefetchScalarGridSpec(
        num_scalar_prefetch=1,
        grid=(B, steps),
        in_specs=[
            pl.BlockSpec(memory_space=pltpu.SMEM),
            pl.BlockSpec((1, kvh, rows, dh), lambda b, p, pt: (b, 0, 0, 0)),
            pl.BlockSpec((1,) + mb.shape[1:], lambda b, p, pt: (b, 0, 0, 0)),
            pl.BlockSpec((1,) + k_new.shape[1:], lambda b, p, pt: (b, 0, 0)),
            pl.BlockSpec((1,) + v_new.shape[1:], lambda b, p, pt: (b, 0, 0)),
            pl.BlockSpec((2, tp, P), lambda b, p, pt: (0, 0, 0)),
        ] + [page_spec(r) for r in range(PAGES_PER_STEP)] * 2,
        out_specs=pl.BlockSpec((1, kvh, rows, dh), lambda b, p, pt: (b, 0, 0, 0)),
        scratch_shapes=[
            pltpu.VMEM((kvh, 2, rows, P), F32),
            pltpu.VMEM((kvh, rows, dh), F32),
            pltpu.VMEM((kvh, rows, 2 * dh), F32),
        ],
    )
    return pl.pallas_call(
        _dsa_att_body,
        grid_spec=grid_spec,
        out_shape=jax.ShapeDtypeStruct((B, kvh, rows, dh), F32),
        compiler_params=_cparams(("arbitrary", "arbitrary")),
        name="dsa_sample_attend",
    )(pt_flat, rel_bias, q, mb, k_new, v_new, btab, *([pool_k] * PAGES_PER_STEP), *([pool_v] * PAGES_PER_STEP))


def _dsa_sample(c1, tail, c2b, pt_flat, rel_bias, pool_k, pool_v, pool_ki, *, Bs, Ts, n_pages):
    tp = SAMPLE_PAD_T
    kv_w = N_KV_HEADS * HEAD_DIM_C
    d_c = c1.shape[1] - N_IDX_HEADS * IDX_DIM
    G = d_c // kv_w
    P = pool_ki.shape[2]
    past = n_pages * P
    qi_s = c1[:, d_c:].reshape(Bs, tp * N_IDX_HEADS, IDX_DIM)
    w_s = tail[:, IDX_DIM:IDX_DIM + N_IDX_HEADS].reshape(Bs, tp * N_IDX_HEADS, 1)
    new_rows = jnp.pad(c2b.reshape(Bs, tp, -1), ((0, 0), (0, P - tp), (0, 0)))
    mb = _dsa_sample_select(pt_flat, qi_s, w_s, new_rows[:, :, 2 * kv_w:2 * kv_w + IDX_DIM], pool_ki,
                            n_pages=n_pages, t_new=Ts, n_keep=min(TOPK_MAX, (past + Ts) // 4))
    q_s = c1[:, :d_c].reshape(Bs, tp, N_KV_HEADS, G, HEAD_DIM_C).transpose(0, 2, 3, 1, 4)
    q_s = q_s.reshape(Bs, N_KV_HEADS, G * tp, HEAD_DIM_C)
    o_s = _dsa_sample_attend(pt_flat, rel_bias, q_s, mb, new_rows[:, :, :kv_w], new_rows[:, :, kv_w:2 * kv_w],
                             pool_k, pool_v, n_pages=n_pages, past=past)
    return o_s.reshape(Bs, N_KV_HEADS, G, tp, HEAD_DIM_C).transpose(0, 3, 1, 2, 4).reshape(Bs * tp, d_c).astype(BF16)


def kernel(x_prompt, x_sample, cache_k, cache_v, cache_kidx, page_table, state_lru_h, state_conv, state_ret,
           norm_ffn1, ffn1_gate, ffn1_up, ffn1_down, norm_mix, w_in, w_out, conv_w, conv_b,
           lru_wa, lru_ba, lru_wx, lru_bx, lru_lambda, rel_bias,
           norm_ffn2, ffn2_gate, ffn2_up, ffn2_down, norm_final):
    depth = norm_ffn1.shape[0]
    _, S, D = x_prompt.shape
    Bs, Ts, _ = x_sample.shape
    n_pool, P = cache_k.shape[1], cache_k.shape[2]
    n_pages = page_table.shape[1]
    past = n_pages * P
    d_a = state_lru_h.shape[-1]
    H_r, rdk, rdv = state_ret.shape[2:]
    d_b = H_r * rdv
    d_c = D - d_a - d_b
    kv_w = N_KV_HEADS * HEAD_DIM_C
    qi_w = N_IDX_HEADS * IDX_DIM
    G = d_c // HEAD_DIM_C // N_KV_HEADS
    tp = SAMPLE_PAD_T
    Rs = Bs * tp

    xp = x_prompt.reshape(S, D)
    xs = jnp.pad(x_sample, ((0, 0), (0, tp - Ts), (0, 0))).reshape(Rs, D)

    cos_p, sin_p = _rope_tables(jnp.arange(S, dtype=I32), rdk, H_r)
    cos_s, sin_s = _rope_tables(past + jnp.arange(tp, dtype=I32), rdk, H_r)

    pool_k = cache_k.reshape(depth * n_pool, P * N_KV_HEADS, HEAD_DIM_C)
    pool_v = cache_v.reshape(depth * n_pool, P * N_KV_HEADS, HEAD_DIM_C)
    pool_ki = jnp.swapaxes(cache_kidx, 2, 3).reshape(depth * n_pool, IDX_DIM, P)

    o = np.cumsum([0, 2 * d_a, 2 * H_r * rdk + 2 * d_b, d_c, 2 * kv_w, qi_w, IDX_DIM + N_IDX_HEADS])
    proj_cols = tuple(int(c) for c in o[:6])
    w_in_b = w_in.astype(BF16)
    w_in_tail = jnp.pad(w_in_b[:, :, o[5]:o[6]], ((0, 0), (0, 0), (0, 128 - int(o[6] - o[5]))))

    zeros_conv = jnp.zeros((1, 8, d_a), F32)
    zeros_h = jnp.zeros((1, 1, d_a), F32)
    zeros_s = jnp.zeros((1, H_r, rdk, rdv), F32)

    outs_p, outs_s = [], []
    y_prompt = y_sample = None
    for l in range(depth):
        in_proj = functools.partial(_in_proj, g=norm_mix[l], w_all=w_in_b, w_tail=w_in_tail, layer=l,
                                    cols=proj_cols, q_scale=HEAD_DIM_C ** -0.5)
        wo = w_out[l].astype(BF16)
        lwa, lwx = lru_wa[l].astype(BF16), lru_wx[l].astype(BF16)
        last = l == depth - 1
        pt_flat = (page_table + l * n_pool).reshape(-1).astype(I32)

        xs, *f1 = _ffn_cast(xs, norm_ffn1[l], ffn1_gate, ffn1_up, ffn1_down, l, tf=512)
        ua, ub, c1, k_new, v_new, ki_new, tail, c2b = in_proj(xs, tm=Rs)
        conv0 = jnp.pad(state_conv[l], ((0, 0), (8 - (CONV_WIDTH - 1), 0), (0, 0)))
        ya, h_last, conv_new = _lru(ua.reshape(Bs, tp, 2 * d_a), conv0, state_lru_h[l].reshape(Bs, 1, d_a),
                                    conv_w[l], conv_b[l], lwa, lru_ba[l], lwx, lru_bx[l], lru_lambda[l],
                                    tc=tp, t_valid_last=Ts)
        yb, s_last = _retention(ub.reshape(Bs, tp, -1), cos_s, sin_s, state_ret[l], cp=tp, c_valid=Ts, mm_dtype=F32)
        yc = _dsa_sample(c1, tail, c2b, pt_flat, rel_bias, pool_k, pool_v, pool_ki, Bs=Bs, Ts=Ts, n_pages=n_pages)
        xs = _out_proj(xs, ya.reshape(Rs, d_a), yb.reshape(Rs, d_b), yc, wo, tm=Rs)
        if last:
            xs, y_sample, *f2 = _ffn_cast(xs, norm_ffn2[l], ffn2_gate, ffn2_up, ffn2_down, l, norm_final, tf=512)
        else:
            xs, *f2 = _ffn_cast(xs, norm_ffn2[l], ffn2_gate, ffn2_up, ffn2_down, l, tf=512)
        valid = lambda a: a.reshape(Bs, tp, -1)[:, :Ts]
        outs_s.append((valid(k_new).reshape(Bs, Ts, N_KV_HEADS, HEAD_DIM_C),
                       valid(v_new).reshape(Bs, Ts, N_KV_HEADS, HEAD_DIM_C),
                       valid(ki_new),
                       h_last.reshape(Bs, d_a), conv_new, s_last))

        xp = _ffn(xp, norm_ffn1[l], *f1, tm=1024, tf=256)
        ua, ub, c1, k_new, v_new, ki_new, tail, c2b = in_proj(xp, tm=256)
        ya, h_last, conv_new = _lru(ua.reshape(1, S, 2 * d_a), zeros_conv, zeros_h, conv_w[l], conv_b[l],
                                    lwa, lru_ba[l], lwx, lru_bx[l], lru_lambda[l], tc=1024, t_valid_last=1024)
        yb, s_last = _retention(ub.reshape(1, S, -1), cos_p, sin_p, zeros_s, cp=128, c_valid=128, mm_dtype=BF16,
                                chunks_per_step=4)
        yc = _dsa_prompt(c1, tail, c2b, rel_bias, tq=256, n_keep=min(TOPK_MAX, S // 4))
        xp = _out_proj(xp, ya.reshape(S, d_a), yb.reshape(S, d_b), yc, wo, tm=512)
        if last:
            y_prompt = _ffn(xp, norm_ffn2[l], *f2, norm_final, tm=1024, tf=256)
        else:
            xp = _ffn(xp, norm_ffn2[l], *f2, tm=1024, tf=256)
        outs_p.append((k_new.reshape(1, S, N_KV_HEADS, HEAD_DIM_C),
                       v_new.reshape(1, S, N_KV_HEADS, HEAD_DIM_C),
                       ki_new.reshape(1, S, IDX_DIM),
                       h_last.reshape(1, d_a), conv_new, s_last))

    stack = lambda outs, k: jnp.stack([o[k] for o in outs])
    return ((y_prompt.reshape(1, S, D), y_sample.reshape(Bs, tp, D)[:, :Ts])
            + tuple(stack(outs_p, k) for k in range(6))
            + tuple(stack(outs_s, k) for k in range(6)))
```
